```python
import jax, jax.numpy as jnp
from jax import lax
import numpy as np

D_MODEL = 2048
BATCH = 4
SEQ = 4096
DEPTH = 2

GRID_W = 64
CTX_LEN = 256
CHUNK = 64
EPS = 1e-5
ROPE_BASE = 10000.0

D_MIX = D_MODEL
N_GROUPS = 4
GROUP_W = D_MIX // N_GROUPS

M_HEADS = 4
M_DH = GROUP_W // M_HEADS
S_HEADS = 8
S_P = GROUP_W // S_HEADS
S_G = 2
S_N = 128
CONV_W = 3
S_CONV_CH = GROUP_W + 2 * S_G * S_N
R_HEADS = 4
R_DH = GROUP_W // R_HEADS
G_HEADS = 4
G_DV = GROUP_W // G_HEADS
G_DK = G_DV // 2
G_RANK = 16
G_TAU = 16.0
D_FF = -(-(8 * D_MODEL) // (3 * 256)) * 256

IN_COLS = (
    ('m_q', M_HEADS * M_DH), ('m_k', M_HEADS * M_DH), ('m_v', M_HEADS * M_DH), ('m_o', GROUP_W),
    ('m_i', 2 * M_HEADS), ('m_f', 2 * M_HEADS),
    ('s_z', GROUP_W), ('s_xbc', S_CONV_CH), ('s_dt', 2 * S_HEADS),
    ('r_q', GROUP_W), ('r_k', GROUP_W), ('r_v', GROUP_W), ('r_g', GROUP_W),
    ('g_q', G_HEADS * G_DK), ('g_k', G_HEADS * G_DK), ('g_v', G_HEADS * G_DV), ('g_g', GROUP_W),
    ('g_a', 2 * G_RANK),
)
N_IN = sum(n for _, n in IN_COLS)

kernel_name = 'hybrid_parallel_heads_dit_block'


def layer_norm(x, gain=None, bias=None):
    x32 = x.astype(jnp.float32)
    mu = jnp.mean(x32, axis=-1, keepdims=True)
    var = jnp.mean(jnp.square(x32 - mu), axis=-1, keepdims=True)
    y = (x32 - mu) * lax.rsqrt(var + EPS)
    if gain is not None:
        y = y * gain + bias
    return y.astype(x.dtype)


def rms_norm(x, gain):
    x32 = x.astype(jnp.float32)
    y = x32 * lax.rsqrt(jnp.mean(jnp.square(x32), axis=-1, keepdims=True) + EPS)
    return (y * gain).astype(x.dtype)


def head_norm(y, gain, center):
    y32 = y.astype(jnp.float32)
    if center:
        y32 = y32 - jnp.mean(y32, axis=-1, keepdims=True)
    y32 = y32 * lax.rsqrt(jnp.mean(jnp.square(y32), axis=-1, keepdims=True) + EPS)
    return (y32.reshape(*y.shape[:2], -1) * gain).astype(y.dtype)


def modulate(x, shift, scale):
    return layer_norm(x) * (1.0 + scale) + shift


def _heads(a, h):
    return a.reshape(*a.shape[:-1], h, -1)


def split_cols(u):
    sizes = [n for _, n in IN_COLS]
    parts = jnp.split(u, np.cumsum(sizes)[:-1].tolist(), axis=-1)
    return {name: p for (name, _), p in zip(IN_COLS, parts)}


def dwconv_centred(x, w, b):
    k = w.shape[0]
    pad = k // 2
    length = x.shape[1]
    xp = jnp.pad(x, ((0, 0), (pad, pad), (0, 0)))
    out = b
    for i in range(k):
        out = out + xp[:, i:i + length] * w[i]
    return out


def rotary_2d(x, rows, cols):
    half = x.shape[-1] // 2
    quarter = half // 2
    freqs = 1.0 / (ROPE_BASE ** (jnp.arange(quarter, dtype=jnp.float32) / quarter))

    def rot(xh, pos):
        ang = pos[:, None] * freqs[None, :]
        cos = jnp.cos(ang)[None, :, None, :].astype(x.dtype)
        sin = jnp.sin(ang)[None, :, None, :].astype(x.dtype)
        x1, x2 = xh[..., :quarter], xh[..., quarter:]
        return jnp.concatenate([x1 * cos - x2 * sin, x1 * sin + x2 * cos], axis=-1)

    return jnp.concatenate([rot(x[..., :half], rows), rot(x[..., half:], cols)], axis=-1)


def _chunks(a):
    b, t = a.shape[:2]
    return jnp.moveaxis(a.reshape(b, t // CHUNK, CHUNK, *a.shape[2:]), 1, 0)


def _unchunk(a):
    n, b = a.shape[:2]
    return jnp.moveaxis(a, 0, 1).reshape(b, n * CHUNK, *a.shape[3:])


def _causal_mask():
    return jnp.tril(jnp.ones((CHUNK, CHUNK), dtype=bool))[None, :, :, None]


def mlstm_scan(q, k, v, ig, lf):
    dtype = v.dtype
    q, k, v, ig, lf = (a.astype(jnp.float32) for a in (q, k, v, ig, lf))
    b_sz, _, h_sz, dk = q.shape
    dv = v.shape[-1]
    causal = _causal_mask()

    def step(carry, inp):
        c_st, n_st, m_st = carry
        qb, kb, vb, igb, lfb = inp
        b = jnp.cumsum(lfb, axis=1)
        log_intra = jnp.where(causal, b[:, :, None] - b[:, None] + igb[:, None], -jnp.inf)
        log_inter = b + m_st[:, None]
        m_i = jnp.maximum(log_inter, jnp.max(log_intra, axis=2))
        w = jnp.exp(log_intra - m_i[:, :, None])
        w_inter = jnp.exp(log_inter - m_i)
        s = jnp.einsum('bihd,bjhd->bijh', qb, kb) * w
        num = jnp.einsum('bijh,bjhe->bihe', s, vb) + w_inter[..., None] * jnp.einsum('bihd,bhde->bihe', qb, c_st)
        den = jnp.sum(s, axis=2) + w_inter * jnp.einsum('bihd,bhd->bih', qb, n_st)
        h = num / jnp.maximum(jnp.abs(den), jnp.exp(-m_i))[..., None]
        b_end = b[:, -1]
        log_state = b_end[:, None] - b + igb
        m_new = jnp.maximum(b_end + m_st, jnp.max(log_state, axis=1))
        ws = jnp.exp(log_state - m_new[:, None])
        decay = jnp.exp(b_end + m_st - m_new)
        c_st = decay[..., None, None] * c_st + jnp.einsum('bjhd,bjhe->bhde', kb * ws[..., None], vb)
        n_st = decay[..., None] * n_st + jnp.einsum('bjhd,bjh->bhd', kb, ws)
        return (c_st, n_st, m_new), h

    init = (jnp.zeros((b_sz, h_sz, dk, dv), jnp.float32), jnp.zeros((b_sz, h_sz, dk), jnp.float32),
            jnp.zeros((b_sz, h_sz), jnp.float32))
    _, hs = lax.scan(step, init, tuple(_chunks(a) for a in (q, k, v, ig, lf)))
    return _unchunk(hs).astype(dtype)


def decay_scan(q, k, v, log_a):
    dtype = v.dtype
    q, k, v, log_a = (a.astype(jnp.float32) for a in (q, k, v, log_a))
    b_sz, _, h_sz, dk = q.shape
    dv = v.shape[-1]
    causal = _causal_mask()

    def step(s_st, inp):
        qb, kb, vb, la = inp
        b = jnp.cumsum(la, axis=1)
        w = jnp.exp(jnp.where(causal, b[:, :, None] - b[:, None], -jnp.inf))
        s = jnp.einsum('bihd,bjhd->bijh', qb, kb) * w
        y = jnp.einsum('bijh,bjhe->bihe', s, vb) + jnp.exp(b)[..., None] * jnp.einsum('bihd,bhde->bihe', qb, s_st)
        b_end = b[:, -1]
        s_st = jnp.exp(b_end)[..., None, None] * s_st + jnp.einsum(
            'bjhd,bjhe->bhde', kb * jnp.exp(b_end[:, None] - b)[..., None], vb)
        return s_st, y

    init = jnp.zeros((b_sz, h_sz, dk, dv), jnp.float32)
    _, ys = lax.scan(step, init, tuple(_chunks(a) for a in (q, k, v, log_a)))
    return _unchunk(ys).astype(dtype)


def gla_scan(q, k, v, log_a):
    dtype = v.dtype
    q, k, v, log_a = (a.astype(jnp.float32) for a in (q, k, v, log_a))
    b_sz, _, h_sz, dk = q.shape
    dv = v.shape[-1]
    causal = _causal_mask()

    def step(s_st, inp):
        qb, kb, vb, la = inp
        b = jnp.cumsum(la, axis=1)
        qg = qb * jnp.exp(b)
        kg = kb * jnp.exp(-b)
        s = jnp.where(causal, jnp.einsum('bihd,bjhd->bijh', qg, kg), 0.0)
        y = jnp.einsum('bijh,bjhe->bihe', s, vb) + jnp.einsum('bihd,bhde->bihe', qg, s_st)
        b_end = b[:, -1]
        s_st = jnp.exp(b_end)[..., None] * s_st + jnp.einsum(
            'bjhd,bjhe->bhde', kb * jnp.exp(b_end[:, None] - b), vb)
        return s_st, y

    init = jnp.zeros((b_sz, h_sz, dk, dv), jnp.float32)
    _, ys = lax.scan(step, init, tuple(_chunks(a) for a in (q, k, v, log_a)))
    return _unchunk(ys).astype(dtype)


def _orient(a, backward):
    return jnp.flip(a, axis=1) if backward else a


def two_way(scan_fn, ctx_args, lat_args):
    lc = ctx_args[0][0].shape[1]
    outs = []
    for d in range(2):
        bwd = d == 1
        seq = tuple(jnp.concatenate([_orient(a, bwd), _orient(b, bwd)], axis=1)
                    for a, b in zip(ctx_args[d], lat_args[d]))
        y = scan_fn(*seq)
        outs.append((_orient(y[:, :lc], bwd), _orient(y[:, lc:], bwd)))
    return outs[0][0] + outs[1][0], outs[0][1] + outs[1][1]


def mlstm_group(pc, pl, ig_b, fg_b, norm_g):
    def prep(p):
        b_sz, length, _ = p['m_q'].shape
        q = _heads(p['m_q'], M_HEADS) * M_DH ** -0.5
        k = _heads(p['m_k'], M_HEADS)
        v = _heads(p['m_v'], M_HEADS)
        ig = p['m_i'].reshape(b_sz, length, 2, M_HEADS) + ig_b
        lf = jax.nn.log_sigmoid(p['m_f'].reshape(b_sz, length, 2, M_HEADS) + fg_b)
        return [(q, k, v, ig[:, :, d], lf[:, :, d]) for d in range(2)]

    h_c, h_l = two_way(mlstm_scan, prep(pc), prep(pl))
    return (jax.nn.sigmoid(pc['m_o']) * head_norm(h_c, norm_g, True),
            jax.nn.sigmoid(pl['m_o']) * head_norm(h_l, norm_g, True))


def ssd_group(pc, pl, conv_w, conv_b, dt_bias, a_log, d_skip, norm_g):
    a_neg = -jnp.exp(a_log)

    def prep(p):
        b_sz, length, _ = p['s_z'].shape
        xbc = jax.nn.silu(dwconv_centred(p['s_xbc'], conv_w, conv_b))
        xs, bm, cm = jnp.split(xbc, [GROUP_W, GROUP_W + S_G * S_N], axis=-1)
        xs = _heads(xs, S_HEADS)
        bm = jnp.repeat(_heads(bm, S_G), S_HEADS // S_G, axis=2)
        cm = jnp.repeat(_heads(cm, S_G), S_HEADS // S_G, axis=2)
        dt = jax.nn.softplus(p['s_dt'].reshape(b_sz, length, 2, S_HEADS) + dt_bias)
        return xs, [(cm, bm * dt[:, :, d, :, None], xs, dt[:, :, d] * a_neg[d]) for d in range(2)]

    xs_c, args_c = prep(pc)
    xs_l, args_l = prep(pl)
    y_c, y_l = two_way(decay_scan, args_c, args_l)

    def out(y, xs, p):
        y = (y + d_skip[:, None] * xs).reshape(*xs.shape[:2], GROUP_W) * jax.nn.silu(p['s_z'])
        return rms_norm(y, norm_g)

    return out(y_c, xs_c, pc), out(y_l, xs_l, pl)


def retention_group(pc, pl, rows, cols, r_decay, norm_g):
    log_gamma = -jnp.exp(r_decay)

    def prep(p, pos):
        q = _heads(p['r_q'], R_HEADS)
        k = _heads(p['r_k'], R_HEADS) * R_DH ** -0.5
        if pos is not None:
            q = rotary_2d(q, *pos)
            k = rotary_2d(k, *pos)
        v = _heads(p['r_v'], R_HEADS)
        return [(q, k, v, jnp.broadcast_to(log_gamma[d], q.shape[:3])) for d in range(2)]

    y_c, y_l = two_way(decay_scan, prep(pc, None), prep(pl, (rows, cols)))
    return (head_norm(y_c, norm_g, True) * jax.nn.silu(pc['r_g']),
            head_norm(y_l, norm_g, True) * jax.nn.silu(pl['r_g']))


def gla_group(pc, pl, gate_w, gate_b, norm_g):
    def prep(p):
        b_sz, length, _ = p['g_q'].shape
        q = _heads(p['g_q'], G_HEADS) * G_DK ** -0.5
        k = _heads(p['g_k'], G_HEADS)
        v = _heads(p['g_v'], G_HEADS)
        a = p['g_a'].reshape(b_sz, length, 2, G_RANK)
        la = jax.nn.log_sigmoid(jnp.einsum('blnr,nrk->blnk', a, gate_w) + gate_b) / G_TAU
        return [(q, k, v, _heads(la[:, :, d], G_HEADS)) for d in range(2)]

    y_c, y_l = two_way(gla_scan, prep(pc), prep(pl))
    return (head_norm(y_c, norm_g, False) * jax.nn.silu(pc['g_g']),
            head_norm(y_l, norm_g, False) * jax.nn.silu(pl['g_g']))


def token_mixers(h, hc, rows, cols, w_in, w_out, m_ig_b, m_fg_b, m_norm, s_conv_w, s_conv_b,
                 s_dt_bias, s_a_log, s_d, s_norm, r_decay, r_norm, g_gate_w, g_gate_b, g_norm):
    pl = split_cols(h @ w_in)
    pc = split_cols(hc @ w_in)
    m_c, m_l = mlstm_group(pc, pl, m_ig_b, m_fg_b, m_norm)
    s_c, s_l = ssd_group(pc, pl, s_conv_w, s_conv_b, s_dt_bias, s_a_log, s_d, s_norm)
    r_c, r_l = retention_group(pc, pl, rows, cols, r_decay, r_norm)
    g_c, g_l = gla_group(pc, pl, g_gate_w, g_gate_b, g_norm)
    y_l = jnp.concatenate([m_l, s_l, r_l, g_l], axis=-1) @ w_out
    y_c = jnp.concatenate([m_c, s_c, r_c, g_c], axis=-1) @ w_out
    return y_l, y_c


def swiglu(h, w_up, w_down):
    a, g = jnp.split(h @ w_up, 2, axis=-1)
    return (jax.nn.silu(a) * g) @ w_down


def setup_inputs(seed: int = 0) -> dict:
    key = jax.random.key(seed)
    ks = iter(jax.random.split(key, 40))

    def nrm(shape, scale):
        return scale * jax.random.normal(next(ks), shape, jnp.float32)

    def uni(shape, lo, hi):
        return jax.random.uniform(next(ks), shape, jnp.float32, lo, hi)

    beta = (8.0 * DEPTH) ** -0.25
    dt0 = jnp.exp(uni((DEPTH, 2, S_HEADS), float(np.log(1e-3)), float(np.log(1e-1))))
    gamma = 1.0 - 2.0 ** (-5.0 - np.arange(R_HEADS, dtype=np.float32))
    r_decay0 = jnp.asarray(np.log(-np.log(gamma)), jnp.float32)
    return {
        'x': nrm((BATCH, SEQ, D_MODEL), 1.0),
        'c': nrm((BATCH, D_MODEL), 1.0),
        'ctx': nrm((BATCH, CTX_LEN, D_MODEL), 1.0),
        'c_ctx': nrm((D_MODEL,), 1.0),
        'ada_w': nrm((DEPTH, D_MODEL, 6 * D_MODEL), 0.5 * D_MODEL ** -0.5),
        'ada_b': nrm((DEPTH, 6 * D_MODEL), 0.02),
        'w_in': nrm((DEPTH, D_MODEL, N_IN), D_MODEL ** -0.5),
        'm_ig_b': -2.0 + nrm((DEPTH, 2, M_HEADS), 0.1),
        'm_fg_b': jnp.linspace(3.0, 6.0, M_HEADS, dtype=jnp.float32) + nrm((DEPTH, 2, M_HEADS), 0.1),
        'm_norm': 1.0 + nrm((DEPTH, GROUP_W), 0.02),
        's_conv_w': nrm((DEPTH, CONV_W, S_CONV_CH), CONV_W ** -0.5),
        's_conv_b': nrm((DEPTH, S_CONV_CH), 0.02),
        's_dt_bias': dt0 + jnp.log(-jnp.expm1(-dt0)),
        's_a_log': jnp.log(uni((DEPTH, 2, S_HEADS), 1.0, 16.0)),
        's_d': 1.0 + nrm((DEPTH, S_HEADS), 0.02),
        's_norm': 1.0 + nrm((DEPTH, GROUP_W), 0.02),
        'r_decay': r_decay0 + nrm((DEPTH, 2, R_HEADS), 0.05),
        'r_norm': 1.0 + nrm((DEPTH, GROUP_W), 0.02),
        'g_gate_w': nrm((DEPTH, 2, G_RANK, G_HEADS * G_DK), G_RANK ** -0.5),
        'g_gate_b': nrm((DEPTH, 2, G_HEADS * G_DK), 0.1),
        'g_norm': 1.0 + nrm((DEPTH, GROUP_W), 0.02),
        'w_out': nrm((DEPTH, D_MIX, D_MODEL), beta * D_MIX ** -0.5),
        'post_g': 1.0 + nrm((DEPTH, 2, D_MODEL), 0.02),
        'post_b': nrm((DEPTH, 2, D_MODEL), 0.02),
        'ffn_w_up': nrm((DEPTH, D_MODEL, 2 * D_FF), D_MODEL ** -0.5),
        'ffn_w_down': nrm((DEPTH, D_FF, D_MODEL), beta * D_FF ** -0.5),
    }


def reference(x, c, ctx, c_ctx, ada_w, ada_b, w_in, m_ig_b, m_fg_b, m_norm, s_conv_w, s_conv_b,
              s_dt_bias, s_a_log, s_d, s_norm, r_decay, r_norm, g_gate_w, g_gate_b, g_norm, w_out,
              post_g, post_b, ffn_w_up, ffn_w_down):
    alpha = (2.0 * DEPTH) ** 0.25
    n_lat = x.shape[1]
    ROWS = n_lat // GRID_W
    rows = jnp.repeat(jnp.arange(ROWS, dtype=jnp.float32), GRID_W)
    cols = jnp.tile(jnp.arange(GRID_W, dtype=jnp.float32), ROWS)
    xc = ctx
    for l in range(DEPTH):
        last = l == DEPTH - 1
        mod = jax.nn.silu(c) @ ada_w[l] + ada_b[l]
        mod_c = jax.nn.silu(c_ctx) @ ada_w[l] + ada_b[l]
        sh1, sc1, g1, sh2, sc2, g2 = jnp.split(mod[:, None, :], 6, axis=-1)
        sh1c, sc1c, g1c, sh2c, sc2c, g2c = jnp.split(mod_c, 6, axis=-1)

        h = modulate(x, sh1, sc1)
        hc = modulate(xc, sh1c, sc1c)
        y, yc = token_mixers(h, hc, rows, cols, w_in[l], w_out[l], m_ig_b[l], m_fg_b[l], m_norm[l],
                             s_conv_w[l], s_conv_b[l], s_dt_bias[l], s_a_log[l], s_d[l], s_norm[l],
                             r_decay[l], r_norm[l], g_gate_w[l], g_gate_b[l], g_norm[l])
        x = layer_norm(alpha * x + g1 * y, post_g[l, 0], post_b[l, 0])
        x = layer_norm(alpha * x + g2 * swiglu(modulate(x, sh2, sc2), ffn_w_up[l], ffn_w_down[l]),
                       post_g[l, 1], post_b[l, 1])
        if not last:
            xc = layer_norm(alpha * xc + g1c * yc, post_g[l, 0], post_b[l, 0])
            xc = layer_norm(alpha * xc + g2c * swiglu(modulate(xc, sh2c, sc2c), ffn_w_up[l], ffn_w_down[l]),
                            post_g[l, 1], post_b[l, 1])
    return x
```

```python
import functools
import math

import numpy as np
import jax
import jax.numpy as jnp
from jax import lax
from jax.experimental import pallas as pl
from jax.experimental.pallas import tpu as pltpu

F32 = jnp.float32
BF16 = jnp.bfloat16

EPS = 1e-5
ROPE_BASE = 10000.0
GRID_W = 64
N_GROUPS = 4
M_HEADS = 4
S_HEADS = 8
S_G = 2
S_N = 128
CONV_W = 3
R_HEADS = 4
G_HEADS = 4
G_RANK = 16
G_TAU = 16.0
NEG = -1e30

V7X_VMEM_BYTES = 64 * 1024 * 1024
LANES = 128
GATE_W = LANES
MAX_TOKEN_BLOCK = 256
GLA_CHUNK = 64


def _cparams(sem, vmem_mb):
    return pltpu.CompilerParams(dimension_semantics=sem, vmem_limit_bytes=vmem_mb * 1024 * 1024)


def _dot(a, b):
    return jnp.dot(a, b, preferred_element_type=F32)


def _dot_nt(a, b):
    return lax.dot_general(a, b, (((1,), (1,)), ((), ())), preferred_element_type=F32)


def _dot_tn(a, b):
    return lax.dot_general(a, b, (((0,), (0,)), ((), ())), preferred_element_type=F32)


def _split3(x):
    hi = x.astype(BF16)
    r1 = x - hi.astype(F32)
    mid = r1.astype(BF16)
    lo = (r1 - mid.astype(F32)).astype(BF16)
    return hi, mid, lo


def _tri_cumsum(tri, x):
    hi, mid, lo = _split3(x)
    return _dot(tri, hi) + _dot(tri, mid) + _dot(tri, lo)


def _dot_f32(a, b):
    ah = a.astype(BF16)
    al = (a - ah.astype(F32)).astype(BF16)
    bh = b.astype(BF16)
    bl = (b - bh.astype(F32)).astype(BF16)
    return _dot(ah, bh) + _dot(ah, bl) + _dot(al, bh)


def _sigmoid(x):
    return 1.0 / (1.0 + jnp.exp(-x))


def _silu(x):
    return x * _sigmoid(x)


def _softplus(x):
    return jnp.maximum(x, 0.0) + jnp.log1p(jnp.exp(-jnp.abs(x)))


def _log_sigmoid(x):
    return jnp.minimum(x, 0.0) - jnp.log1p(jnp.exp(-jnp.abs(x)))


def _ln(x):
    mu = jnp.mean(x, axis=-1, keepdims=True)
    xc = x - mu
    var = jnp.mean(xc * xc, axis=-1, keepdims=True)
    return xc * lax.rsqrt(var + EPS)


def _causal(n, rev):
    row = lax.broadcasted_iota(jnp.int32, (n, n), 0)
    col = lax.broadcasted_iota(jnp.int32, (n, n), 1)
    return (col >= row) if rev else (col <= row)


def _expand4(cols, lane_head):
    return jnp.where(lane_head == 0, cols[0],
                     jnp.where(lane_head == 1, cols[1],
                               jnp.where(lane_head == 2, cols[2], cols[3])))


def _mod_kernel(c_ref, w_ref, b_ref, o_ref):
    o_ref[0] = _dot_f32(_silu(c_ref[...]), w_ref[0]) + b_ref[0]


def _mod_call(cc, ada_w, ada_b):
    depth, d, n = ada_w.shape
    tn = 1024
    return pl.pallas_call(
        _mod_kernel,
        out_shape=jax.ShapeDtypeStruct((depth, 8, n), F32),
        grid=(depth, n // tn),
        in_specs=[pl.BlockSpec((8, d), lambda l, j: (0, 0)),
                  pl.BlockSpec((1, d, tn), lambda l, j: (l, 0, j)),
                  pl.BlockSpec((1, 1, tn), lambda l, j: (l, 0, j))],
        out_specs=pl.BlockSpec((1, 8, tn), lambda l, j: (l, 0, j)),
        compiler_params=_cparams(("parallel", "parallel"), 40),
        name="adaln_mod",
    )(cc, ada_w, ada_b.reshape(depth, 1, n))


def _ln_mod_to(h_ref, x_ref, mods_ref, nsub, tb, sh_row, sc_row):
    for r in range(nsub):
        x = x_ref[r * tb:(r + 1) * tb, :]
        h = _ln(x) * (1.0 + mods_ref[r, sc_row:sc_row + 1, :]) + mods_ref[r, sh_row:sh_row + 1, :]
        h_ref[r * tb:(r + 1) * tb, :] = h.astype(BF16)


def _in_kernel(x_ref, mods_ref, wm_ref, wg_ref, um_ref, ug_ref, h_ref, *, nsub, tb):
    @pl.when(pl.program_id(1) == 0)
    def _():
        _ln_mod_to(h_ref, x_ref, mods_ref, nsub, tb, 0, 1)
        ug_ref[...] = _dot(h_ref[...], wg_ref[...])

    um_ref[...] = _dot(h_ref[...], wm_ref[...]).astype(BF16)


def _in_call(xs, mods, wm, wg, tm, tb):
    t, d = xs.shape
    n = wm.shape[1]
    tn = 1024
    nsub = tm // tb
    return pl.pallas_call(
        functools.partial(_in_kernel, nsub=nsub, tb=tb),
        out_shape=(jax.ShapeDtypeStruct((t, n), BF16), jax.ShapeDtypeStruct((t, GATE_W), F32)),
        grid=(t // tm, n // tn),
        in_specs=[pl.BlockSpec((tm, d), lambda i, j: (i, 0)),
                  pl.BlockSpec((nsub, 8, d), lambda i, j: (i, 0, 0)),
                  pl.BlockSpec((d, tn), lambda i, j: (0, j)),
                  pl.BlockSpec((d, GATE_W), lambda i, j: (0, 0))],
        out_specs=(pl.BlockSpec((tm, tn), lambda i, j: (i, j)),
                   pl.BlockSpec((tm, GATE_W), lambda i, j: (i, 0))),
        scratch_shapes=[pltpu.VMEM((tm, d), BF16)],
        compiler_params=_cparams(("parallel", "arbitrary"), 48),
        name="ln_in_proj",
    )(xs, mods, wm, wg)


def _up_kernel(x_ref, mods_ref, wa_ref, wg_ref, o_ref, h_ref, *, nsub, tb):
    @pl.when(pl.program_id(1) == 0)
    def _():
        _ln_mod_to(h_ref, x_ref, mods_ref, nsub, tb, 3, 4)

    a = _dot(h_ref[...], wa_ref[...])
    g = _dot(h_ref[...], wg_ref[...])
    o_ref[...] = (_silu(a) * g).astype(BF16)


def _up_call(xs, mods, w_up, tm, tb):
    t, d = xs.shape
    dff = w_up.shape[1] // 2
    tn = 512
    nj = dff // tn
    nsub = tm // tb
    return pl.pallas_call(
        functools.partial(_up_kernel, nsub=nsub, tb=tb),
        out_shape=jax.ShapeDtypeStruct((t, dff), BF16),
        grid=(t // tm, nj),
        in_specs=[pl.BlockSpec((tm, d), lambda i, j: (i, 0)),
                  pl.BlockSpec((nsub, 8, d), lambda i, j: (i, 0, 0)),
                  pl.BlockSpec((d, tn), lambda i, j: (0, j)),
                  pl.BlockSpec((d, tn), lambda i, j: (0, j + nj))],
        out_specs=pl.BlockSpec((tm, tn), lambda i, j: (i, j)),
        scratch_shapes=[pltpu.VMEM((tm, d), BF16)],
        compiler_params=_cparams(("parallel", "arbitrary"), 48),
        name="ln_ffn_up",
    )(xs, mods, w_up, w_up)


def _res_kernel(*refs, n_in, nk, nsub, tb, g_row, alpha):
    a_refs = refs[:n_in]
    w_refs = refs[n_in:2 * n_in]
    x_ref, mods_ref, pg_ref, pb_ref, o_ref, acc_ref = refs[2 * n_in:]
    k = pl.program_id(1)
    part = _dot(a_refs[0][...], w_refs[0][...])
    for a_ref, w_ref in zip(a_refs[1:], w_refs[1:]):
        part = part + _dot(a_ref[...], w_ref[...])

    @pl.when(k == 0)
    def _():
        acc_ref[...] = part

    @pl.when(k > 0)
    def _():
        acc_ref[...] += part

    @pl.when(k == nk - 1)
    def _():
        for r in range(nsub):
            rows = slice(r * tb, (r + 1) * tb)
            z = alpha * x_ref[rows, :] + mods_ref[r, g_row:g_row + 1, :] * acc_ref[rows, :]
            o_ref[rows, :] = _ln(z) * pg_ref[...] + pb_ref[...]


def _res_call(acts, w, w_row_blocks, tk, xs, mods, pg, pb, tm, tb, g_row, alpha, name):
    t, d = xs.shape
    n_in = len(acts)
    ka = acts[0].shape[1]
    nk = ka // tk
    nsub = tm // tb
    in_specs = [pl.BlockSpec((tm, tk), lambda i, k: (i, k)) for _ in acts]
    in_specs += [pl.BlockSpec((tk, d), functools.partial(lambda i, k, r0: (r0 + k, 0), r0=r0))
                 for r0 in w_row_blocks]
    in_specs += [pl.BlockSpec((tm, d), lambda i, k: (i, 0)),
                 pl.BlockSpec((nsub, 8, d), lambda i, k: (i, 0, 0)),
                 pl.BlockSpec((1, d), lambda i, k: (0, 0)),
                 pl.BlockSpec((1, d), lambda i, k: (0, 0))]
    return pl.pallas_call(
        functools.partial(_res_kernel, n_in=n_in, nk=nk, nsub=nsub, tb=tb, g_row=g_row, alpha=alpha),
        out_shape=jax.ShapeDtypeStruct((t, d), F32),
        grid=(t // tm, nk),
        in_specs=in_specs,
        out_specs=pl.BlockSpec((tm, d), lambda i, k: (i, 0)),
        scratch_shapes=[pltpu.VMEM((tm, d), F32)],
        compiler_params=_cparams(("parallel", "arbitrary"), 48),
        name=name,
    )(*acts, *([w] * n_in), xs, mods, pg, pb)


def _prep_kernel(cur_ref, prev_ref, next_ref, rqk_ref, cw_ref, cb_ref, rope_ref, o_ref, *, nctx, nblk, tb):
    t = pl.program_id(1)
    seg_start = jnp.logical_or(t == 0, t == nctx)
    seg_end = jnp.logical_or(t == nctx - 1, t == nblk - 1)
    x = cur_ref[...].astype(F32)
    nc = x.shape[1]
    prev_row = jnp.where(seg_start, 0.0, prev_ref[...].astype(F32)[15:16, :])
    next_row = jnp.where(seg_end, 0.0, next_ref[...].astype(F32)[0:1, :])
    ridx = lax.broadcasted_iota(jnp.int32, (tb, 1), 0)
    xp = jnp.where(ridx == 0, prev_row, pltpu.roll(x, 1, 0))
    xn = jnp.where(ridx == tb - 1, next_row, pltpu.roll(x, tb - 1, 0))
    y = cb_ref[...] + xp * cw_ref[0:1, :] + x * cw_ref[1:2, :] + xn * cw_ref[2:3, :]
    o_ref[:, 0:nc] = _silu(y).astype(BF16)

    cos = rope_ref[:, 0:LANES]
    sin = rope_ref[:, LANES:2 * LANES]
    lane = lax.broadcasted_iota(jnp.int32, (1, LANES), 1)
    first = (lane % 64) < 32
    for blk in range(rqk_ref.shape[1] // LANES):
        xh = rqk_ref[:, blk * LANES:(blk + 1) * LANES].astype(F32)
        partner = jnp.where(first, pltpu.roll(xh, 96, 1), pltpu.roll(xh, 32, 1))
        o_ref[:, nc + blk * LANES:nc + (blk + 1) * LANES] = (xh * cos + partner * sin).astype(BF16)


def _prep_call(um3, conv_w, conv_b, rope, nctx, tb):
    b, s_tot, _ = um3.shape
    nblk = s_tot // tb
    hb = tb // 16
    nc = conv_w.shape[1]
    return pl.pallas_call(
        functools.partial(_prep_kernel, nctx=nctx, nblk=nblk, tb=tb),
        out_shape=jax.ShapeDtypeStruct((b, s_tot, 2 * nc), BF16),
        grid=(b, nblk),
        in_specs=[pl.BlockSpec((None, tb, nc), lambda i, t: (i, t, 2)),
                  pl.BlockSpec((None, 16, nc), lambda i, t: (i, jnp.maximum(t * hb - 1, 0), 2)),
                  pl.BlockSpec((None, 16, nc), lambda i, t: (i, jnp.minimum((t + 1) * hb, nblk * hb - 1), 2)),
                  pl.BlockSpec((None, tb, nc), lambda i, t: (i, t, 3)),
                  pl.BlockSpec((8, nc), lambda i, t: (0, 0)),
                  pl.BlockSpec((1, nc), lambda i, t: (0, 0)),
                  pl.BlockSpec((tb, 2 * LANES), lambda i, t: (t, 0))],
        out_specs=pl.BlockSpec((None, tb, 2 * nc), lambda i, t: (i, t, 0)),
        compiler_params=_cparams(("parallel", "parallel"), 32),
        name="prep_conv_rope",
    )(um3, um3, um3, um3, conv_w, conv_b, rope)


def _blk_order(rev, nctx, nblk):
    if not rev:
        return lambda c: c
    return lambda c: jnp.where(c < nctx, nctx - 1 - c, 2 * nctx + (nblk - nctx) - 1 - c)


def _mlstm_kernel(*refs, rev, final, L, nsub):
    if final:
        qkvo_ref, gate_ref, bias_ref, yb_ref, ng_ref, o_ref, caug_ref, m_ref = refs
    else:
        qkvo_ref, gate_ref, bias_ref, o_ref, caug_ref, m_ref = refs
    d = 1 if rev else 0
    gw = 4 * LANES

    @pl.when(pl.program_id(1) == 0)
    def _():
        caug_ref[...] = jnp.zeros_like(caug_ref)
        m_ref[...] = jnp.zeros_like(m_ref)

    mask = _causal(L, rev)
    tri = mask.astype(BF16)
    lane = lax.broadcasted_iota(jnp.int32, (1, LANES), 1)
    is_f = jnp.logical_and(lane >= 8, lane < 16)
    e_idx = 0 if rev else L - 1
    ones = jnp.ones((L, LANES), BF16)
    for sc in (reversed(range(nsub)) if rev else range(nsub)):
        rows = slice(sc * L, (sc + 1) * L)
        g = gate_ref[rows, :] + bias_ref[0:1, :]
        gp = jnp.where(is_f, _log_sigmoid(g), g)
        cum = _tri_cumsum(tri, gp)
        gp_t = gp.T
        cum_t = cum.T
        for h in range(M_HEADS):
            ci = 4 * d + h
            cf = 8 + 4 * d + h
            b_col = cum[:, cf:cf + 1]
            b_row = cum_t[cf:cf + 1, :]
            ig_col = gp[:, ci:ci + 1]
            ig_row = gp_t[ci:ci + 1, :]
            m_st = m_ref[h, 0:1, 0:1]
            log_intra = jnp.where(mask, b_col - b_row + ig_row, NEG)
            log_inter = b_col + m_st
            m_i = jnp.maximum(log_inter, jnp.max(log_intra, axis=1, keepdims=True))
            w = jnp.exp(log_intra - m_i)
            w_inter = jnp.exp(log_inter - m_i)
            q = qkvo_ref[rows, h * LANES:(h + 1) * LANES]
            k = qkvo_ref[rows, gw + h * LANES:gw + (h + 1) * LANES]
            v = qkvo_ref[rows, 2 * gw + h * LANES:2 * gw + (h + 1) * LANES]
            vaug = jnp.concatenate([v, ones], axis=1)
            s = (_dot_nt(q, k) * w).astype(BF16)
            caug = caug_ref[h]
            res = _dot(s, vaug) + w_inter * _dot(q, caug.astype(BF16))
            num = res[:, 0:LANES]
            den = res[:, LANES:2 * LANES]
            hh = num / jnp.maximum(jnp.abs(den), jnp.exp(-m_i))
            b_end = cum[e_idx:e_idx + 1, cf:cf + 1]
            ls_row = b_end - b_row + ig_row
            m_new = jnp.maximum(b_end + m_st, jnp.max(ls_row, axis=1, keepdims=True))
            ws_col = jnp.exp(b_end - b_col + ig_col - m_new)
            decay = jnp.exp(b_end + m_st - m_new)
            kws = (k.astype(F32) * ws_col).astype(BF16)
            caug_ref[h] = decay * caug + _dot_tn(kws, vaug)
            m_ref[h] = jnp.broadcast_to(m_new, (8, LANES))
            cols = slice(h * LANES, (h + 1) * LANES)
            if final:
                y = hh + yb_ref[rows, cols]
                yc = y - jnp.mean(y, axis=-1, keepdims=True)
                yn = yc * lax.rsqrt(jnp.mean(yc * yc, axis=-1, keepdims=True) + EPS)
                og = qkvo_ref[rows, 3 * gw + h * LANES:3 * gw + (h + 1) * LANES].astype(F32)
                o_ref[rows, cols] = (_sigmoid(og) * (yn * ng_ref[0:1, cols])).astype(BF16)
            else:
                o_ref[rows, cols] = hh


def _ssd_kernel(*refs, rev, final, L, nsub):
    if final:
        xbc_ref, gate_ref, prm_ref, z_ref, yb_ref, dsk_ref, ng_ref, o_ref, st_ref = refs
    else:
        xbc_ref, gate_ref, prm_ref, o_ref, st_ref = refs
    d = 1 if rev else 0
    gw = 4 * LANES
    hpg = S_HEADS // S_G
    gcols = gw // S_G

    @pl.when(pl.program_id(1) == 0)
    def _():
        st_ref[...] = jnp.zeros_like(st_ref)

    mask = _causal(L, rev)
    tri = mask.astype(BF16)
    e_idx = 0 if rev else L - 1
    lane_head = lax.broadcasted_iota(jnp.int32, (1, gcols), 1) // (gcols // hpg)
    a_neg = -jnp.exp(prm_ref[1:2, :])
    for sc in (reversed(range(nsub)) if rev else range(nsub)):
        rows = slice(sc * L, (sc + 1) * L)
        dt_all = _softplus(gate_ref[rows, :] + prm_ref[0:1, :])
        cum = _tri_cumsum(tri, dt_all * a_neg)
        cum_t = cum.T
        ys = []
        for g in range(S_G):
            wts, e_cols, dec_cols, dt_cols, decays = [], [], [], [], []
            for hl in range(hpg):
                c = 16 + S_HEADS * d + g * hpg + hl
                b_col = cum[:, c:c + 1]
                b_row = cum_t[c:c + 1, :]
                b_end = cum[e_idx:e_idx + 1, c:c + 1]
                wts.append(jnp.exp(jnp.where(mask, b_col - b_row, NEG)))
                e_cols.append(jnp.exp(b_col))
                dec_cols.append(jnp.exp(b_end - b_col))
                dt_cols.append(dt_all[:, c:c + 1])
                decays.append(jnp.exp(b_end))
            xs_g = xbc_ref[rows, g * gcols:(g + 1) * gcols].astype(F32)
            bg = xbc_ref[rows, gw + g * S_N:gw + (g + 1) * S_N]
            cg = xbc_ref[rows, gw + S_G * S_N + g * S_N:gw + S_G * S_N + (g + 1) * S_N]
            xdt = xs_g * _expand4(dt_cols, lane_head)
            gm = _dot_nt(cg, bg)
            st = st_ref[g]
            acc = _expand4(e_cols, lane_head) * _dot(cg, st.astype(BF16))
            for hl in range(hpg):
                xh = jnp.where(lane_head == hl, xdt, 0.0).astype(BF16)
                acc = acc + _dot((gm * wts[hl]).astype(BF16), xh)
            xdec = (xdt * _expand4(dec_cols, lane_head)).astype(BF16)
            st_ref[g] = _expand4(decays, lane_head) * st + _dot_tn(bg, xdec)
            if final:
                cols = slice(g * gcols, (g + 1) * gcols)
                acc = acc + yb_ref[rows, cols] + dsk_ref[0:1, cols] * xs_g
                acc = acc * _silu(z_ref[rows, cols].astype(F32))
                ys.append(acc)
            else:
                o_ref[rows, g * gcols:(g + 1) * gcols] = acc
        if final:
            ms = sum(jnp.sum(y * y, axis=-1, keepdims=True) for y in ys) * (1.0 / gw)
            inv = lax.rsqrt(ms + EPS)
            for g in range(S_G):
                cols = slice(g * gcols, (g + 1) * gcols)
                o_ref[rows, cols] = (ys[g] * inv * ng_ref[0:1, cols]).astype(BF16)


def _ret_kernel(*refs, rev, final, L, nsub):
    if final:
        qk_ref, v_ref, prm_ref, g_ref, yb_ref, ng_ref, o_ref, st_ref = refs
    else:
        qk_ref, v_ref, prm_ref, o_ref, st_ref = refs
    d = 1 if rev else 0
    gw = 4 * LANES

    @pl.when(pl.program_id(1) == 0)
    def _():
        st_ref[...] = jnp.zeros_like(st_ref)

    mask = _causal(L, rev)
    icol = lax.broadcasted_iota(jnp.int32, (L, 1), 0).astype(F32)
    row = lax.broadcasted_iota(jnp.int32, (L, L), 0)
    col = lax.broadcasted_iota(jnp.int32, (L, L), 1)
    dist = ((col - row) if rev else (row - col)).astype(F32)
    lg_all = -jnp.exp(prm_ref[d:d + 1, :])
    for sc in (reversed(range(nsub)) if rev else range(nsub)):
        rows = slice(sc * L, (sc + 1) * L)
        for h in range(R_HEADS):
            lg = lg_all[:, h:h + 1]
            w = jnp.exp(jnp.where(mask, dist * lg, NEG))
            if rev:
                e_col = jnp.exp((float(L) - icol) * lg)
                dec_col = jnp.exp(icol * lg)
            else:
                e_col = jnp.exp((icol + 1.0) * lg)
                dec_col = jnp.exp((float(L - 1) - icol) * lg)
            decay = jnp.exp(float(L) * lg)
            cols = slice(h * LANES, (h + 1) * LANES)
            q = qk_ref[rows, cols]
            k = qk_ref[rows, gw + h * LANES:gw + (h + 1) * LANES]
            v = v_ref[rows, cols]
            st = st_ref[h]
            s = (_dot_nt(q, k) * w).astype(BF16)
            y = _dot(s, v) + e_col * _dot(q, st.astype(BF16))
            kd = (k.astype(F32) * dec_col).astype(BF16)
            st_ref[h] = decay * st + _dot_tn(kd, v)
            if final:
                y = y + yb_ref[rows, cols]
                yc = y - jnp.mean(y, axis=-1, keepdims=True)
                yn = yc * lax.rsqrt(jnp.mean(yc * yc, axis=-1, keepdims=True) + EPS)
                o_ref[rows, cols] = (yn * ng_ref[0:1, cols] * _silu(g_ref[rows, cols].astype(F32))).astype(BF16)
            else:
                o_ref[rows, cols] = y


def _gla_kernel(*refs, rev, final, L, nsub):
    if final:
        qk_ref, v_ref, gate_ref, gw_ref, gb_ref, g_ref, yb_ref, ng_ref, o_ref, st_ref = refs
    else:
        qk_ref, v_ref, gate_ref, gw_ref, gb_ref, o_ref, st_ref = refs
    tb = L * nsub
    kw = 2 * LANES

    @pl.when(pl.program_id(1) == 0)
    def _():
        st_ref[...] = jnp.zeros_like(st_ref)

    row = lax.broadcasted_iota(jnp.int32, (tb, tb), 0)
    col = lax.broadcasted_iota(jnp.int32, (tb, tb), 1)
    same = (row // L) == (col // L)
    tri = jnp.logical_and(same, (col >= row) if rev else (col <= row)).astype(BF16)
    mask = _causal(L, rev)
    e_idx = 0 if rev else L - 1
    lane_head = lax.broadcasted_iota(jnp.int32, (1, kw), 1) // (kw // G_HEADS)

    z = _dot_f32(gate_ref[...], gw_ref[...]) + gb_ref[...]
    la = _log_sigmoid(z) * (1.0 / G_TAU)
    cum = _tri_cumsum(tri, la)
    for sc in (reversed(range(nsub)) if rev else range(nsub)):
        rows = slice(sc * L, (sc + 1) * L)
        bsub = cum[sc * L:(sc + 1) * L, :]
        b_end = bsub[e_idx:e_idx + 1, :]
        qf = qk_ref[rows, 0:kw].astype(F32)
        kf = qk_ref[rows, kw:2 * kw].astype(F32)
        qg = qf * jnp.exp(bsub)
        kg = (kf * jnp.exp(-bsub)).astype(BF16)
        kd = kf * jnp.exp(b_end - bsub)
        st = st_ref[...]
        st_b = st.astype(BF16)
        upd = jnp.exp(b_end) * st
        for h in range(G_HEADS):
            hm = lane_head == h
            cols = slice(h * LANES, (h + 1) * LANES)
            qgh = jnp.where(hm, qg, 0.0).astype(BF16)
            v = v_ref[rows, cols]
            s = jnp.where(mask, _dot_nt(qgh, kg), 0.0).astype(BF16)
            y = _dot(s, v) + _dot_nt(qgh, st_b)
            upd = upd + _dot_tn(v, jnp.where(hm, kd, 0.0).astype(BF16))
            if final:
                y = y + yb_ref[rows, cols]
                yn = y * lax.rsqrt(jnp.mean(y * y, axis=-1, keepdims=True) + EPS)
                o_ref[rows, cols] = (yn * ng_ref[0:1, cols] * _silu(g_ref[rows, cols].astype(F32))).astype(BF16)
            else:
                o_ref[rows, cols] = y
        st_ref[...] = upd


def _scan_call(body, rev, final, L, tb, nctx, ins, scratch, out_cols, name, b, s_tot):
    nblk = s_tot // tb
    order = _blk_order(rev, nctx, nblk)
    in_specs = []
    for arr, bshape, kind, cb in ins:
        if kind == "tok":
            in_specs.append(pl.BlockSpec((None,) + bshape,
                                         functools.partial(lambda i, c, cb: (i, order(c), cb), cb=cb)))
        else:
            in_specs.append(pl.BlockSpec(bshape, lambda i, c: (0,) * len(bshape)))
    return pl.pallas_call(
        functools.partial(body, rev=rev, final=final, L=L, nsub=tb // L),
        out_shape=jax.ShapeDtypeStruct((b, s_tot, out_cols), BF16 if final else F32),
        grid=(b, nblk),
        in_specs=in_specs,
        out_specs=pl.BlockSpec((None, tb, out_cols), lambda i, c: (i, order(c), 0)),
        scratch_shapes=scratch,
        compiler_params=_cparams(("parallel", "arbitrary"), 48),
        name=name,
    )(*[a for a, _, _, _ in ins])


def _two_way(body, L, tb, nctx, ins, final_ins, scratch, name, b, s_tot, gw):
    yb = _scan_call(body, True, False, L, tb, nctx, ins, scratch, gw, name + "_bwd", b, s_tot)
    yb_in = (yb, (tb, gw), "tok", 0)
    return _scan_call(body, False, True, L, tb, nctx, ins + final_ins(yb_in), scratch, gw,
                      name + "_fwd", b, s_tot)


def _in_col_layout(d_model):
    gw = d_model // N_GROUPS
    conv_ch = gw + 2 * S_G * S_N
    names = [('m_q', gw), ('m_k', gw), ('m_v', gw), ('m_o', gw), ('m_i', 2 * M_HEADS), ('m_f', 2 * M_HEADS),
             ('s_z', gw), ('s_xbc', conv_ch), ('s_dt', 2 * S_HEADS),
             ('r_q', gw), ('r_k', gw), ('r_v', gw), ('r_g', gw),
             ('g_q', gw // 2), ('g_k', gw // 2), ('g_v', gw), ('g_g', gw), ('g_a', 2 * G_RANK)]
    off, o = {}, 0
    for nm, n in names:
        off[nm] = (o, n)
        o += n
    return off, o


def kernel(x, c, ctx, c_ctx, ada_w, ada_b, w_in, m_ig_b, m_fg_b, m_norm, s_conv_w, s_conv_b, s_dt_bias, s_a_log, s_d, s_norm, r_decay, r_norm, g_gate_w, g_gate_b, g_norm, w_out, post_g, post_b, ffn_w_up, ffn_w_down):
    b, seq, d = x.shape
    n_ctx_tok = ctx.shape[1]
    depth = ada_w.shape[0]
    gw = d // N_GROUPS
    assert gw == 4 * LANES and b + 1 <= 8
    s_tot = n_ctx_tok + seq
    t = b * s_tot
    tb = math.gcd(MAX_TOKEN_BLOCK, math.gcd(n_ctx_tok, seq))
    assert tb % 16 == 0 and tb % GLA_CHUNK == 0
    nctx = n_ctx_tok // tb
    nblk = s_tot // tb
    nsb = t // tb
    tm = max(m for m in (1024, 512, 256, 128, 64, 32, 16) if t % m == 0 and m % tb == 0)
    tm_res = max(m for m in (512, 256, 128, 64, 32, 16) if t % m == 0 and m % tb == 0)
    alpha = (2.0 * depth) ** 0.25

    off, _ = _in_col_layout(d)
    main_order = ['m_q', 'm_k', 'm_v', 'm_o', 's_xbc', 'r_q', 'r_k', 'r_v', 'r_g', 's_z', 'g_q', 'g_k', 'g_v', 'g_g']
    gate_order = ['m_i', 'm_f', 's_dt', 'g_a']
    col_scale = {'m_q': float(LANES) ** -0.5, 'r_k': float(LANES) ** -0.5, 'g_q': float(gw // 2 // G_HEADS) ** -0.5}
    main_idx = np.concatenate([np.arange(off[n][0], off[n][0] + off[n][1]) for n in main_order])
    main_scale = np.concatenate([np.full(off[n][1], col_scale.get(n, 1.0), np.float32) for n in main_order])
    gate_idx = np.concatenate([np.arange(off[n][0], off[n][0] + off[n][1]) for n in gate_order])
    n_gate = gate_idx.shape[0]

    pos = jnp.arange(seq)
    quarter = LANES // 4
    freqs = 1.0 / (ROPE_BASE ** (jnp.arange(quarter, dtype=F32) / quarter))
    ang_r = (pos // GRID_W).astype(F32)[:, None] * freqs[None, :]
    ang_c = (pos % GRID_W).astype(F32)[:, None] * freqs[None, :]
    cos_t = jnp.concatenate([jnp.cos(ang_r)] * 2 + [jnp.cos(ang_c)] * 2, axis=1)
    sin_t = jnp.concatenate([-jnp.sin(ang_r), jnp.sin(ang_r), -jnp.sin(ang_c), jnp.sin(ang_c)], axis=1)
    rope = jnp.concatenate([
        jnp.concatenate([jnp.ones((n_ctx_tok, LANES), F32), jnp.zeros((n_ctx_tok, LANES), F32)], axis=1),
        jnp.concatenate([cos_t, sin_t], axis=1)], axis=0)

    xs = jnp.concatenate([ctx, x], axis=1).reshape(t, d)
    cc = jnp.concatenate([c, c_ctx[None, :], jnp.zeros((8 - b - 1, d), F32)], axis=0)
    mod = _mod_call(cc, ada_w, ada_b)
    sb = np.arange(nsb)
    mod_row = np.where(sb % nblk < nctx, b, sb // nblk)

    def pad_lanes(v, start, width=LANES):
        return jnp.zeros((width,), F32).at[start:start + v.shape[0]].set(v)

    for l in range(depth):
        mods = jnp.pad(mod[l][mod_row].reshape(nsb, 6, d), ((0, 0), (0, 2), (0, 0)))
        wl = w_in[l]
        wm = (wl[:, main_idx] * main_scale).astype(BF16)
        wg = jnp.pad(wl[:, gate_idx], ((0, 0), (0, GATE_W - n_gate))).astype(BF16)
        um, ug = _in_call(xs, mods, wm, wg, tm, tb)
        um3 = um.reshape(b, s_tot, um.shape[1])
        ug3 = ug.reshape(b, s_tot, GATE_W)

        conv_w = jnp.pad(s_conv_w[l], ((0, 8 - CONV_W), (0, 0)))
        prep = _prep_call(um3, conv_w, s_conv_b[l][None, :], rope, nctx, tb)

        gate_in = (ug3, (tb, GATE_W), "tok", 0)

        m_bias = (pad_lanes(m_ig_b[l].reshape(-1), 0) + pad_lanes(m_fg_b[l].reshape(-1), 2 * M_HEADS))[None, :]
        m_ins = [(um3, (tb, 4 * gw), "tok", 0), gate_in, (m_bias, (1, LANES), "full", 0)]
        m_scr = [pltpu.VMEM((M_HEADS, LANES, 2 * LANES), F32), pltpu.VMEM((M_HEADS, 8, LANES), F32)]
        m_out = _two_way(_mlstm_kernel, tb, tb, nctx, m_ins,
                         lambda yb: [yb, (m_norm[l][None, :], (1, gw), "full", 0)],
                         m_scr, "mlstm", b, s_tot, gw)

        s_prm = jnp.zeros((8, LANES), F32).at[0].set(pad_lanes(s_dt_bias[l].reshape(-1), 16)).at[1].set(
            pad_lanes(s_a_log[l].reshape(-1), 16))
        dsk = jnp.repeat(s_d[l], gw // S_HEADS)[None, :]
        s_ins = [(prep, (tb, 2 * gw), "tok", 0), gate_in, (s_prm, (8, LANES), "full", 0)]
        s_scr = [pltpu.VMEM((S_G, S_N, gw // S_G), F32)]
        s_out = _two_way(_ssd_kernel, tb, tb, nctx, s_ins,
                         lambda yb: [(um3, (tb, gw), "tok", 10), yb, (dsk, (1, gw), "full", 0),
                                     (s_norm[l][None, :], (1, gw), "full", 0)],
                         s_scr, "ssd", b, s_tot, gw)

        r_prm = jnp.zeros((8, LANES), F32).at[0].set(pad_lanes(r_decay[l][0], 0)).at[1].set(pad_lanes(r_decay[l][1], 0))
        r_ins = [(prep, (tb, 2 * gw), "tok", 1), (um3, (tb, gw), "tok", 8), (r_prm, (8, LANES), "full", 0)]
        r_scr = [pltpu.VMEM((R_HEADS, LANES, LANES), F32)]
        r_out = _two_way(_ret_kernel, tb, tb, nctx, r_ins,
                         lambda yb: [(um3, (tb, gw), "tok", 9), yb, (r_norm[l][None, :], (1, gw), "full", 0)],
                         r_scr, "ret", b, s_tot, gw)

        g_dir = []
        for dd in range(2):
            gwp = jnp.zeros((GATE_W, gw // 2), F32).at[32 + G_RANK * dd:32 + G_RANK * (dd + 1)].set(g_gate_w[l][dd])
            g_dir.append([(um3, (tb, gw), "tok", 11), (um3, (tb, gw), "tok", 12), gate_in,
                          (gwp, (GATE_W, gw // 2), "full", 0), (g_gate_b[l][dd][None, :], (1, gw // 2), "full", 0)])
        g_scr = [pltpu.VMEM((LANES, 2 * LANES), F32)]
        g_yb = _scan_call(_gla_kernel, True, False, GLA_CHUNK, tb, nctx, g_dir[1], g_scr, gw, "gla_bwd", b, s_tot)
        g_out = _scan_call(_gla_kernel, False, True, GLA_CHUNK, tb, nctx,
                           g_dir[0] + [(um3, (tb, gw), "tok", 13), (g_yb, (tb, gw), "tok", 0),
                                       (g_norm[l][None, :], (1, gw), "full", 0)],
                           g_scr, gw, "gla_fwd", b, s_tot)

        acts = [a.reshape(t, gw) for a in (m_out, s_out, r_out, g_out)]
        x1 = _res_call(acts, w_out[l].astype(BF16), [0, 1, 2, 3], gw, xs, mods,
                       post_g[l, 0][None, :], post_b[l, 0][None, :], tm_res, tb, 2, alpha, "out_proj_res_ln")
        act = _up_call(x1, mods, ffn_w_up[l].astype(BF16), tm, tb)
        xs = _res_call([act], ffn_w_down[l].astype(BF16), [0], 512, x1, mods,
                       post_g[l, 1][None, :], post_b[l, 1][None, :], tm_res, tb, 5, alpha, "ffn_down_res_ln")

    return xs.reshape(b, s_tot, d)[:, n_ctx_tok:, :]
```

```python
import functools
import math

import numpy as np
import jax
import jax.numpy as jnp
from jax import lax
from jax.experimental import pallas as pl
from jax.experimental.pallas import tpu as pltpu

F32 = jnp.float32
BF16 = jnp.bfloat16

EPS = 1e-5
ROPE_BASE = 10000.0
GRID_W = 64
N_GROUPS = 4
M_HEADS = 4
S_HEADS = 8
S_G = 2
S_N = 128
CONV_W = 3
R_HEADS = 4
G_HEADS = 4
G_RANK = 16
G_TAU = 16.0
NEG = -1e30

LANES = 128
GATE_W = LANES
MAX_TOKEN_BLOCK = 256
GLA_CHUNK = 64


def _cparams(sem, vmem_mb):
    return pltpu.CompilerParams(dimension_semantics=sem, vmem_limit_bytes=vmem_mb * 1024 * 1024)


def _dot(a, b):
    return jnp.dot(a, b, preferred_element_type=F32)


def _dot_nt(a, b):
    return lax.dot_general(a, b, (((1,), (1,)), ((), ())), preferred_element_type=F32)


def _dot_tn(a, b):
    return lax.dot_general(a, b, (((0,), (0,)), ((), ())), preferred_element_type=F32)


def _split3(x):
    hi = x.astype(BF16)
    r1 = x - hi.astype(F32)
    mid = r1.astype(BF16)
    lo = (r1 - mid.astype(F32)).astype(BF16)
    return hi, mid, lo


def _tri_cumsum(tri, x):
    hi, mid, lo = _split3(x)
    return _dot(tri, hi) + _dot(tri, mid) + _dot(tri, lo)


def _dot_f32(a, b):
    ah = a.astype(BF16)
    al = (a - ah.astype(F32)).astype(BF16)
    bh = b.astype(BF16)
    bl = (b - bh.astype(F32)).astype(BF16)
    return _dot(ah, bh) + _dot(ah, bl) + _dot(al, bh)


def _sigmoid(x):
    return 1.0 / (1.0 + jnp.exp(-x))


def _silu(x):
    return x * _sigmoid(x)


def _softplus(x):
    return jnp.maximum(x, 0.0) + jnp.log1p(jnp.exp(-jnp.abs(x)))


def _log_sigmoid(x):
    return jnp.minimum(x, 0.0) - jnp.log1p(jnp.exp(-jnp.abs(x)))


def _ln(x):
    mu = jnp.mean(x, axis=-1, keepdims=True)
    xc = x - mu
    var = jnp.mean(xc * xc, axis=-1, keepdims=True)
    return xc * lax.rsqrt(var + EPS)


def _causal(n, rev):
    row = lax.broadcasted_iota(jnp.int32, (n, n), 0)
    col = lax.broadcasted_iota(jnp.int32, (n, n), 1)
    return (col >= row) if rev else (col <= row)


def _expand4(cols, lane_head):
    return jnp.where(lane_head == 0, cols[0],
                     jnp.where(lane_head == 1, cols[1],
                               jnp.where(lane_head == 2, cols[2], cols[3])))


def _mod_kernel(c_ref, w_ref, b_ref, o_ref):
    o_ref[0] = _dot_f32(_silu(c_ref[...]), w_ref[0]) + b_ref[0]


def _mod_call(cc, ada_w, ada_b):
    depth, d, n = ada_w.shape
    tn = 1024
    return pl.pallas_call(
        _mod_kernel,
        out_shape=jax.ShapeDtypeStruct((depth, 8, n), F32),
        grid=(depth, n // tn),
        in_specs=[pl.BlockSpec((8, d), lambda l, j: (0, 0)),
                  pl.BlockSpec((1, d, tn), lambda l, j: (l, 0, j)),
                  pl.BlockSpec((1, 1, tn), lambda l, j: (l, 0, j))],
        out_specs=pl.BlockSpec((1, 8, tn), lambda l, j: (l, 0, j)),
        compiler_params=_cparams(("parallel", "parallel"), 40),
        name="adaln_mod",
    )(cc, ada_w, ada_b.reshape(depth, 1, n))


def _ln_mod_to(h_ref, x_ref, mods_ref, nsub, tb, sh_row, sc_row):
    for r in range(nsub):
        x = x_ref[r * tb:(r + 1) * tb, :]
        h = _ln(x) * (1.0 + mods_ref[r, sc_row:sc_row + 1, :]) + mods_ref[r, sh_row:sh_row + 1, :]
        h_ref[r * tb:(r + 1) * tb, :] = h.astype(BF16)


def _in_kernel(x_ref, mods_ref, wm_ref, wg_ref, um_ref, ug_ref, h_ref, *, nsub, tb):
    @pl.when(pl.program_id(1) == 0)
    def _():
        _ln_mod_to(h_ref, x_ref, mods_ref, nsub, tb, 0, 1)
        ug_ref[...] = _dot(h_ref[...], wg_ref[...])

    um_ref[...] = _dot(h_ref[...], wm_ref[...]).astype(BF16)


def _in_call(xs, mods, wm, wg, tm, tb):
    t, d = xs.shape
    n = wm.shape[1]
    tn = 1024
    nsub = tm // tb
    return pl.pallas_call(
        functools.partial(_in_kernel, nsub=nsub, tb=tb),
        out_shape=(jax.ShapeDtypeStruct((t, n), BF16), jax.ShapeDtypeStruct((t, GATE_W), F32)),
        grid=(t // tm, n // tn),
        in_specs=[pl.BlockSpec((tm, d), lambda i, j: (i, 0)),
                  pl.BlockSpec((nsub, 8, d), lambda i, j: (i, 0, 0)),
                  pl.BlockSpec((d, tn), lambda i, j: (0, j)),
                  pl.BlockSpec((d, GATE_W), lambda i, j: (0, 0))],
        out_specs=(pl.BlockSpec((tm, tn), lambda i, j: (i, j)),
                   pl.BlockSpec((tm, GATE_W), lambda i, j: (i, 0))),
        scratch_shapes=[pltpu.VMEM((tm, d), BF16)],
        compiler_params=_cparams(("parallel", "arbitrary"), 48),
        name="ln_in_proj",
    )(xs, mods, wm, wg)


def _up_kernel(x_ref, mods_ref, wa_ref, wg_ref, o_ref, h_ref, *, nsub, tb):
    @pl.when(pl.program_id(1) == 0)
    def _():
        _ln_mod_to(h_ref, x_ref, mods_ref, nsub, tb, 3, 4)

    a = _dot(h_ref[...], wa_ref[...])
    g = _dot(h_ref[...], wg_ref[...])
    o_ref[...] = (_silu(a) * g).astype(BF16)


def _up_call(xs, mods, w_up, tm, tb):
    t, d = xs.shape
    dff = w_up.shape[1] // 2
    tn = 512
    nj = dff // tn
    nsub = tm // tb
    return pl.pallas_call(
        functools.partial(_up_kernel, nsub=nsub, tb=tb),
        out_shape=jax.ShapeDtypeStruct((t, dff), BF16),
        grid=(t // tm, nj),
        in_specs=[pl.BlockSpec((tm, d), lambda i, j: (i, 0)),
                  pl.BlockSpec((nsub, 8, d), lambda i, j: (i, 0, 0)),
                  pl.BlockSpec((d, tn), lambda i, j: (0, j)),
                  pl.BlockSpec((d, tn), lambda i, j: (0, j + nj))],
        out_specs=pl.BlockSpec((tm, tn), lambda i, j: (i, j)),
        scratch_shapes=[pltpu.VMEM((tm, d), BF16)],
        compiler_params=_cparams(("parallel", "arbitrary"), 48),
        name="ln_ffn_up",
    )(xs, mods, w_up, w_up)


def _res_kernel(a_ref, w_ref, x_ref, mods_ref, pg_ref, pb_ref, o_ref, *, nsub, tb, g_row, alpha):
    y = _dot(a_ref[...], w_ref[...])
    for r in range(nsub):
        rows = slice(r * tb, (r + 1) * tb)
        z = alpha * x_ref[rows, :] + mods_ref[r, g_row:g_row + 1, :] * y[rows, :]
        o_ref[rows, :] = _ln(z) * pg_ref[...] + pb_ref[...]


def _res_call(act, w, xs, mods, pg, pb, tm, tb, g_row, alpha, name):
    t, d = xs.shape
    ka = act.shape[1]
    nsub = tm // tb
    return pl.pallas_call(
        functools.partial(_res_kernel, nsub=nsub, tb=tb, g_row=g_row, alpha=alpha),
        out_shape=jax.ShapeDtypeStruct((t, d), F32),
        grid=(t // tm,),
        in_specs=[pl.BlockSpec((tm, ka), lambda i: (i, 0)),
                  pl.BlockSpec((ka, d), lambda i: (0, 0)),
                  pl.BlockSpec((tm, d), lambda i: (i, 0)),
                  pl.BlockSpec((nsub, 8, d), lambda i: (i, 0, 0)),
                  pl.BlockSpec((1, d), lambda i: (0, 0)),
                  pl.BlockSpec((1, d), lambda i: (0, 0))],
        out_specs=pl.BlockSpec((tm, d), lambda i: (i, 0)),
        compiler_params=_cparams(("parallel",), 52),
        name=name,
    )(act, w, xs, mods, pg, pb)


def _prep_kernel(cur_ref, prev_ref, next_ref, rqk_ref, cw_ref, cb_ref, rope_ref, o_ref, *, nctx, nblk, tb):
    t = pl.program_id(1)
    seg_start = jnp.logical_or(t == 0, t == nctx)
    seg_end = jnp.logical_or(t == nctx - 1, t == nblk - 1)
    x = cur_ref[...].astype(F32)
    nc = x.shape[1]
    prev_row = jnp.where(seg_start, 0.0, prev_ref[...].astype(F32)[15:16, :])
    next_row = jnp.where(seg_end, 0.0, next_ref[...].astype(F32)[0:1, :])
    ridx = lax.broadcasted_iota(jnp.int32, (tb, 1), 0)
    xp = jnp.where(ridx == 0, prev_row, pltpu.roll(x, 1, 0))
    xn = jnp.where(ridx == tb - 1, next_row, pltpu.roll(x, tb - 1, 0))
    y = cb_ref[...] + xp * cw_ref[0:1, :] + x * cw_ref[1:2, :] + xn * cw_ref[2:3, :]
    o_ref[:, 0:nc] = _silu(y).astype(BF16)

    cos = rope_ref[:, 0:LANES]
    sin = rope_ref[:, LANES:2 * LANES]
    lane = lax.broadcasted_iota(jnp.int32, (1, LANES), 1)
    first = (lane % 64) < 32
    for blk in range(rqk_ref.shape[1] // LANES):
        xh = rqk_ref[:, blk * LANES:(blk + 1) * LANES].astype(F32)
        partner = jnp.where(first, pltpu.roll(xh, 96, 1), pltpu.roll(xh, 32, 1))
        o_ref[:, nc + blk * LANES:nc + (blk + 1) * LANES] = (xh * cos + partner * sin).astype(BF16)


def _prep_call(um3, conv_w, conv_b, rope, nctx, tb):
    b, s_tot, _ = um3.shape
    nblk = s_tot // tb
    hb = tb // 16
    nc = conv_w.shape[1]
    return pl.pallas_call(
        functools.partial(_prep_kernel, nctx=nctx, nblk=nblk, tb=tb),
        out_shape=jax.ShapeDtypeStruct((b, s_tot, 2 * nc), BF16),
        grid=(b, nblk),
        in_specs=[pl.BlockSpec((None, tb, nc), lambda i, t: (i, t, 2)),
                  pl.BlockSpec((None, 16, nc), lambda i, t: (i, jnp.maximum(t * hb - 1, 0), 2)),
                  pl.BlockSpec((None, 16, nc), lambda i, t: (i, jnp.minimum((t + 1) * hb, nblk * hb - 1), 2)),
                  pl.BlockSpec((None, tb, nc), lambda i, t: (i, t, 3)),
                  pl.BlockSpec((8, nc), lambda i, t: (0, 0)),
                  pl.BlockSpec((1, nc), lambda i, t: (0, 0)),
                  pl.BlockSpec((tb, 2 * LANES), lambda i, t: (t, 0))],
        out_specs=pl.BlockSpec((None, tb, 2 * nc), lambda i, t: (i, t, 0)),
        compiler_params=_cparams(("parallel", "parallel"), 32),
        name="prep_conv_rope",
    )(um3, um3, um3, um3, conv_w, conv_b, rope)


def _mlstm_body(rev, final, L, nsub, qkvo_ref, gate_ref, bias_ref, caug_ref, m_ref, o_ref, oc, yb_ref, nrm_ref):
    d = 1 if rev else 0
    gw = 4 * LANES
    mask = _causal(L, rev)
    tri = mask.astype(BF16)
    lane = lax.broadcasted_iota(jnp.int32, (1, LANES), 1)
    is_f = jnp.logical_and(lane >= 8, lane < 16)
    e_idx = 0 if rev else L - 1
    ones = jnp.ones((L, LANES), BF16)
    for sc in (reversed(range(nsub)) if rev else range(nsub)):
        rows = slice(sc * L, (sc + 1) * L)
        g = gate_ref[rows, :] + bias_ref[0:1, :]
        gp = jnp.where(is_f, _log_sigmoid(g), g)
        cum = _tri_cumsum(tri, gp)
        gp_t = gp.T
        cum_t = cum.T
        for h in range(M_HEADS):
            ci = 4 * d + h
            cf = 8 + 4 * d + h
            b_col = cum[:, cf:cf + 1]
            b_row = cum_t[cf:cf + 1, :]
            ig_col = gp[:, ci:ci + 1]
            ig_row = gp_t[ci:ci + 1, :]
            m_st = m_ref[h, 0:1, 0:1]
            log_intra = jnp.where(mask, b_col - b_row + ig_row, NEG)
            log_inter = b_col + m_st
            m_i = jnp.maximum(log_inter, jnp.max(log_intra, axis=1, keepdims=True))
            w = jnp.exp(log_intra - m_i)
            w_inter = jnp.exp(log_inter - m_i)
            q = qkvo_ref[rows, h * LANES:(h + 1) * LANES]
            k = qkvo_ref[rows, gw + h * LANES:gw + (h + 1) * LANES]
            v = qkvo_ref[rows, 2 * gw + h * LANES:2 * gw + (h + 1) * LANES]
            vaug = jnp.concatenate([v, ones], axis=1)
            s = (_dot_nt(q, k) * w).astype(BF16)
            caug = caug_ref[h]
            res = _dot(s, vaug) + w_inter * _dot(q, caug.astype(BF16))
            num = res[:, 0:LANES]
            den = res[:, LANES:2 * LANES]
            hh = num / jnp.maximum(jnp.abs(den), jnp.exp(-m_i))
            b_end = cum[e_idx:e_idx + 1, cf:cf + 1]
            ls_row = b_end - b_row + ig_row
            m_new = jnp.maximum(b_end + m_st, jnp.max(ls_row, axis=1, keepdims=True))
            ws_col = jnp.exp(b_end - b_col + ig_col - m_new)
            decay = jnp.exp(b_end + m_st - m_new)
            kws = (k.astype(F32) * ws_col).astype(BF16)
            caug_ref[h] = decay * caug + _dot_tn(kws, vaug)
            m_ref[h] = jnp.broadcast_to(m_new, (8, LANES))
            hc = slice(h * LANES, (h + 1) * LANES)
            cols = slice(oc + h * LANES, oc + (h + 1) * LANES)
            if final:
                y = hh + yb_ref[rows, cols]
                yc = y - jnp.mean(y, axis=-1, keepdims=True)
                yn = yc * lax.rsqrt(jnp.mean(yc * yc, axis=-1, keepdims=True) + EPS)
                og = qkvo_ref[rows, 3 * gw + h * LANES:3 * gw + (h + 1) * LANES].astype(F32)
                o_ref[rows, cols] = (_sigmoid(og) * (yn * nrm_ref[0:1, hc])).astype(BF16)
            else:
                o_ref[rows, cols] = hh


def _ssd_body(rev, final, L, nsub, xbc_ref, gate_ref, prm_ref, st_ref, o_ref, oc, z_ref, yb_ref, nrm_ref):
    d = 1 if rev else 0
    gw = 4 * LANES
    hpg = S_HEADS // S_G
    gcols = gw // S_G
    mask = _causal(L, rev)
    tri = mask.astype(BF16)
    e_idx = 0 if rev else L - 1
    lane_head = lax.broadcasted_iota(jnp.int32, (1, gcols), 1) // (gcols // hpg)
    a_neg = -jnp.exp(prm_ref[1:2, :])
    for sc in (reversed(range(nsub)) if rev else range(nsub)):
        rows = slice(sc * L, (sc + 1) * L)
        dt_all = _softplus(gate_ref[rows, :] + prm_ref[0:1, :])
        cum = _tri_cumsum(tri, dt_all * a_neg)
        cum_t = cum.T
        ys = []
        for g in range(S_G):
            wts, e_cols, dec_cols, dt_cols, decays = [], [], [], [], []
            for hl in range(hpg):
                c = 16 + S_HEADS * d + g * hpg + hl
                b_col = cum[:, c:c + 1]
                b_row = cum_t[c:c + 1, :]
                b_end = cum[e_idx:e_idx + 1, c:c + 1]
                wts.append(jnp.exp(jnp.where(mask, b_col - b_row, NEG)))
                e_cols.append(jnp.exp(b_col))
                dec_cols.append(jnp.exp(b_end - b_col))
                dt_cols.append(dt_all[:, c:c + 1])
                decays.append(jnp.exp(b_end))
            gc = slice(g * gcols, (g + 1) * gcols)
            xs_g = xbc_ref[rows, gc].astype(F32)
            bg = xbc_ref[rows, gw + g * S_N:gw + (g + 1) * S_N]
            cg = xbc_ref[rows, gw + S_G * S_N + g * S_N:gw + S_G * S_N + (g + 1) * S_N]
            xdt = xs_g * _expand4(dt_cols, lane_head)
            gm = _dot_nt(cg, bg)
            st = st_ref[g]
            acc = _expand4(e_cols, lane_head) * _dot(cg, st.astype(BF16))
            for hl in range(hpg):
                xh = jnp.where(lane_head == hl, xdt, 0.0).astype(BF16)
                acc = acc + _dot((gm * wts[hl]).astype(BF16), xh)
            xdec = (xdt * _expand4(dec_cols, lane_head)).astype(BF16)
            st_ref[g] = _expand4(decays, lane_head) * st + _dot_tn(bg, xdec)
            cols = slice(oc + g * gcols, oc + (g + 1) * gcols)
            if final:
                acc = acc + yb_ref[rows, cols] + nrm_ref[4:5, gc] * xs_g
                ys.append(acc * _silu(z_ref[rows, gc].astype(F32)))
            else:
                o_ref[rows, cols] = acc
        if final:
            ms = sum(jnp.sum(y * y, axis=-1, keepdims=True) for y in ys) * (1.0 / gw)
            inv = lax.rsqrt(ms + EPS)
            for g in range(S_G):
                gc = slice(g * gcols, (g + 1) * gcols)
                o_ref[rows, oc + g * gcols:oc + (g + 1) * gcols] = (ys[g] * inv * nrm_ref[1:2, gc]).astype(BF16)


def _ret_body(rev, final, L, nsub, qk_ref, qc, v_ref, prm_ref, st_ref, o_ref, oc, g_ref, yb_ref, nrm_ref):
    d = 1 if rev else 0
    gw = 4 * LANES
    mask = _causal(L, rev)
    icol = lax.broadcasted_iota(jnp.int32, (L, 1), 0).astype(F32)
    row = lax.broadcasted_iota(jnp.int32, (L, L), 0)
    col = lax.broadcasted_iota(jnp.int32, (L, L), 1)
    dist = ((col - row) if rev else (row - col)).astype(F32)
    lg_all = -jnp.exp(prm_ref[d:d + 1, :])
    for sc in (reversed(range(nsub)) if rev else range(nsub)):
        rows = slice(sc * L, (sc + 1) * L)
        for h in range(R_HEADS):
            lg = lg_all[:, h:h + 1]
            w = jnp.exp(jnp.where(mask, dist * lg, NEG))
            if rev:
                e_col = jnp.exp((float(L) - icol) * lg)
                dec_col = jnp.exp(icol * lg)
            else:
                e_col = jnp.exp((icol + 1.0) * lg)
                dec_col = jnp.exp((float(L - 1) - icol) * lg)
            decay = jnp.exp(float(L) * lg)
            hc = slice(h * LANES, (h + 1) * LANES)
            q = qk_ref[rows, qc + h * LANES:qc + (h + 1) * LANES]
            k = qk_ref[rows, qc + gw + h * LANES:qc + gw + (h + 1) * LANES]
            v = v_ref[rows, hc]
            st = st_ref[h]
            s = (_dot_nt(q, k) * w).astype(BF16)
            y = _dot(s, v) + e_col * _dot(q, st.astype(BF16))
            kd = (k.astype(F32) * dec_col).astype(BF16)
            st_ref[h] = decay * st + _dot_tn(kd, v)
            cols = slice(oc + h * LANES, oc + (h + 1) * LANES)
            if final:
                y = y + yb_ref[rows, cols]
                yc = y - jnp.mean(y, axis=-1, keepdims=True)
                yn = yc * lax.rsqrt(jnp.mean(yc * yc, axis=-1, keepdims=True) + EPS)
                o_ref[rows, cols] = (yn * nrm_ref[2:3, hc] * _silu(g_ref[rows, hc].astype(F32))).astype(BF16)
            else:
                o_ref[rows, cols] = y


def _gla_body(rev, final, L, nsub, qk_ref, v_ref, gate_ref, gw_ref, gb_ref, st_ref, o_ref, oc, g_ref, yb_ref, nrm_ref):
    tb = L * nsub
    kw = 2 * LANES
    row = lax.broadcasted_iota(jnp.int32, (tb, tb), 0)
    col = lax.broadcasted_iota(jnp.int32, (tb, tb), 1)
    same = (row // L) == (col // L)
    tri = jnp.logical_and(same, (col >= row) if rev else (col <= row)).astype(BF16)
    mask = _causal(L, rev)
    e_idx = 0 if rev else L - 1
    lane_head = lax.broadcasted_iota(jnp.int32, (1, kw), 1) // (kw // G_HEADS)

    z = _dot_f32(gate_ref[...], gw_ref[...]) + gb_ref[...]
    la = _log_sigmoid(z) * (1.0 / G_TAU)
    cum = _tri_cumsum(tri, la)
    for sc in (reversed(range(nsub)) if rev else range(nsub)):
        rows = slice(sc * L, (sc + 1) * L)
        bsub = cum[sc * L:(sc + 1) * L, :]
        b_end = bsub[e_idx:e_idx + 1, :]
        qf = qk_ref[rows, 0:kw].astype(F32)
        kf = qk_ref[rows, kw:2 * kw].astype(F32)
        qg = qf * jnp.exp(bsub)
        kg = (kf * jnp.exp(-bsub)).astype(BF16)
        kd = kf * jnp.exp(b_end - bsub)
        st = st_ref[...]
        st_b = st.astype(BF16)
        upd = jnp.exp(b_end) * st
        for h in range(G_HEADS):
            hm = lane_head == h
            hc = slice(h * LANES, (h + 1) * LANES)
            cols = slice(oc + h * LANES, oc + (h + 1) * LANES)
            qgh = jnp.where(hm, qg, 0.0).astype(BF16)
            v = v_ref[rows, hc]
            s = jnp.where(mask, _dot_nt(qgh, kg), 0.0).astype(BF16)
            y = _dot(s, v) + _dot_nt(qgh, st_b)
            upd = upd + _dot_tn(v, jnp.where(hm, kd, 0.0).astype(BF16))
            if final:
                y = y + yb_ref[rows, cols]
                yn = y * lax.rsqrt(jnp.mean(y * y, axis=-1, keepdims=True) + EPS)
                o_ref[rows, cols] = (yn * nrm_ref[3:4, hc] * _silu(g_ref[rows, hc].astype(F32))).astype(BF16)
            else:
                o_ref[rows, cols] = y
        st_ref[...] = upd


def _scan_kernel(*refs, rev, final, tb):
    (um_m, ug, prep, r_v, g_qk, g_v, m_bias, s_prm, r_prm, gwp, gb) = refs[:11]
    if final:
        r_g, s_z, g_g, yb, nrm = refs[11:16]
        o_ref = refs[16]
        scratch = refs[17:]
    else:
        r_g = s_z = g_g = yb = nrm = None
        o_ref = refs[11]
        scratch = refs[12:]
    caug, m_st, s_st, r_st, g_st = scratch
    gw = 4 * LANES

    @pl.when(pl.program_id(1) == 0)
    def _():
        for ref in scratch:
            ref[...] = jnp.zeros_like(ref)

    _mlstm_body(rev, final, tb, 1, um_m, ug, m_bias, caug, m_st, o_ref, 0, yb, nrm)
    _ssd_body(rev, final, tb, 1, prep, ug, s_prm, s_st, o_ref, gw, s_z, yb, nrm)
    _ret_body(rev, final, tb, 1, prep, 2 * gw, r_v, r_prm, r_st, o_ref, 2 * gw, r_g, yb, nrm)
    _gla_body(rev, final, GLA_CHUNK, tb // GLA_CHUNK, g_qk, g_v, ug, gwp, gb, g_st, o_ref, 3 * gw, g_g, yb, nrm)


def _scan_call(rev, final, tb, nctx, um3, ug3, prep, params, yb, nrm):
    b, s_tot, _ = um3.shape
    gw = 4 * LANES
    nblk = s_tot // tb
    if rev:
        def order(c):
            return jnp.where(c < nctx, nctx - 1 - c, nblk + nctx - 1 - c)
    else:
        def order(c):
            return c

    def tok(width, cb):
        return pl.BlockSpec((None, tb, width), lambda i, c: (i, order(c), cb))

    def full(arr):
        return pl.BlockSpec(arr.shape, lambda i, c: (0,) * arr.ndim)

    ins = [um3, ug3, prep, um3, um3, um3] + list(params)
    in_specs = [tok(4 * gw, 0), tok(GATE_W, 0), tok(4 * gw, 0), tok(gw, 8), tok(gw, 11), tok(gw, 12)]
    in_specs += [full(p) for p in params]
    if final:
        ins += [um3, um3, um3, yb, nrm]
        in_specs += [tok(gw, 9), tok(gw, 10), tok(gw, 13), tok(4 * gw, 0), full(nrm)]
    scratch = [pltpu.VMEM((M_HEADS, LANES, 2 * LANES), F32), pltpu.VMEM((M_HEADS, 8, LANES), F32),
               pltpu.VMEM((S_G, S_N, gw // S_G), F32), pltpu.VMEM((R_HEADS, LANES, LANES), F32),
               pltpu.VMEM((LANES, 2 * LANES), F32)]
    return pl.pallas_call(
        functools.partial(_scan_kernel, rev=rev, final=final, tb=tb),
        out_shape=jax.ShapeDtypeStruct((b, s_tot, 4 * gw), BF16 if final else F32),
        grid=(b, nblk),
        in_specs=in_specs,
        out_specs=tok(4 * gw, 0),
        scratch_shapes=scratch,
        compiler_params=_cparams(("parallel", "arbitrary"), 48),
        name="scan_fwd" if final else "scan_bwd",
    )(*ins)


def _in_col_layout(d_model):
    gw = d_model // N_GROUPS
    conv_ch = gw + 2 * S_G * S_N
    names = [('m_q', gw), ('m_k', gw), ('m_v', gw), ('m_o', gw), ('m_i', 2 * M_HEADS), ('m_f', 2 * M_HEADS),
             ('s_z', gw), ('s_xbc', conv_ch), ('s_dt', 2 * S_HEADS),
             ('r_q', gw), ('r_k', gw), ('r_v', gw), ('r_g', gw),
             ('g_q', gw // 2), ('g_k', gw // 2), ('g_v', gw), ('g_g', gw), ('g_a', 2 * G_RANK)]
    off, o = {}, 0
    for nm, n in names:
        off[nm] = (o, n)
        o += n
    return off


def kernel(x, c, ctx, c_ctx, ada_w, ada_b, w_in, m_ig_b, m_fg_b, m_norm, s_conv_w, s_conv_b, s_dt_bias, s_a_log, s_d, s_norm, r_decay, r_norm, g_gate_w, g_gate_b, g_norm, w_out, post_g, post_b, ffn_w_up, ffn_w_down):
    b, seq, d = x.shape
    n_ctx_tok = ctx.shape[1]
    depth = ada_w.shape[0]
    gw = d // N_GROUPS
    assert gw == 4 * LANES and b + 1 <= 8
    s_tot = n_ctx_tok + seq
    t = b * s_tot
    tb = math.gcd(MAX_TOKEN_BLOCK, math.gcd(n_ctx_tok, seq))
    assert tb % 16 == 0 and tb % GLA_CHUNK == 0
    nctx = n_ctx_tok // tb
    nblk = s_tot // tb
    nsb = t // tb
    tm = max(m for m in (1024, 512, 256, 128, 64, 32, 16) if t % m == 0 and m % tb == 0)
    tm_out = max(m for m in (512, 256, 128, 64, 32, 16) if t % m == 0 and m % tb == 0)
    alpha = (2.0 * depth) ** 0.25

    off = _in_col_layout(d)
    main_order = ['m_q', 'm_k', 'm_v', 'm_o', 's_xbc', 'r_q', 'r_k', 'r_v', 'r_g', 's_z', 'g_q', 'g_k', 'g_v', 'g_g']
    gate_order = ['m_i', 'm_f', 's_dt', 'g_a']
    col_scale = {'m_q': float(LANES) ** -0.5, 'r_k': float(LANES) ** -0.5, 'g_q': float(gw // 2 // G_HEADS) ** -0.5}
    n_gate = sum(off[n][1] for n in gate_order)

    def cols_of(wl, name):
        o, n = off[name]
        w = wl[:, o:o + n]
        return w * col_scale[name] if name in col_scale else w

    pos = jnp.arange(seq)
    quarter = LANES // 4
    freqs = 1.0 / (ROPE_BASE ** (jnp.arange(quarter, dtype=F32) / quarter))
    ang_r = (pos // GRID_W).astype(F32)[:, None] * freqs[None, :]
    ang_c = (pos % GRID_W).astype(F32)[:, None] * freqs[None, :]
    cos_t = jnp.concatenate([jnp.cos(ang_r)] * 2 + [jnp.cos(ang_c)] * 2, axis=1)
    sin_t = jnp.concatenate([-jnp.sin(ang_r), jnp.sin(ang_r), -jnp.sin(ang_c), jnp.sin(ang_c)], axis=1)
    rope = jnp.concatenate([
        jnp.concatenate([jnp.ones((n_ctx_tok, LANES), F32), jnp.zeros((n_ctx_tok, LANES), F32)], axis=1),
        jnp.concatenate([cos_t, sin_t], axis=1)], axis=0)

    xs = jnp.concatenate([ctx, x], axis=1).reshape(t, d)
    cc = jnp.concatenate([c, c_ctx[None, :], jnp.zeros((8 - b - 1, d), F32)], axis=0)
    mod = _mod_call(cc, ada_w, ada_b)
    sb = np.arange(nsb)
    mod_row = np.where(sb % nblk < nctx, b, sb // nblk)

    def pad_lanes(v, start, width=LANES):
        return jnp.zeros((width,), F32).at[start:start + v.shape[0]].set(v)

    for l in range(depth):
        mods = jnp.pad(mod[l][mod_row].reshape(nsb, 6, d), ((0, 0), (0, 2), (0, 0)))
        wl = w_in[l]
        wm = jnp.concatenate([cols_of(wl, n) for n in main_order], axis=1).astype(BF16)
        wg = jnp.concatenate([cols_of(wl, n) for n in gate_order] + [jnp.zeros((d, GATE_W - n_gate), F32)],
                             axis=1).astype(BF16)
        um, ug = _in_call(xs, mods, wm, wg, tm, tb)
        um3 = um.reshape(b, s_tot, um.shape[1])
        ug3 = ug.reshape(b, s_tot, GATE_W)

        conv_w = jnp.pad(s_conv_w[l], ((0, 8 - CONV_W), (0, 0)))
        prep = _prep_call(um3, conv_w, s_conv_b[l][None, :], rope, nctx, tb)

        m_bias = (pad_lanes(m_ig_b[l].reshape(-1), 0) + pad_lanes(m_fg_b[l].reshape(-1), 2 * M_HEADS))[None, :]
        s_prm = jnp.zeros((8, LANES), F32).at[0].set(pad_lanes(s_dt_bias[l].reshape(-1), 16)).at[1].set(
            pad_lanes(s_a_log[l].reshape(-1), 16))
        r_prm = jnp.zeros((8, LANES), F32).at[0].set(pad_lanes(r_decay[l][0], 0)).at[1].set(pad_lanes(r_decay[l][1], 0))
        nrm = jnp.zeros((8, gw), F32).at[0].set(m_norm[l]).at[1].set(s_norm[l]).at[2].set(r_norm[l]).at[3].set(
            g_norm[l]).at[4].set(jnp.repeat(s_d[l], gw // S_HEADS))

        def dir_params(dd):
            gwp = jnp.zeros((GATE_W, gw // 2), F32).at[32 + G_RANK * dd:32 + G_RANK * (dd + 1)].set(g_gate_w[l][dd])
            return [m_bias, s_prm, r_prm, gwp, g_gate_b[l][dd][None, :]]

        yb = _scan_call(True, False, tb, nctx, um3, ug3, prep, dir_params(1), None, None)
        mix = _scan_call(False, True, tb, nctx, um3, ug3, prep, dir_params(0), yb, nrm)

        x1 = _res_call(mix.reshape(t, 4 * gw), w_out[l].astype(BF16), xs, mods,
                       post_g[l, 0][None, :], post_b[l, 0][None, :], tm_out, tb, 2, alpha, "out_proj_res_ln")
        act = _up_call(x1, mods, ffn_w_up[l].astype(BF16), tm, tb)
        xs = _res_call(act, ffn_w_down[l].astype(BF16), x1, mods,
                       post_g[l, 1][None, :], post_b[l, 1][None, :], tb, tb, 5, alpha, "ffn_down_res_ln")

    return xs.reshape(b, s_tot, d)[:, n_ctx_tok:, :]
```

```python
import functools
import math

import numpy as np
import jax
import jax.numpy as jnp
from jax import lax
from jax.experimental import pallas as pl
from jax.experimental.pallas import tpu as pltpu

F32 = jnp.float32
BF16 = jnp.bfloat16

EPS = 1e-5
ROPE_BASE = 10000.0
GRID_W = 64
N_GROUPS = 4
M_HEADS = 4
S_HEADS = 8
S_G = 2
S_N = 128
CONV_W = 3
R_HEADS = 4
G_HEADS = 4
G_RANK = 16
G_TAU = 16.0
NEG = -1e30

LANES = 128
GATE_W = LANES
MAX_TOKEN_BLOCK = 256
GLA_CHUNK = 64


def _cparams(sem, vmem_mb):
    return pltpu.CompilerParams(dimension_semantics=sem, vmem_limit_bytes=vmem_mb * 1024 * 1024)


def _dot(a, b):
    return jnp.dot(a, b, preferred_element_type=F32)


def _dot_nt(a, b):
    return lax.dot_general(a, b, (((1,), (1,)), ((), ())), preferred_element_type=F32)


def _dot_tn(a, b):
    return lax.dot_general(a, b, (((0,), (0,)), ((), ())), preferred_element_type=F32)


def _split3(x):
    hi = x.astype(BF16)
    r1 = x - hi.astype(F32)
    mid = r1.astype(BF16)
    lo = (r1 - mid.astype(F32)).astype(BF16)
    return hi, mid, lo


def _tri_cumsum(tri, x):
    hi, mid, lo = _split3(x)
    return _dot(tri, hi) + _dot(tri, mid) + _dot(tri, lo)


def _dot_f32(a, b):
    ah = a.astype(BF16)
    al = (a - ah.astype(F32)).astype(BF16)
    bh = b.astype(BF16)
    bl = (b - bh.astype(F32)).astype(BF16)
    return _dot(ah, bh) + _dot(ah, bl) + _dot(al, bh)


def _sigmoid(x):
    return 1.0 / (1.0 + jnp.exp(-x))


def _silu(x):
    return x * _sigmoid(x)


def _softplus(x):
    return jnp.maximum(x, 0.0) + jnp.log1p(jnp.exp(-jnp.abs(x)))


def _log_sigmoid(x):
    return jnp.minimum(x, 0.0) - jnp.log1p(jnp.exp(-jnp.abs(x)))


def _ln(x):
    mu = jnp.mean(x, axis=-1, keepdims=True)
    xc = x - mu
    var = jnp.mean(xc * xc, axis=-1, keepdims=True)
    return xc * lax.rsqrt(var + EPS)


def _causal(n, rev):
    row = lax.broadcasted_iota(jnp.int32, (n, n), 0)
    col = lax.broadcasted_iota(jnp.int32, (n, n), 1)
    return (col >= row) if rev else (col <= row)


def _expand4(cols, lane_head):
    return jnp.where(lane_head == 0, cols[0],
                     jnp.where(lane_head == 1, cols[1],
                               jnp.where(lane_head == 2, cols[2], cols[3])))


def _mod_kernel(c_ref, w_ref, b_ref, o_ref):
    o_ref[0] = _dot(_silu(c_ref[...]), w_ref[0]) + b_ref[0]


def _mod_call(cc, ada_w, ada_b):
    depth, d, n = ada_w.shape
    tn = 1024
    return pl.pallas_call(
        _mod_kernel,
        out_shape=jax.ShapeDtypeStruct((depth, 8, n), F32),
        grid=(depth, n // tn),
        in_specs=[pl.BlockSpec((8, d), lambda l, j: (0, 0)),
                  pl.BlockSpec((1, d, tn), lambda l, j: (l, 0, j)),
                  pl.BlockSpec((1, 1, tn), lambda l, j: (l, 0, j))],
        out_specs=pl.BlockSpec((1, 8, tn), lambda l, j: (l, 0, j)),
        compiler_params=_cparams(("parallel", "parallel"), 40),
        name="adaln_mod",
    )(cc, ada_w, ada_b.reshape(depth, 1, n))


def _ln_mod_to(h_ref, x_ref, mods_ref, nsub, tb, sh_row, sc_row):
    for r in range(nsub):
        x = x_ref[r * tb:(r + 1) * tb, :]
        h = _ln(x) * (1.0 + mods_ref[r, sc_row:sc_row + 1, :]) + mods_ref[r, sh_row:sh_row + 1, :]
        h_ref[r * tb:(r + 1) * tb, :] = h.astype(BF16)


def _in_kernel(x_ref, mods_ref, wm_ref, wg_ref, um_ref, ug_ref, h_ref, *, nsub, tb):
    @pl.when(pl.program_id(1) == 0)
    def _():
        _ln_mod_to(h_ref, x_ref, mods_ref, nsub, tb, 0, 1)
        ug_ref[...] = _dot(h_ref[...], wg_ref[...])

    um_ref[...] = _dot(h_ref[...], wm_ref[...]).astype(BF16)


def _in_call(xs, mods, wm, wg, tm, tb):
    t, d = xs.shape
    n = wm.shape[1]
    tn = 1024
    nsub = tm // tb
    return pl.pallas_call(
        functools.partial(_in_kernel, nsub=nsub, tb=tb),
        out_shape=(jax.ShapeDtypeStruct((t, n), BF16), jax.ShapeDtypeStruct((t, GATE_W), F32)),
        grid=(t // tm, n // tn),
        in_specs=[pl.BlockSpec((tm, d), lambda i, j: (i, 0)),
                  pl.BlockSpec((nsub, 8, d), lambda i, j: (i, 0, 0)),
                  pl.BlockSpec((d, tn), lambda i, j: (0, j)),
                  pl.BlockSpec((d, GATE_W), lambda i, j: (0, 0))],
        out_specs=(pl.BlockSpec((tm, tn), lambda i, j: (i, j)),
                   pl.BlockSpec((tm, GATE_W), lambda i, j: (i, 0))),
        scratch_shapes=[pltpu.VMEM((tm, d), BF16)],
        compiler_params=_cparams(("parallel", "arbitrary"), 48),
        name="ln_in_proj",
    )(xs, mods, wm, wg)


def _up_kernel(x_ref, mods_ref, wa_ref, wg_ref, o_ref, h_ref, *, nsub, tb):
    @pl.when(pl.program_id(1) == 0)
    def _():
        _ln_mod_to(h_ref, x_ref, mods_ref, nsub, tb, 3, 4)

    a = _dot(h_ref[...], wa_ref[...])
    g = _dot(h_ref[...], wg_ref[...])
    o_ref[...] = (_silu(a) * g).astype(BF16)


def _up_call(xs, mods, w_up, layer, tm, tb):
    t, d = xs.shape
    dff = w_up.shape[2] // 2
    tn = 512
    nj = dff // tn
    nsub = tm // tb
    return pl.pallas_call(
        functools.partial(_up_kernel, nsub=nsub, tb=tb),
        out_shape=jax.ShapeDtypeStruct((t, dff), BF16),
        grid=(t // tm, nj),
        in_specs=[pl.BlockSpec((tm, d), lambda i, j: (i, 0)),
                  pl.BlockSpec((nsub, 8, d), lambda i, j: (i, 0, 0)),
                  pl.BlockSpec((None, d, tn), lambda i, j: (layer, 0, j)),
                  pl.BlockSpec((None, d, tn), lambda i, j: (layer, 0, j + nj))],
        out_specs=pl.BlockSpec((tm, tn), lambda i, j: (i, j)),
        scratch_shapes=[pltpu.VMEM((tm, d), BF16)],
        compiler_params=_cparams(("parallel", "arbitrary"), 48),
        name="ln_ffn_up",
    )(xs, mods, w_up, w_up)


def _res_kernel(a_ref, w_ref, x_ref, mods_ref, pg_ref, pb_ref, o_ref, *, nsub, tb, g_row, alpha):
    y = _dot(a_ref[...], w_ref[...])
    for r in range(nsub):
        rows = slice(r * tb, (r + 1) * tb)
        z = alpha * x_ref[rows, :] + mods_ref[r, g_row:g_row + 1, :] * y[rows, :]
        o_ref[rows, :] = _ln(z) * pg_ref[...] + pb_ref[...]


def _res_call(act, w, layer, xs, mods, pg, pb, tm, tb, g_row, alpha, name):
    t, d = xs.shape
    ka = act.shape[1]
    nsub = tm // tb
    return pl.pallas_call(
        functools.partial(_res_kernel, nsub=nsub, tb=tb, g_row=g_row, alpha=alpha),
        out_shape=jax.ShapeDtypeStruct((t, d), F32),
        grid=(t // tm,),
        in_specs=[pl.BlockSpec((tm, ka), lambda i: (i, 0)),
                  pl.BlockSpec((None, ka, d), lambda i: (layer, 0, 0), pipeline_mode=pl.Buffered(1)),
                  pl.BlockSpec((tm, d), lambda i: (i, 0)),
                  pl.BlockSpec((nsub, 8, d), lambda i: (i, 0, 0)),
                  pl.BlockSpec((1, d), lambda i: (0, 0)),
                  pl.BlockSpec((1, d), lambda i: (0, 0))],
        out_specs=pl.BlockSpec((tm, d), lambda i: (i, 0)),
        compiler_params=_cparams(("parallel",), 52),
        name=name,
    )(act, w, xs, mods, pg, pb)


def _prep_kernel(cur_ref, prev_ref, next_ref, rqk_ref, cw_ref, cb_ref, rope_ref, o_ref, *, nctx, nblk, tb):
    t = pl.program_id(1)
    seg_start = jnp.logical_or(t == 0, t == nctx)
    seg_end = jnp.logical_or(t == nctx - 1, t == nblk - 1)
    x = cur_ref[...].astype(F32)
    nc = x.shape[1]
    prev_row = jnp.where(seg_start, 0.0, prev_ref[...].astype(F32)[15:16, :])
    next_row = jnp.where(seg_end, 0.0, next_ref[...].astype(F32)[0:1, :])
    ridx = lax.broadcasted_iota(jnp.int32, (tb, 1), 0)
    xp = jnp.where(ridx == 0, prev_row, pltpu.roll(x, 1, 0))
    xn = jnp.where(ridx == tb - 1, next_row, pltpu.roll(x, tb - 1, 0))
    y = cb_ref[...] + xp * cw_ref[0:1, :] + x * cw_ref[1:2, :] + xn * cw_ref[2:3, :]
    o_ref[:, 0:nc] = _silu(y).astype(BF16)

    cos = rope_ref[:, 0:LANES]
    sin = rope_ref[:, LANES:2 * LANES]
    lane = lax.broadcasted_iota(jnp.int32, (1, LANES), 1)
    first = (lane % 64) < 32
    for blk in range(rqk_ref.shape[1] // LANES):
        xh = rqk_ref[:, blk * LANES:(blk + 1) * LANES].astype(F32)
        partner = jnp.where(first, pltpu.roll(xh, 96, 1), pltpu.roll(xh, 32, 1))
        o_ref[:, nc + blk * LANES:nc + (blk + 1) * LANES] = (xh * cos + partner * sin).astype(BF16)


def _prep_call(um3, conv_w, conv_b, rope, nctx, tb):
    b, s_tot, _ = um3.shape
    nblk = s_tot // tb
    hb = tb // 16
    nc = conv_w.shape[1]
    return pl.pallas_call(
        functools.partial(_prep_kernel, nctx=nctx, nblk=nblk, tb=tb),
        out_shape=jax.ShapeDtypeStruct((b, s_tot, 2 * nc), BF16),
        grid=(b, nblk),
        in_specs=[pl.BlockSpec((None, tb, nc), lambda i, t: (i, t, 2)),
                  pl.BlockSpec((None, 16, nc), lambda i, t: (i, jnp.maximum(t * hb - 1, 0), 2)),
                  pl.BlockSpec((None, 16, nc), lambda i, t: (i, jnp.minimum((t + 1) * hb, nblk * hb - 1), 2)),
                  pl.BlockSpec((None, tb, nc), lambda i, t: (i, t, 3)),
                  pl.BlockSpec((8, nc), lambda i, t: (0, 0)),
                  pl.BlockSpec((1, nc), lambda i, t: (0, 0)),
                  pl.BlockSpec((tb, 2 * LANES), lambda i, t: (t, 0))],
        out_specs=pl.BlockSpec((None, tb, 2 * nc), lambda i, t: (i, t, 0)),
        compiler_params=_cparams(("parallel", "parallel"), 32),
        name="prep_conv_rope",
    )(um3, um3, um3, um3, conv_w, conv_b, rope)


def _mlstm_a(rev, L, gate_ref, bias_ref):
    lane = lax.broadcasted_iota(jnp.int32, (1, LANES), 1)
    is_f = jnp.logical_and(lane >= 8, lane < 16)
    g = gate_ref[...] + bias_ref[0:1, :]
    gp = jnp.where(is_f, _log_sigmoid(g), g)
    return gp, _tri_cumsum(_causal(L, rev).astype(BF16), gp)


def _mlstm_b(qkvo_ref, caug_ref):
    gw = 4 * LANES
    out = []
    for h in range(M_HEADS):
        q = qkvo_ref[:, h * LANES:(h + 1) * LANES]
        k = qkvo_ref[:, gw + h * LANES:gw + (h + 1) * LANES]
        out.append((_dot_nt(q, k), _dot(q, caug_ref[h].astype(BF16))))
    return out


def _mlstm_c(rev, final, L, pa, pb, qkvo_ref, caug_ref, m_ref, o_ref, oc, yb_ref, nrm_ref):
    d = 1 if rev else 0
    gw = 4 * LANES
    gp, cum = pa
    mask = _causal(L, rev)
    e_idx = 0 if rev else L - 1
    ones = jnp.ones((L, LANES), BF16)
    gp_t = gp.T
    cum_t = cum.T
    for h in range(M_HEADS):
        ci = 4 * d + h
        cf = 8 + 4 * d + h
        b_col = cum[:, cf:cf + 1]
        b_row = cum_t[cf:cf + 1, :]
        ig_col = gp[:, ci:ci + 1]
        ig_row = gp_t[ci:ci + 1, :]
        m_st = m_ref[h, 0:1, 0:1]
        log_intra = jnp.where(mask, b_col - b_row + ig_row, NEG)
        log_inter = b_col + m_st
        m_i = jnp.maximum(log_inter, jnp.max(log_intra, axis=1, keepdims=True))
        w = jnp.exp(log_intra - m_i)
        w_inter = jnp.exp(log_inter - m_i)
        b_end = cum[e_idx:e_idx + 1, cf:cf + 1]
        ls_row = b_end - b_row + ig_row
        m_new = jnp.maximum(b_end + m_st, jnp.max(ls_row, axis=1, keepdims=True))
        ws_col = jnp.exp(b_end - b_col + ig_col - m_new)
        decay = jnp.exp(b_end + m_st - m_new)
        k = qkvo_ref[:, gw + h * LANES:gw + (h + 1) * LANES]
        v = qkvo_ref[:, 2 * gw + h * LANES:2 * gw + (h + 1) * LANES]
        vaug = jnp.concatenate([v, ones], axis=1)
        qk, qc = pb[h]
        s = (qk * w).astype(BF16)
        res = _dot(s, vaug) + w_inter * qc
        kws = (k.astype(F32) * ws_col).astype(BF16)
        caug_ref[h] = decay * caug_ref[h] + _dot_tn(kws, vaug)
        m_ref[h] = jnp.broadcast_to(m_new, (8, LANES))
        hh = res[:, 0:LANES] / jnp.maximum(jnp.abs(res[:, LANES:2 * LANES]), jnp.exp(-m_i))
        hc = slice(h * LANES, (h + 1) * LANES)
        cols = slice(oc + h * LANES, oc + (h + 1) * LANES)
        if final:
            y = hh + yb_ref[:, cols]
            yc = y - jnp.mean(y, axis=-1, keepdims=True)
            yn = yc * lax.rsqrt(jnp.mean(yc * yc, axis=-1, keepdims=True) + EPS)
            og = qkvo_ref[:, 3 * gw + h * LANES:3 * gw + (h + 1) * LANES].astype(F32)
            o_ref[:, cols] = (_sigmoid(og) * (yn * nrm_ref[0:1, hc])).astype(BF16)
        else:
            o_ref[:, cols] = hh


def _ssd_a(rev, L, gate_ref, prm_ref):
    dt_all = _softplus(gate_ref[...] + prm_ref[0:1, :])
    return dt_all, _tri_cumsum(_causal(L, rev).astype(BF16), dt_all * (-jnp.exp(prm_ref[1:2, :])))


def _ssd_b(xbc_ref, st_ref):
    gw = 4 * LANES
    out = []
    for g in range(S_G):
        bg = xbc_ref[:, gw + g * S_N:gw + (g + 1) * S_N]
        cg = xbc_ref[:, gw + S_G * S_N + g * S_N:gw + S_G * S_N + (g + 1) * S_N]
        out.append((_dot_nt(cg, bg), _dot(cg, st_ref[g].astype(BF16))))
    return out


def _ssd_c(rev, final, L, pa, pb, xbc_ref, st_ref, o_ref, oc, z_ref, yb_ref, nrm_ref):
    d = 1 if rev else 0
    gw = 4 * LANES
    hpg = S_HEADS // S_G
    gcols = gw // S_G
    dt_all, cum = pa
    mask = _causal(L, rev)
    e_idx = 0 if rev else L - 1
    lane_head = lax.broadcasted_iota(jnp.int32, (1, gcols), 1) // (gcols // hpg)
    cum_t = cum.T
    ys = []
    for g in range(S_G):
        wts, e_cols, dec_cols, dt_cols, decays = [], [], [], [], []
        for hl in range(hpg):
            c = 16 + S_HEADS * d + g * hpg + hl
            b_col = cum[:, c:c + 1]
            b_row = cum_t[c:c + 1, :]
            b_end = cum[e_idx:e_idx + 1, c:c + 1]
            wts.append(jnp.exp(jnp.where(mask, b_col - b_row, NEG)))
            e_cols.append(jnp.exp(b_col))
            dec_cols.append(jnp.exp(b_end - b_col))
            dt_cols.append(dt_all[:, c:c + 1])
            decays.append(jnp.exp(b_end))
        gc = slice(g * gcols, (g + 1) * gcols)
        xs_g = xbc_ref[:, gc].astype(F32)
        bg = xbc_ref[:, gw + g * S_N:gw + (g + 1) * S_N]
        xdt = xs_g * _expand4(dt_cols, lane_head)
        gm, inter = pb[g]
        acc = _expand4(e_cols, lane_head) * inter
        for hl in range(hpg):
            xh = jnp.where(lane_head == hl, xdt, 0.0).astype(BF16)
            acc = acc + _dot((gm * wts[hl]).astype(BF16), xh)
        xdec = (xdt * _expand4(dec_cols, lane_head)).astype(BF16)
        st_ref[g] = _expand4(decays, lane_head) * st_ref[g] + _dot_tn(bg, xdec)
        cols = slice(oc + g * gcols, oc + (g + 1) * gcols)
        if final:
            acc = acc + yb_ref[:, cols] + nrm_ref[4:5, gc] * xs_g
            ys.append(acc * _silu(z_ref[:, gc].astype(F32)))
        else:
            o_ref[:, cols] = acc
    if final:
        ms = sum(jnp.sum(y * y, axis=-1, keepdims=True) for y in ys) * (1.0 / gw)
        inv = lax.rsqrt(ms + EPS)
        for g in range(S_G):
            gc = slice(g * gcols, (g + 1) * gcols)
            o_ref[:, oc + g * gcols:oc + (g + 1) * gcols] = (ys[g] * inv * nrm_ref[1:2, gc]).astype(BF16)


def _ret_b(qk_ref, qc, st_ref):
    gw = 4 * LANES
    out = []
    for h in range(R_HEADS):
        q = qk_ref[:, qc + h * LANES:qc + (h + 1) * LANES]
        k = qk_ref[:, qc + gw + h * LANES:qc + gw + (h + 1) * LANES]
        out.append((_dot_nt(q, k), _dot(q, st_ref[h].astype(BF16))))
    return out


def _ret_c(rev, final, L, pb, qk_ref, qc, v_ref, prm_ref, st_ref, o_ref, oc, g_ref, yb_ref, nrm_ref):
    d = 1 if rev else 0
    gw = 4 * LANES
    mask = _causal(L, rev)
    icol = lax.broadcasted_iota(jnp.int32, (L, 1), 0).astype(F32)
    row = lax.broadcasted_iota(jnp.int32, (L, L), 0)
    col = lax.broadcasted_iota(jnp.int32, (L, L), 1)
    dist = ((col - row) if rev else (row - col)).astype(F32)
    lg_all = -jnp.exp(prm_ref[d:d + 1, :])
    for h in range(R_HEADS):
        lg = lg_all[:, h:h + 1]
        w = jnp.exp(jnp.where(mask, dist * lg, NEG))
        if rev:
            e_col = jnp.exp((float(L) - icol) * lg)
            dec_col = jnp.exp(icol * lg)
        else:
            e_col = jnp.exp((icol + 1.0) * lg)
            dec_col = jnp.exp((float(L - 1) - icol) * lg)
        decay = jnp.exp(float(L) * lg)
        hc = slice(h * LANES, (h + 1) * LANES)
        k = qk_ref[:, qc + gw + h * LANES:qc + gw + (h + 1) * LANES]
        v = v_ref[:, hc]
        qk, qst = pb[h]
        y = _dot((qk * w).astype(BF16), v) + e_col * qst
        kd = (k.astype(F32) * dec_col).astype(BF16)
        st_ref[h] = decay * st_ref[h] + _dot_tn(kd, v)
        cols = slice(oc + h * LANES, oc + (h + 1) * LANES)
        if final:
            y = y + yb_ref[:, cols]
            yc = y - jnp.mean(y, axis=-1, keepdims=True)
            yn = yc * lax.rsqrt(jnp.mean(yc * yc, axis=-1, keepdims=True) + EPS)
            o_ref[:, cols] = (yn * nrm_ref[2:3, hc] * _silu(g_ref[:, hc].astype(F32))).astype(BF16)
        else:
            o_ref[:, cols] = y


def _gla_a(rev, L, tb, gate_ref, gw_ref, gb_ref):
    row = lax.broadcasted_iota(jnp.int32, (tb, tb), 0)
    col = lax.broadcasted_iota(jnp.int32, (tb, tb), 1)
    blockmask = jnp.logical_and((row // L) == (col // L), (col >= row) if rev else (col <= row))
    z = _dot_f32(gate_ref[...], gw_ref[...]) + gb_ref[...]
    return blockmask, _tri_cumsum(blockmask.astype(BF16), _log_sigmoid(z) * (1.0 / G_TAU))


def _gla_c(rev, final, L, nsub, pa, qk_ref, v_ref, st_ref, o_ref, oc, g_ref, yb_ref, nrm_ref):
    kw = 2 * LANES
    blockmask, cum = pa
    e_idx = 0 if rev else L - 1
    lane_head = lax.broadcasted_iota(jnp.int32, (1, kw), 1) // (kw // G_HEADS)
    sub = [slice(c * L, (c + 1) * L) for c in range(nsub)]
    qf = qk_ref[:, 0:kw].astype(F32)
    kf = qk_ref[:, kw:2 * kw].astype(F32)
    qg = qf * jnp.exp(cum)
    kg = (kf * jnp.exp(-cum)).astype(BF16)
    b_ends = [cum[c * L + e_idx:c * L + e_idx + 1, :] for c in range(nsub)]
    kd = jnp.concatenate([kf[sub[c], :] * jnp.exp(b_ends[c] - cum[sub[c], :]) for c in range(nsub)], axis=0)
    qgh = [jnp.where(lane_head == h, qg, 0.0).astype(BF16) for h in range(G_HEADS)]
    kdh = [jnp.where(lane_head == h, kd, 0.0).astype(BF16) for h in range(G_HEADS)]
    s_all = [_dot_nt(qgh[h], kg) for h in range(G_HEADS)]
    upd = [sum(_dot_tn(v_ref[sub[c], h * LANES:(h + 1) * LANES], kdh[h][sub[c], :]) for h in range(G_HEADS))
           for c in range(nsub)]
    st = st_ref[...]
    st_at = [None] * nsub
    for c in (reversed(range(nsub)) if rev else range(nsub)):
        st_at[c] = st.astype(BF16)
        st = jnp.exp(b_ends[c]) * st + upd[c]
    st_ref[...] = st
    for h in range(G_HEADS):
        hc = slice(h * LANES, (h + 1) * LANES)
        cols = slice(oc + h * LANES, oc + (h + 1) * LANES)
        inter = jnp.concatenate([_dot_nt(qgh[h][sub[c], :], st_at[c]) for c in range(nsub)], axis=0)
        y = _dot(jnp.where(blockmask, s_all[h], 0.0).astype(BF16), v_ref[:, hc]) + inter
        if final:
            y = y + yb_ref[:, cols]
            yn = y * lax.rsqrt(jnp.mean(y * y, axis=-1, keepdims=True) + EPS)
            o_ref[:, cols] = (yn * nrm_ref[3:4, hc] * _silu(g_ref[:, hc].astype(F32))).astype(BF16)
        else:
            o_ref[:, cols] = y


def _scan_kernel(*refs, rev, final, tb):
    (um_m, ug, prep, r_v, g_qk, g_v, m_bias, s_prm, r_prm, gwp, gb) = refs[:11]
    if final:
        r_g, s_z, g_g, yb, nrm = refs[11:16]
        o_ref = refs[16]
        scratch = refs[17:]
    else:
        r_g = s_z = g_g = yb = nrm = None
        o_ref = refs[11]
        scratch = refs[12:]
    caug, m_st, s_st, r_st, g_st = scratch
    gw = 4 * LANES

    @pl.when(pl.program_id(1) == 0)
    def _():
        for ref in scratch:
            ref[...] = jnp.zeros_like(ref)

    m_a = _mlstm_a(rev, tb, ug, m_bias)
    s_a = _ssd_a(rev, tb, ug, s_prm)
    g_a = _gla_a(rev, GLA_CHUNK, tb, ug, gwp, gb)
    r_b = _ret_b(prep, 2 * gw, r_st)
    m_b = _mlstm_b(um_m, caug)
    s_b = _ssd_b(prep, s_st)
    _ret_c(rev, final, tb, r_b, prep, 2 * gw, r_v, r_prm, r_st, o_ref, 2 * gw, r_g, yb, nrm)
    _gla_c(rev, final, GLA_CHUNK, tb // GLA_CHUNK, g_a, g_qk, g_v, g_st, o_ref, 3 * gw, g_g, yb, nrm)
    _mlstm_c(rev, final, tb, m_a, m_b, um_m, caug, m_st, o_ref, 0, yb, nrm)
    _ssd_c(rev, final, tb, s_a, s_b, prep, s_st, o_ref, gw, s_z, yb, nrm)


def _scan_call(rev, final, tb, nctx, um3, ug3, prep, params, yb, nrm):
    b, s_tot, _ = um3.shape
    gw = 4 * LANES
    nblk = s_tot // tb
    if rev:
        def order(c):
            return jnp.where(c < nctx, nctx - 1 - c, nblk + nctx - 1 - c)
    else:
        def order(c):
            return c

    def tok(width, cb):
        return pl.BlockSpec((None, tb, width), lambda i, c: (i, order(c), cb))

    def full(arr):
        return pl.BlockSpec(arr.shape, lambda i, c: (0,) * arr.ndim)

    ins = [um3, ug3, prep, um3, um3, um3] + list(params)
    in_specs = [tok(4 * gw, 0), tok(GATE_W, 0), tok(4 * gw, 0), tok(gw, 8), tok(gw, 11), tok(gw, 12)]
    in_specs += [full(p) for p in params]
    if final:
        ins += [um3, um3, um3, yb, nrm]
        in_specs += [tok(gw, 9), tok(gw, 10), tok(gw, 13), tok(4 * gw, 0), full(nrm)]
    scratch = [pltpu.VMEM((M_HEADS, LANES, 2 * LANES), F32), pltpu.VMEM((M_HEADS, 8, LANES), F32),
               pltpu.VMEM((S_G, S_N, gw // S_G), F32), pltpu.VMEM((R_HEADS, LANES, LANES), F32),
               pltpu.VMEM((LANES, 2 * LANES), F32)]
    return pl.pallas_call(
        functools.partial(_scan_kernel, rev=rev, final=final, tb=tb),
        out_shape=jax.ShapeDtypeStruct((b, s_tot, 4 * gw), BF16 if final else F32),
        grid=(b, nblk),
        in_specs=in_specs,
        out_specs=tok(4 * gw, 0),
        scratch_shapes=scratch,
        compiler_params=_cparams(("parallel", "arbitrary"), 48),
        name="scan_fwd" if final else "scan_bwd",
    )(*ins)


def _in_col_layout(d_model):
    gw = d_model // N_GROUPS
    conv_ch = gw + 2 * S_G * S_N
    names = [('m_q', gw), ('m_k', gw), ('m_v', gw), ('m_o', gw), ('m_i', 2 * M_HEADS), ('m_f', 2 * M_HEADS),
             ('s_z', gw), ('s_xbc', conv_ch), ('s_dt', 2 * S_HEADS),
             ('r_q', gw), ('r_k', gw), ('r_v', gw), ('r_g', gw),
             ('g_q', gw // 2), ('g_k', gw // 2), ('g_v', gw), ('g_g', gw), ('g_a', 2 * G_RANK)]
    off, o = {}, 0
    for nm, n in names:
        off[nm] = (o, n)
        o += n
    return off


def kernel(x, c, ctx, c_ctx, ada_w, ada_b, w_in, m_ig_b, m_fg_b, m_norm, s_conv_w, s_conv_b, s_dt_bias, s_a_log, s_d, s_norm, r_decay, r_norm, g_gate_w, g_gate_b, g_norm, w_out, post_g, post_b, ffn_w_up, ffn_w_down):
    b, seq, d = x.shape
    n_ctx_tok = ctx.shape[1]
    depth = ada_w.shape[0]
    gw = d // N_GROUPS
    assert gw == 4 * LANES and b + 1 <= 8
    s_tot = n_ctx_tok + seq
    t = b * s_tot
    tb = math.gcd(MAX_TOKEN_BLOCK, math.gcd(n_ctx_tok, seq))
    assert tb % 16 == 0 and tb % GLA_CHUNK == 0
    nctx = n_ctx_tok // tb
    nblk = s_tot // tb
    nsb = t // tb
    tm = max(m for m in (1024, 512, 256, 128, 64, 32, 16) if t % m == 0 and m % tb == 0)
    tm_out = max(m for m in (512, 256, 128, 64, 32, 16) if t % m == 0 and m % tb == 0)
    alpha = (2.0 * depth) ** 0.25

    off = _in_col_layout(d)
    main_order = ['m_q', 'm_k', 'm_v', 'm_o', 's_xbc', 'r_q', 'r_k', 'r_v', 'r_g', 's_z', 'g_q', 'g_k', 'g_v', 'g_g']
    gate_order = ['m_i', 'm_f', 's_dt', 'g_a']
    col_scale = {'m_q': float(LANES) ** -0.5, 'r_k': float(LANES) ** -0.5, 'g_q': float(gw // 2 // G_HEADS) ** -0.5}
    n_gate = sum(off[n][1] for n in gate_order)

    def cols_of(wl, name):
        o, n = off[name]
        w = wl[:, o:o + n]
        return w * col_scale[name] if name in col_scale else w

    pos = jnp.arange(seq)
    quarter = LANES // 4
    freqs = 1.0 / (ROPE_BASE ** (jnp.arange(quarter, dtype=F32) / quarter))
    ang_r = (pos // GRID_W).astype(F32)[:, None] * freqs[None, :]
    ang_c = (pos % GRID_W).astype(F32)[:, None] * freqs[None, :]
    cos_t = jnp.concatenate([jnp.cos(ang_r)] * 2 + [jnp.cos(ang_c)] * 2, axis=1)
    sin_t = jnp.concatenate([-jnp.sin(ang_r), jnp.sin(ang_r), -jnp.sin(ang_c), jnp.sin(ang_c)], axis=1)
    rope = jnp.concatenate([
        jnp.concatenate([jnp.ones((n_ctx_tok, LANES), F32), jnp.zeros((n_ctx_tok, LANES), F32)], axis=1),
        jnp.concatenate([cos_t, sin_t], axis=1)], axis=0)

    xs = jnp.concatenate([ctx, x], axis=1).reshape(t, d)
    cc = jnp.concatenate([c, c_ctx[None, :], jnp.zeros((8 - b - 1, d), F32)], axis=0)
    mod = _mod_call(cc, ada_w, ada_b)
    sb = np.arange(nsb)
    mod_row = np.where(sb % nblk < nctx, b, sb // nblk)

    def pad_lanes(v, start, width=LANES):
        return jnp.zeros((width,), F32).at[start:start + v.shape[0]].set(v)

    w_out_b = w_out.astype(BF16)
    w_up_b = ffn_w_up.astype(BF16)
    w_down_b = ffn_w_down.astype(BF16)
    for l in range(depth):
        mods = jnp.pad(mod[l][mod_row].reshape(nsb, 6, d), ((0, 0), (0, 2), (0, 0)))
        wl = w_in[l]
        wm = jnp.concatenate([cols_of(wl, n) for n in main_order], axis=1).astype(BF16)
        wg = jnp.concatenate([cols_of(wl, n) for n in gate_order] + [jnp.zeros((d, GATE_W - n_gate), F32)],
                             axis=1).astype(BF16)
        um, ug = _in_call(xs, mods, wm, wg, tm, tb)
        um3 = um.reshape(b, s_tot, um.shape[1])
        ug3 = ug.reshape(b, s_tot, GATE_W)

        conv_w = jnp.pad(s_conv_w[l], ((0, 8 - CONV_W), (0, 0)))
        prep = _prep_call(um3, conv_w, s_conv_b[l][None, :], rope, nctx, tb)

        m_bias = (pad_lanes(m_ig_b[l].reshape(-1), 0) + pad_lanes(m_fg_b[l].reshape(-1), 2 * M_HEADS))[None, :]
        s_prm = jnp.zeros((8, LANES), F32).at[0].set(pad_lanes(s_dt_bias[l].reshape(-1), 16)).at[1].set(
            pad_lanes(s_a_log[l].reshape(-1), 16))
        r_prm = jnp.zeros((8, LANES), F32).at[0].set(pad_lanes(r_decay[l][0], 0)).at[1].set(pad_lanes(r_decay[l][1], 0))
        nrm = jnp.zeros((8, gw), F32).at[0].set(m_norm[l]).at[1].set(s_norm[l]).at[2].set(r_norm[l]).at[3].set(
            g_norm[l]).at[4].set(jnp.repeat(s_d[l], gw // S_HEADS))

        def dir_params(dd):
            gwp = jnp.zeros((GATE_W, gw // 2), F32).at[32 + G_RANK * dd:32 + G_RANK * (dd + 1)].set(g_gate_w[l][dd])
            return [m_bias, s_prm, r_prm, gwp, g_gate_b[l][dd][None, :]]

        yb = _scan_call(True, False, tb, nctx, um3, ug3, prep, dir_params(1), None, None)
        mix = _scan_call(False, True, tb, nctx, um3, ug3, prep, dir_params(0), yb, nrm)

        x1 = _res_call(mix.reshape(t, 4 * gw), w_out_b, l, xs, mods,
                       post_g[l, 0][None, :], post_b[l, 0][None, :], tm_out, tb, 2, alpha, "out_proj_res_ln")
        act = _up_call(x1, mods, w_up_b, l, tm, tb)
        xs = _res_call(act, w_down_b, l, x1, mods,
                       post_g[l, 1][None, :], post_b[l, 1][None, :], tb, tb, 5, alpha, "ffn_down_res_ln")

    return xs.reshape(b, s_tot, d)[:, n_ctx_tok:, :]
```

```python
import functools
import math

import numpy as np
import jax
import jax.numpy as jnp
from jax import lax
from jax.experimental import pallas as pl
from jax.experimental.pallas import tpu as pltpu

F32 = jnp.float32
BF16 = jnp.bfloat16

EPS = 1e-5
ROPE_BASE = 10000.0
GRID_W = 64
N_GROUPS = 4
M_HEADS = 4
S_HEADS = 8
S_G = 2
S_N = 128
CONV_W = 3
R_HEADS = 4
G_HEADS = 4
G_RANK = 16
G_TAU = 16.0
NEG = -1e30

LANES = 128
GATE_W = LANES
MAX_TOKEN_BLOCK = 256
GLA_CHUNK = 64


def _cparams(sem, vmem_mb):
    return pltpu.CompilerParams(dimension_semantics=sem, vmem_limit_bytes=vmem_mb * 1024 * 1024)


def _dot(a, b):
    return jnp.dot(a, b, preferred_element_type=F32)


def _dot_nt(a, b):
    return lax.dot_general(a, b, (((1,), (1,)), ((), ())), preferred_element_type=F32)


def _dot_tn(a, b):
    return lax.dot_general(a, b, (((0,), (0,)), ((), ())), preferred_element_type=F32)


def _split3(x):
    hi = x.astype(BF16)
    r1 = x - hi.astype(F32)
    mid = r1.astype(BF16)
    lo = (r1 - mid.astype(F32)).astype(BF16)
    return hi, mid, lo


def _tri_cumsum(tri, x):
    hi, mid, lo = _split3(x)
    return _dot(tri, hi) + _dot(tri, mid) + _dot(tri, lo)


def _dot_f32(a, b):
    ah = a.astype(BF16)
    al = (a - ah.astype(F32)).astype(BF16)
    bh = b.astype(BF16)
    bl = (b - bh.astype(F32)).astype(BF16)
    return _dot(ah, bh) + _dot(ah, bl) + _dot(al, bh)


def _sigmoid(x):
    return 1.0 / (1.0 + jnp.exp(-x))


def _silu(x):
    return x * _sigmoid(x)


def _softplus(x):
    return jnp.maximum(x, 0.0) + jnp.log1p(jnp.exp(-jnp.abs(x)))


def _log_sigmoid(x):
    return jnp.minimum(x, 0.0) - jnp.log1p(jnp.exp(-jnp.abs(x)))


def _ln(x):
    mu = jnp.mean(x, axis=-1, keepdims=True)
    xc = x - mu
    var = jnp.mean(xc * xc, axis=-1, keepdims=True)
    return xc * lax.rsqrt(var + EPS)


def _causal(n, rev):
    row = lax.broadcasted_iota(jnp.int32, (n, n), 0)
    col = lax.broadcasted_iota(jnp.int32, (n, n), 1)
    return (col >= row) if rev else (col <= row)


def _expand4(cols, lane_head):
    return jnp.where(lane_head == 0, cols[0],
                     jnp.where(lane_head == 1, cols[1],
                               jnp.where(lane_head == 2, cols[2], cols[3])))


def _mod_kernel(c_ref, w_ref, b_ref, o_ref):
    o_ref[0] = _dot(_silu(c_ref[...]).astype(BF16), w_ref[0].astype(BF16)) + b_ref[0]


def _mod_call(cc, ada_w, ada_b):
    depth, d, n = ada_w.shape
    tn = 1024
    return pl.pallas_call(
        _mod_kernel,
        out_shape=jax.ShapeDtypeStruct((depth, 8, n), F32),
        grid=(depth, n // tn),
        in_specs=[pl.BlockSpec((8, d), lambda l, j: (0, 0)),
                  pl.BlockSpec((1, d, tn), lambda l, j: (l, 0, j)),
                  pl.BlockSpec((1, 1, tn), lambda l, j: (l, 0, j))],
        out_specs=pl.BlockSpec((1, 8, tn), lambda l, j: (l, 0, j)),
        compiler_params=_cparams(("parallel", "parallel"), 40),
        name="adaln_mod",
    )(cc, ada_w, ada_b.reshape(depth, 1, n))


def _ln_mod_to(h_ref, x_ref, mods_ref, nsub, tb, sh_row, sc_row):
    for r in range(nsub):
        x = x_ref[r * tb:(r + 1) * tb, :]
        h = _ln(x) * (1.0 + mods_ref[r, sc_row:sc_row + 1, :]) + mods_ref[r, sh_row:sh_row + 1, :]
        h_ref[r * tb:(r + 1) * tb, :] = h.astype(BF16)


def _in_kernel(x_ref, mods_ref, wm_ref, wg_ref, um_ref, ug_ref, h_ref, *, nsub, tb):
    @pl.when(pl.program_id(1) == 0)
    def _():
        _ln_mod_to(h_ref, x_ref, mods_ref, nsub, tb, 0, 1)
        ug_ref[...] = _dot(h_ref[...], wg_ref[...])

    um_ref[...] = _dot(h_ref[...], wm_ref[...]).astype(BF16)


def _in_call(xs, mods, wm, wg, tm, tb):
    t, d = xs.shape
    n = wm.shape[1]
    tn = 1024
    nsub = tm // tb
    return pl.pallas_call(
        functools.partial(_in_kernel, nsub=nsub, tb=tb),
        out_shape=(jax.ShapeDtypeStruct((t, n), BF16), jax.ShapeDtypeStruct((t, GATE_W), F32)),
        grid=(t // tm, n // tn),
        in_specs=[pl.BlockSpec((tm, d), lambda i, j: (i, 0)),
                  pl.BlockSpec((nsub, 8, d), lambda i, j: (i, 0, 0)),
                  pl.BlockSpec((d, tn), lambda i, j: (0, j)),
                  pl.BlockSpec((d, GATE_W), lambda i, j: (0, 0))],
        out_specs=(pl.BlockSpec((tm, tn), lambda i, j: (i, j)),
                   pl.BlockSpec((tm, GATE_W), lambda i, j: (i, 0))),
        scratch_shapes=[pltpu.VMEM((tm, d), BF16)],
        compiler_params=_cparams(("parallel", "arbitrary"), 48),
        name="ln_in_proj",
    )(xs, mods, wm, wg)


def _up_kernel(x_ref, mods_ref, wa_ref, wg_ref, o_ref, h_ref, *, nsub, tb):
    @pl.when(pl.program_id(1) == 0)
    def _():
        _ln_mod_to(h_ref, x_ref, mods_ref, nsub, tb, 3, 4)

    a = _dot(h_ref[...], wa_ref[...])
    g = _dot(h_ref[...], wg_ref[...])
    o_ref[...] = (_silu(a) * g).astype(BF16)


def _up_call(xs, mods, w_up, layer, tm, tb):
    t, d = xs.shape
    dff = w_up.shape[2] // 2
    tn = 512
    nj = dff // tn
    nsub = tm // tb
    return pl.pallas_call(
        functools.partial(_up_kernel, nsub=nsub, tb=tb),
        out_shape=jax.ShapeDtypeStruct((t, dff), BF16),
        grid=(t // tm, nj),
        in_specs=[pl.BlockSpec((tm, d), lambda i, j: (i, 0)),
                  pl.BlockSpec((nsub, 8, d), lambda i, j: (i, 0, 0)),
                  pl.BlockSpec((None, d, tn), lambda i, j: (layer, 0, j)),
                  pl.BlockSpec((None, d, tn), lambda i, j: (layer, 0, j + nj))],
        out_specs=pl.BlockSpec((tm, tn), lambda i, j: (i, j)),
        scratch_shapes=[pltpu.VMEM((tm, d), BF16)],
        compiler_params=_cparams(("parallel", "arbitrary"), 48),
        name="ln_ffn_up",
    )(xs, mods, w_up, w_up)


def _res_kernel(a_ref, w_ref, x_ref, mods_ref, pg_ref, pb_ref, o_ref, *, nsub, tb, g_row, alpha):
    y = _dot(a_ref[...], w_ref[...])
    for r in range(nsub):
        rows = slice(r * tb, (r + 1) * tb)
        z = alpha * x_ref[rows, :] + mods_ref[r, g_row:g_row + 1, :] * y[rows, :]
        o_ref[rows, :] = _ln(z) * pg_ref[...] + pb_ref[...]


def _res_call(act, w, layer, xs, mods, pg, pb, tm, tb, g_row, alpha, name):
    t, d = xs.shape
    ka = act.shape[1]
    nsub = tm // tb
    return pl.pallas_call(
        functools.partial(_res_kernel, nsub=nsub, tb=tb, g_row=g_row, alpha=alpha),
        out_shape=jax.ShapeDtypeStruct((t, d), F32),
        grid=(t // tm,),
        in_specs=[pl.BlockSpec((tm, ka), lambda i: (i, 0)),
                  pl.BlockSpec((None, ka, d), lambda i: (layer, 0, 0), pipeline_mode=pl.Buffered(1)),
                  pl.BlockSpec((tm, d), lambda i: (i, 0)),
                  pl.BlockSpec((nsub, 8, d), lambda i: (i, 0, 0)),
                  pl.BlockSpec((1, d), lambda i: (0, 0)),
                  pl.BlockSpec((1, d), lambda i: (0, 0))],
        out_specs=pl.BlockSpec((tm, d), lambda i: (i, 0)),
        compiler_params=_cparams(("parallel",), 52),
        name=name,
    )(act, w, xs, mods, pg, pb)


def _head_norm(y, center):
    if center:
        y = y - jnp.mean(y, axis=-1, keepdims=True)
    return y * lax.rsqrt(jnp.mean(y * y, axis=-1, keepdims=True) + EPS)


def _mix_finalize(act_ref, rows, ys_ref, mo_ref, sz_ref, rg_ref, gg_ref, nrm_ref):
    gw = 4 * LANES
    for h in range(M_HEADS):
        hc = slice(h * LANES, (h + 1) * LANES)
        yn = _head_norm(ys_ref[rows, hc].astype(F32), True)
        act_ref[rows, hc] = (_sigmoid(mo_ref[rows, hc].astype(F32)) * (yn * nrm_ref[0:1, hc])).astype(BF16)
    halves = [slice(g * (gw // S_G), (g + 1) * (gw // S_G)) for g in range(S_G)]
    ys = [ys_ref[rows, gw + gc.start:gw + gc.stop].astype(F32) * _silu(sz_ref[rows, gc].astype(F32)) for gc in halves]
    inv = lax.rsqrt(sum(jnp.sum(y * y, axis=-1, keepdims=True) for y in ys) * (1.0 / gw) + EPS)
    for y, gc in zip(ys, halves):
        act_ref[rows, gw + gc.start:gw + gc.stop] = (y * inv * nrm_ref[1:2, gc]).astype(BF16)
    for h in range(R_HEADS):
        hc = slice(h * LANES, (h + 1) * LANES)
        yn = _head_norm(ys_ref[rows, 2 * gw + h * LANES:2 * gw + (h + 1) * LANES].astype(F32), True)
        act_ref[rows, 2 * gw + h * LANES:2 * gw + (h + 1) * LANES] = (
            yn * nrm_ref[2:3, hc] * _silu(rg_ref[rows, hc].astype(F32))).astype(BF16)
    for h in range(G_HEADS):
        hc = slice(h * LANES, (h + 1) * LANES)
        yn = _head_norm(ys_ref[rows, 3 * gw + h * LANES:3 * gw + (h + 1) * LANES].astype(F32), False)
        act_ref[rows, 3 * gw + h * LANES:3 * gw + (h + 1) * LANES] = (
            yn * nrm_ref[3:4, hc] * _silu(gg_ref[rows, hc].astype(F32))).astype(BF16)


def _out_kernel(ys_ref, mo_ref, sz_ref, rg_ref, gg_ref, nrm_ref, w_ref, x_ref, mods_ref, pg_ref, pb_ref,
                o_ref, act_ref, *, nsub, tb, alpha):
    for r in range(nsub):
        rows = slice(r * tb, (r + 1) * tb)
        _mix_finalize(act_ref, rows, ys_ref, mo_ref, sz_ref, rg_ref, gg_ref, nrm_ref)
        y = _dot(act_ref[rows, :], w_ref[...])
        z = alpha * x_ref[rows, :] + mods_ref[r, 2:3, :] * y
        o_ref[rows, :] = _ln(z) * pg_ref[...] + pb_ref[...]


def _out_call(ysum, um, nrm, w, layer, xs, mods, pg, pb, tm, tb, alpha):
    t, d = xs.shape
    gw = d // N_GROUPS
    nsub = tm // tb

    def gate(cb):
        return pl.BlockSpec((tm, gw), lambda i: (i, cb))

    return pl.pallas_call(
        functools.partial(_out_kernel, nsub=nsub, tb=tb, alpha=alpha),
        out_shape=jax.ShapeDtypeStruct((t, d), F32),
        grid=(t // tm,),
        in_specs=[pl.BlockSpec((tm, d), lambda i: (i, 0)), gate(3), gate(10), gate(9), gate(13),
                  pl.BlockSpec(nrm.shape, lambda i: (0, 0)),
                  pl.BlockSpec((None, d, d), lambda i: (layer, 0, 0), pipeline_mode=pl.Buffered(1)),
                  pl.BlockSpec((tm, d), lambda i: (i, 0)),
                  pl.BlockSpec((nsub, 8, d), lambda i: (i, 0, 0)),
                  pl.BlockSpec((1, d), lambda i: (0, 0)),
                  pl.BlockSpec((1, d), lambda i: (0, 0))],
        out_specs=pl.BlockSpec((tm, d), lambda i: (i, 0)),
        scratch_shapes=[pltpu.VMEM((tm, d), BF16)],
        compiler_params=_cparams(("parallel",), 52),
        name="mix_out_proj_res_ln",
    )(ysum, um, um, um, um, nrm, w, xs, mods, pg, pb)


def _prep_kernel(cur_ref, prev_ref, next_ref, rqk_ref, cw_ref, cb_ref, rope_ref, o_ref, *, nctx, nblk, tb):
    t = pl.program_id(1)
    seg_start = jnp.logical_or(t == 0, t == nctx)
    seg_end = jnp.logical_or(t == nctx - 1, t == nblk - 1)
    nc = cur_ref.shape[1]
    ridx = lax.broadcasted_iota(jnp.int32, (tb, 1), 0)
    for blk in range(nc // LANES):
        cs = slice(blk * LANES, (blk + 1) * LANES)
        x = cur_ref[:, cs].astype(F32)
        prev_row = jnp.where(seg_start, 0.0, prev_ref[:, cs].astype(F32)[15:16, :])
        next_row = jnp.where(seg_end, 0.0, next_ref[:, cs].astype(F32)[0:1, :])
        xp = jnp.where(ridx == 0, prev_row, pltpu.roll(x, 1, 0))
        xn = jnp.where(ridx == tb - 1, next_row, pltpu.roll(x, tb - 1, 0))
        y = cb_ref[:, cs] + xp * cw_ref[0:1, cs] + x * cw_ref[1:2, cs] + xn * cw_ref[2:3, cs]
        o_ref[:, cs] = _silu(y).astype(BF16)

    cos = rope_ref[:, 0:LANES]
    sin = rope_ref[:, LANES:2 * LANES]
    lane = lax.broadcasted_iota(jnp.int32, (1, LANES), 1)
    first = (lane % 64) < 32
    for blk in range(rqk_ref.shape[1] // LANES):
        xh = rqk_ref[:, blk * LANES:(blk + 1) * LANES].astype(F32)
        partner = jnp.where(first, pltpu.roll(xh, 96, 1), pltpu.roll(xh, 32, 1))
        o_ref[:, nc + blk * LANES:nc + (blk + 1) * LANES] = (xh * cos + partner * sin).astype(BF16)


def _prep_call(um3, conv_w, conv_b, rope, nctx, tb):
    b, s_tot, _ = um3.shape
    nblk = s_tot // tb
    hb = tb // 16
    nc = conv_w.shape[1]
    return pl.pallas_call(
        functools.partial(_prep_kernel, nctx=nctx, nblk=nblk, tb=tb),
        out_shape=jax.ShapeDtypeStruct((b, s_tot, 2 * nc), BF16),
        grid=(b, nblk),
        in_specs=[pl.BlockSpec((None, tb, nc), lambda i, t: (i, t, 2)),
                  pl.BlockSpec((None, 16, nc), lambda i, t: (i, jnp.maximum(t * hb - 1, 0), 2)),
                  pl.BlockSpec((None, 16, nc), lambda i, t: (i, jnp.minimum((t + 1) * hb, nblk * hb - 1), 2)),
                  pl.BlockSpec((None, tb, nc), lambda i, t: (i, t, 3)),
                  pl.BlockSpec((8, nc), lambda i, t: (0, 0)),
                  pl.BlockSpec((1, nc), lambda i, t: (0, 0)),
                  pl.BlockSpec((tb, 2 * LANES), lambda i, t: (t, 0))],
        out_specs=pl.BlockSpec((None, tb, 2 * nc), lambda i, t: (i, t, 0)),
        compiler_params=_cparams(("parallel", "parallel"), 32),
        name="prep_conv_rope",
    )(um3, um3, um3, um3, conv_w, conv_b, rope)


def _mlstm_a(rev, L, gate_ref, bias_ref):
    lane = lax.broadcasted_iota(jnp.int32, (1, LANES), 1)
    is_f = jnp.logical_and(lane >= 8, lane < 16)
    g = gate_ref[...] + bias_ref[0:1, :]
    gp = jnp.where(is_f, _log_sigmoid(g), g)
    return gp, _tri_cumsum(_causal(L, rev).astype(BF16), gp)


def _mlstm_b(qkvo_ref, caug_ref):
    gw = 4 * LANES
    out = []
    for h in range(M_HEADS):
        q = qkvo_ref[:, h * LANES:(h + 1) * LANES]
        k = qkvo_ref[:, gw + h * LANES:gw + (h + 1) * LANES]
        out.append((_dot_nt(q, k), _dot(q, caug_ref[h].astype(BF16))))
    return out


def _mlstm_c(rev, final, L, pa, pb, qkvo_ref, caug_ref, m_ref, o_ref, oc, yb_ref):
    d = 1 if rev else 0
    gw = 4 * LANES
    gp, cum = pa
    mask = _causal(L, rev)
    e_idx = 0 if rev else L - 1
    ones = jnp.ones((L, LANES), BF16)
    gp_t = gp.T
    cum_t = cum.T
    for h in range(M_HEADS):
        ci = 4 * d + h
        cf = 8 + 4 * d + h
        b_col = cum[:, cf:cf + 1]
        b_row = cum_t[cf:cf + 1, :]
        ig_col = gp[:, ci:ci + 1]
        ig_row = gp_t[ci:ci + 1, :]
        m_st = m_ref[h, 0:1, 0:1]
        log_intra = jnp.where(mask, b_col - b_row + ig_row, NEG)
        log_inter = b_col + m_st
        m_i = jnp.maximum(log_inter, jnp.max(log_intra, axis=1, keepdims=True))
        w = jnp.exp(log_intra - m_i)
        w_inter = jnp.exp(log_inter - m_i)
        b_end = cum[e_idx:e_idx + 1, cf:cf + 1]
        ls_row = b_end - b_row + ig_row
        m_new = jnp.maximum(b_end + m_st, jnp.max(ls_row, axis=1, keepdims=True))
        ws_col = jnp.exp(b_end - b_col + ig_col - m_new)
        decay = jnp.exp(b_end + m_st - m_new)
        k = qkvo_ref[:, gw + h * LANES:gw + (h + 1) * LANES]
        v = qkvo_ref[:, 2 * gw + h * LANES:2 * gw + (h + 1) * LANES]
        vaug = jnp.concatenate([v, ones], axis=1)
        qk, qc = pb[h]
        s = (qk * w).astype(BF16)
        res = _dot(s, vaug) + w_inter * qc
        kws = (k.astype(F32) * ws_col).astype(BF16)
        caug_ref[h] = decay * caug_ref[h] + _dot_tn(kws, vaug)
        m_ref[h] = jnp.broadcast_to(m_new, (8, LANES))
        hh = res[:, 0:LANES] / jnp.maximum(jnp.abs(res[:, LANES:2 * LANES]), jnp.exp(-m_i))
        cols = slice(oc + h * LANES, oc + (h + 1) * LANES)
        if final:
            o_ref[:, cols] = (hh + yb_ref[:, cols]).astype(BF16)
        else:
            o_ref[:, cols] = hh


def _ssd_a(rev, L, gate_ref, prm_ref):
    dt_all = _softplus(gate_ref[...] + prm_ref[0:1, :])
    return dt_all, _tri_cumsum(_causal(L, rev).astype(BF16), dt_all * (-jnp.exp(prm_ref[1:2, :])))


def _ssd_b(xbc_ref, st_ref):
    gw = 4 * LANES
    out = []
    for g in range(S_G):
        bg = xbc_ref[:, gw + g * S_N:gw + (g + 1) * S_N]
        cg = xbc_ref[:, gw + S_G * S_N + g * S_N:gw + S_G * S_N + (g + 1) * S_N]
        out.append((_dot_nt(cg, bg), _dot(cg, st_ref[g].astype(BF16))))
    return out


def _ssd_c(rev, final, L, pa, pb, xbc_ref, st_ref, o_ref, oc, yb_ref, dsk_ref):
    d = 1 if rev else 0
    gw = 4 * LANES
    hpg = S_HEADS // S_G
    gcols = gw // S_G
    dt_all, cum = pa
    mask = _causal(L, rev)
    e_idx = 0 if rev else L - 1
    lane_head = lax.broadcasted_iota(jnp.int32, (1, gcols), 1) // (gcols // hpg)
    cum_t = cum.T
    for g in range(S_G):
        wts, e_cols, dec_cols, dt_cols, decays = [], [], [], [], []
        for hl in range(hpg):
            c = 16 + S_HEADS * d + g * hpg + hl
            b_col = cum[:, c:c + 1]
            b_row = cum_t[c:c + 1, :]
            b_end = cum[e_idx:e_idx + 1, c:c + 1]
            wts.append(jnp.exp(jnp.where(mask, b_col - b_row, NEG)))
            e_cols.append(jnp.exp(b_col))
            dec_cols.append(jnp.exp(b_end - b_col))
            dt_cols.append(dt_all[:, c:c + 1])
            decays.append(jnp.exp(b_end))
        gc = slice(g * gcols, (g + 1) * gcols)
        xs_g = xbc_ref[:, gc].astype(F32)
        bg = xbc_ref[:, gw + g * S_N:gw + (g + 1) * S_N]
        xdt = xs_g * _expand4(dt_cols, lane_head)
        gm, inter = pb[g]
        acc = _expand4(e_cols, lane_head) * inter
        for hl in range(hpg):
            xh = jnp.where(lane_head == hl, xdt, 0.0).astype(BF16)
            acc = acc + _dot((gm * wts[hl]).astype(BF16), xh)
        xdec = (xdt * _expand4(dec_cols, lane_head)).astype(BF16)
        st_ref[g] = _expand4(decays, lane_head) * st_ref[g] + _dot_tn(bg, xdec)
        cols = slice(oc + g * gcols, oc + (g + 1) * gcols)
        if final:
            o_ref[:, cols] = (acc + yb_ref[:, cols] + dsk_ref[0:1, gc] * xs_g).astype(BF16)
        else:
            o_ref[:, cols] = acc


def _ret_b(qk_ref, qc, st_ref):
    gw = 4 * LANES
    out = []
    for h in range(R_HEADS):
        q = qk_ref[:, qc + h * LANES:qc + (h + 1) * LANES]
        k = qk_ref[:, qc + gw + h * LANES:qc + gw + (h + 1) * LANES]
        out.append((_dot_nt(q, k), _dot(q, st_ref[h].astype(BF16))))
    return out


def _ret_c(rev, final, L, pb, qk_ref, qc, v_ref, prm_ref, st_ref, o_ref, oc, yb_ref):
    d = 1 if rev else 0
    gw = 4 * LANES
    mask = _causal(L, rev)
    icol = lax.broadcasted_iota(jnp.int32, (L, 1), 0).astype(F32)
    row = lax.broadcasted_iota(jnp.int32, (L, L), 0)
    col = lax.broadcasted_iota(jnp.int32, (L, L), 1)
    dist = ((col - row) if rev else (row - col)).astype(F32)
    lg_all = -jnp.exp(prm_ref[d:d + 1, :])
    for h in range(R_HEADS):
        lg = lg_all[:, h:h + 1]
        w = jnp.exp(jnp.where(mask, dist * lg, NEG))
        if rev:
            e_col = jnp.exp((float(L) - icol) * lg)
            dec_col = jnp.exp(icol * lg)
        else:
            e_col = jnp.exp((icol + 1.0) * lg)
            dec_col = jnp.exp((float(L - 1) - icol) * lg)
        decay = jnp.exp(float(L) * lg)
        hc = slice(h * LANES, (h + 1) * LANES)
        k = qk_ref[:, qc + gw + h * LANES:qc + gw + (h + 1) * LANES]
        v = v_ref[:, hc]
        qk, qst = pb[h]
        y = _dot((qk * w).astype(BF16), v) + e_col * qst
        kd = (k.astype(F32) * dec_col).astype(BF16)
        st_ref[h] = decay * st_ref[h] + _dot_tn(kd, v)
        cols = slice(oc + h * LANES, oc + (h + 1) * LANES)
        if final:
            o_ref[:, cols] = (y + yb_ref[:, cols]).astype(BF16)
        else:
            o_ref[:, cols] = y


def _gla_a(rev, L, tb, gate_ref, gw_ref, gb_ref):
    row = lax.broadcasted_iota(jnp.int32, (tb, tb), 0)
    col = lax.broadcasted_iota(jnp.int32, (tb, tb), 1)
    blockmask = jnp.logical_and((row // L) == (col // L), (col >= row) if rev else (col <= row))
    z = _dot_f32(gate_ref[...], gw_ref[...]) + gb_ref[...]
    return blockmask, _tri_cumsum(blockmask.astype(BF16), _log_sigmoid(z) * (1.0 / G_TAU))


def _gla_c(rev, final, L, nsub, pa, qk_ref, v_ref, st_ref, o_ref, oc, yb_ref):
    kw = 2 * LANES
    blockmask, cum = pa
    e_idx = 0 if rev else L - 1
    lane_head = lax.broadcasted_iota(jnp.int32, (1, kw), 1) // (kw // G_HEADS)
    sub = [slice(c * L, (c + 1) * L) for c in range(nsub)]
    qf = qk_ref[:, 0:kw].astype(F32)
    kf = qk_ref[:, kw:2 * kw].astype(F32)
    qg = qf * jnp.exp(cum)
    kg = (kf * jnp.exp(-cum)).astype(BF16)
    b_ends = [cum[c * L + e_idx:c * L + e_idx + 1, :] for c in range(nsub)]
    kd = jnp.concatenate([kf[sub[c], :] * jnp.exp(b_ends[c] - cum[sub[c], :]) for c in range(nsub)], axis=0)
    qgh = [jnp.where(lane_head == h, qg, 0.0).astype(BF16) for h in range(G_HEADS)]
    kdh = [jnp.where(lane_head == h, kd, 0.0).astype(BF16) for h in range(G_HEADS)]
    s_all = [_dot_nt(qgh[h], kg) for h in range(G_HEADS)]
    upd = [sum(_dot_tn(v_ref[sub[c], h * LANES:(h + 1) * LANES], kdh[h][sub[c], :]) for h in range(G_HEADS))
           for c in range(nsub)]
    st = st_ref[...]
    st_at = [None] * nsub
    for c in (reversed(range(nsub)) if rev else range(nsub)):
        st_at[c] = st.astype(BF16)
        st = jnp.exp(b_ends[c]) * st + upd[c]
    st_ref[...] = st
    for h in range(G_HEADS):
        hc = slice(h * LANES, (h + 1) * LANES)
        cols = slice(oc + h * LANES, oc + (h + 1) * LANES)
        inter = jnp.concatenate([_dot_nt(qgh[h][sub[c], :], st_at[c]) for c in range(nsub)], axis=0)
        y = _dot(jnp.where(blockmask, s_all[h], 0.0).astype(BF16), v_ref[:, hc]) + inter
        if final:
            o_ref[:, cols] = (y + yb_ref[:, cols]).astype(BF16)
        else:
            o_ref[:, cols] = y


def _scan_kernel(*refs, rev, final, tb):
    (um_m, ug, prep, r_v, g_qk, g_v, m_bias, s_prm, r_prm, gwp, gb) = refs[:11]
    if final:
        yb, dsk = refs[11:13]
        o_ref = refs[13]
        scratch = refs[14:]
    else:
        yb = dsk = None
        o_ref = refs[11]
        scratch = refs[12:]
    caug, m_st, s_st, r_st, g_st = scratch
    gw = 4 * LANES

    @pl.when(pl.program_id(1) == 0)
    def _():
        for ref in scratch:
            ref[...] = jnp.zeros_like(ref)

    m_a = _mlstm_a(rev, tb, ug, m_bias)
    s_a = _ssd_a(rev, tb, ug, s_prm)
    g_a = _gla_a(rev, GLA_CHUNK, tb, ug, gwp, gb)
    r_b = _ret_b(prep, 2 * gw, r_st)
    m_b = _mlstm_b(um_m, caug)
    s_b = _ssd_b(prep, s_st)
    _ret_c(rev, final, tb, r_b, prep, 2 * gw, r_v, r_prm, r_st, o_ref, 2 * gw, yb)
    _gla_c(rev, final, GLA_CHUNK, tb // GLA_CHUNK, g_a, g_qk, g_v, g_st, o_ref, 3 * gw, yb)
    _mlstm_c(rev, final, tb, m_a, m_b, um_m, caug, m_st, o_ref, 0, yb)
    _ssd_c(rev, final, tb, s_a, s_b, prep, s_st, o_ref, gw, yb, dsk)


def _scan_call(rev, final, tb, nctx, um3, ug3, prep, params, yb, dsk):
    b, s_tot, _ = um3.shape
    gw = 4 * LANES
    nblk = s_tot // tb
    if rev:
        def order(c):
            return jnp.where(c < nctx, nctx - 1 - c, nblk + nctx - 1 - c)
    else:
        def order(c):
            return c

    def tok(width, cb):
        return pl.BlockSpec((None, tb, width), lambda i, c: (i, order(c), cb))

    def full(arr):
        return pl.BlockSpec(arr.shape, lambda i, c: (0,) * arr.ndim)

    ins = [um3, ug3, prep, um3, um3, um3] + list(params)
    in_specs = [tok(4 * gw, 0), tok(GATE_W, 0), tok(4 * gw, 0), tok(gw, 8), tok(gw, 11), tok(gw, 12)]
    in_specs += [full(p) for p in params]
    if final:
        ins += [yb, dsk]
        in_specs += [tok(4 * gw, 0), full(dsk)]
    scratch = [pltpu.VMEM((M_HEADS, LANES, 2 * LANES), F32), pltpu.VMEM((M_HEADS, 8, LANES), F32),
               pltpu.VMEM((S_G, S_N, gw // S_G), F32), pltpu.VMEM((R_HEADS, LANES, LANES), F32),
               pltpu.VMEM((LANES, 2 * LANES), F32)]
    return pl.pallas_call(
        functools.partial(_scan_kernel, rev=rev, final=final, tb=tb),
        out_shape=jax.ShapeDtypeStruct((b, s_tot, 4 * gw), BF16 if final else F32),
        grid=(b, nblk),
        in_specs=in_specs,
        out_specs=tok(4 * gw, 0),
        scratch_shapes=scratch,
        compiler_params=_cparams(("parallel", "arbitrary"), 48),
        name="scan_fwd" if final else "scan_bwd",
    )(*ins)


def _in_col_layout(d_model):
    gw = d_model // N_GROUPS
    conv_ch = gw + 2 * S_G * S_N
    names = [('m_q', gw), ('m_k', gw), ('m_v', gw), ('m_o', gw), ('m_i', 2 * M_HEADS), ('m_f', 2 * M_HEADS),
             ('s_z', gw), ('s_xbc', conv_ch), ('s_dt', 2 * S_HEADS),
             ('r_q', gw), ('r_k', gw), ('r_v', gw), ('r_g', gw),
             ('g_q', gw // 2), ('g_k', gw // 2), ('g_v', gw), ('g_g', gw), ('g_a', 2 * G_RANK)]
    off, o = {}, 0
    for nm, n in names:
        off[nm] = (o, n)
        o += n
    return off


def kernel(x, c, ctx, c_ctx, ada_w, ada_b, w_in, m_ig_b, m_fg_b, m_norm, s_conv_w, s_conv_b, s_dt_bias, s_a_log, s_d, s_norm, r_decay, r_norm, g_gate_w, g_gate_b, g_norm, w_out, post_g, post_b, ffn_w_up, ffn_w_down):
    b, seq, d = x.shape
    n_ctx_tok = ctx.shape[1]
    depth = ada_w.shape[0]
    gw = d // N_GROUPS
    assert gw == 4 * LANES and b + 1 <= 8
    s_tot = n_ctx_tok + seq
    t = b * s_tot
    tb = math.gcd(MAX_TOKEN_BLOCK, math.gcd(n_ctx_tok, seq))
    assert tb % 16 == 0 and tb % GLA_CHUNK == 0
    nctx = n_ctx_tok // tb
    nblk = s_tot // tb
    nsb = t // tb
    tm = max(m for m in (1024, 512, 256, 128, 64, 32, 16) if t % m == 0 and m % tb == 0)
    tm_out = max(m for m in (512, 256, 128, 64, 32, 16) if t % m == 0 and m % tb == 0)
    alpha = (2.0 * depth) ** 0.25

    off = _in_col_layout(d)
    main_order = ['m_q', 'm_k', 'm_v', 'm_o', 's_xbc', 'r_q', 'r_k', 'r_v', 'r_g', 's_z', 'g_q', 'g_k', 'g_v', 'g_g']
    gate_order = ['m_i', 'm_f', 's_dt', 'g_a']
    col_scale = {'m_q': float(LANES) ** -0.5, 'r_k': float(LANES) ** -0.5, 'g_q': float(gw // 2 // G_HEADS) ** -0.5}
    n_gate = sum(off[n][1] for n in gate_order)

    def cols_of(wl, name):
        o, n = off[name]
        w = wl[:, o:o + n]
        return w * col_scale[name] if name in col_scale else w

    pos = jnp.arange(seq)
    quarter = LANES // 4
    freqs = 1.0 / (ROPE_BASE ** (jnp.arange(quarter, dtype=F32) / quarter))
    ang_r = (pos // GRID_W).astype(F32)[:, None] * freqs[None, :]
    ang_c = (pos % GRID_W).astype(F32)[:, None] * freqs[None, :]
    cos_t = jnp.concatenate([jnp.cos(ang_r)] * 2 + [jnp.cos(ang_c)] * 2, axis=1)
    sin_t = jnp.concatenate([-jnp.sin(ang_r), jnp.sin(ang_r), -jnp.sin(ang_c), jnp.sin(ang_c)], axis=1)
    rope = jnp.concatenate([
        jnp.concatenate([jnp.ones((n_ctx_tok, LANES), F32), jnp.zeros((n_ctx_tok, LANES), F32)], axis=1),
        jnp.concatenate([cos_t, sin_t], axis=1)], axis=0)

    xs = jnp.concatenate([ctx, x], axis=1).reshape(t, d)
    cc = jnp.concatenate([c, c_ctx[None, :], jnp.zeros((8 - b - 1, d), F32)], axis=0)
    mod = _mod_call(cc, ada_w, ada_b)
    sb = np.arange(nsb)
    mod_row = np.where(sb % nblk < nctx, b, sb // nblk)

    def pad_lanes(v, start, width=LANES):
        return jnp.zeros((width,), F32).at[start:start + v.shape[0]].set(v)

    w_out_b = w_out.astype(BF16)
    w_up_b = ffn_w_up.astype(BF16)
    w_down_b = ffn_w_down.astype(BF16)
    for l in range(depth):
        mods = jnp.pad(mod[l][mod_row].reshape(nsb, 6, d), ((0, 0), (0, 2), (0, 0)))
        wl = w_in[l]
        wm = jnp.concatenate([cols_of(wl, n) for n in main_order], axis=1).astype(BF16)
        wg = jnp.concatenate([cols_of(wl, n) for n in gate_order] + [jnp.zeros((d, GATE_W - n_gate), F32)],
                             axis=1).astype(BF16)
        um, ug = _in_call(xs, mods, wm, wg, tm, tb)
        um3 = um.reshape(b, s_tot, um.shape[1])
        ug3 = ug.reshape(b, s_tot, GATE_W)

        conv_w = jnp.pad(s_conv_w[l], ((0, 8 - CONV_W), (0, 0)))
        prep = _prep_call(um3, conv_w, s_conv_b[l][None, :], rope, nctx, tb)

        m_bias = (pad_lanes(m_ig_b[l].reshape(-1), 0) + pad_lanes(m_fg_b[l].reshape(-1), 2 * M_HEADS))[None, :]
        s_prm = jnp.zeros((8, LANES), F32).at[0].set(pad_lanes(s_dt_bias[l].reshape(-1), 16)).at[1].set(
            pad_lanes(s_a_log[l].reshape(-1), 16))
        r_prm = jnp.zeros((8, LANES), F32).at[0].set(pad_lanes(r_decay[l][0], 0)).at[1].set(pad_lanes(r_decay[l][1], 0))
        nrm = jnp.zeros((8, gw), F32).at[0].set(m_norm[l]).at[1].set(s_norm[l]).at[2].set(r_norm[l]).at[3].set(
            g_norm[l])
        dsk = jnp.repeat(s_d[l], gw // S_HEADS)[None, :]

        def dir_params(dd):
            gwp = jnp.zeros((GATE_W, gw // 2), F32).at[32 + G_RANK * dd:32 + G_RANK * (dd + 1)].set(g_gate_w[l][dd])
            return [m_bias, s_prm, r_prm, gwp, g_gate_b[l][dd][None, :]]

        yb = _scan_call(True, False, tb, nctx, um3, ug3, prep, dir_params(1), None, None)
        ysum = _scan_call(False, True, tb, nctx, um3, ug3, prep, dir_params(0), yb, dsk)

        x1 = _out_call(ysum.reshape(t, 4 * gw), um, nrm, w_out_b, l, xs, mods,
                       post_g[l, 0][None, :], post_b[l, 0][None, :], tm_out, tb, alpha)
        act = _up_call(x1, mods, w_up_b, l, tm, tb)
        xs = _res_call(act, w_down_b, l, x1, mods,
                       post_g[l, 1][None, :], post_b[l, 1][None, :], tb, tb, 5, alpha, "ffn_down_res_ln")

    return xs.reshape(b, s_tot, d)[:, n_ctx_tok:, :]
```

```python
import functools
import math

import numpy as np
import jax
import jax.numpy as jnp
from jax import lax
from jax.experimental import pallas as pl
from jax.experimental.pallas import tpu as pltpu

F32 = jnp.float32
BF16 = jnp.bfloat16

EPS = 1e-5
ROPE_BASE = 10000.0
GRID_W = 64
N_GROUPS = 4
M_HEADS = 4
S_HEADS = 8
S_G = 2
S_N = 128
CONV_W = 3
R_HEADS = 4
G_HEADS = 4
G_RANK = 16
G_TAU = 16.0
NEG = -1e30

LANES = 128
GATE_W = LANES
MAX_TOKEN_BLOCK = 256
GLA_CHUNK = 64


def _cparams(sem, vmem_mb):
    return pltpu.CompilerParams(dimension_semantics=sem, vmem_limit_bytes=vmem_mb * 1024 * 1024)


def _dot(a, b):
    return jnp.dot(a, b, preferred_element_type=F32)


def _dot_nt(a, b):
    return lax.dot_general(a, b, (((1,), (1,)), ((), ())), preferred_element_type=F32)


def _dot_tn(a, b):
    return lax.dot_general(a, b, (((0,), (0,)), ((), ())), preferred_element_type=F32)


def _split3(x):
    hi = x.astype(BF16)
    r1 = x - hi.astype(F32)
    mid = r1.astype(BF16)
    lo = (r1 - mid.astype(F32)).astype(BF16)
    return hi, mid, lo


def _tri_cumsum(tri, x):
    hi, mid, lo = _split3(x)
    return _dot(tri, hi) + _dot(tri, mid) + _dot(tri, lo)


def _dot_f32(a, b):
    ah = a.astype(BF16)
    al = (a - ah.astype(F32)).astype(BF16)
    bh = b.astype(BF16)
    bl = (b - bh.astype(F32)).astype(BF16)
    return _dot(ah, bh) + _dot(ah, bl) + _dot(al, bh)


def _sigmoid(x):
    return 1.0 / (1.0 + jnp.exp(-x))


def _silu(x):
    return x * _sigmoid(x)


def _softplus(x):
    return jnp.maximum(x, 0.0) + jnp.log1p(jnp.exp(-jnp.abs(x)))


def _log_sigmoid(x):
    return jnp.minimum(x, 0.0) - jnp.log1p(jnp.exp(-jnp.abs(x)))


def _ln(x):
    mu = jnp.mean(x, axis=-1, keepdims=True)
    xc = x - mu
    var = jnp.mean(xc * xc, axis=-1, keepdims=True)
    return xc * lax.rsqrt(var + EPS)


def _causal(n, rev):
    row = lax.broadcasted_iota(jnp.int32, (n, n), 0)
    col = lax.broadcasted_iota(jnp.int32, (n, n), 1)
    return (col >= row) if rev else (col <= row)


def _expand4(cols, lane_head):
    return jnp.where(lane_head == 0, cols[0],
                     jnp.where(lane_head == 1, cols[1],
                               jnp.where(lane_head == 2, cols[2], cols[3])))


def _mod_kernel(c_ref, w_ref, b_ref, o_ref):
    part = _dot(_silu(c_ref[...]).astype(BF16), w_ref[0].astype(BF16))

    @pl.when(pl.program_id(1) == 0)
    def _():
        o_ref[0] = part + b_ref[0]

    @pl.when(pl.program_id(1) > 0)
    def _():
        o_ref[0] += part


def _mod_call(cc, ada_w, ada_b):
    depth, d, n = ada_w.shape
    tk = 256
    return pl.pallas_call(
        _mod_kernel,
        out_shape=jax.ShapeDtypeStruct((depth, 8, n), F32),
        grid=(depth, d // tk),
        in_specs=[pl.BlockSpec((8, tk), lambda l, k: (0, k)),
                  pl.BlockSpec((1, tk, n), lambda l, k: (l, k, 0)),
                  pl.BlockSpec((1, 1, n), lambda l, k: (l, 0, 0))],
        out_specs=pl.BlockSpec((1, 8, n), lambda l, k: (l, 0, 0)),
        compiler_params=_cparams(("parallel", "arbitrary"), 40),
        name="adaln_mod",
    )(cc, ada_w, ada_b.reshape(depth, 1, n))


def _ln_mod_to(h_ref, x_ref, mods_ref, nsub, tb, sh_row, sc_row):
    for r in range(nsub):
        x = x_ref[r * tb:(r + 1) * tb, :]
        h = _ln(x) * (1.0 + mods_ref[r, sc_row:sc_row + 1, :]) + mods_ref[r, sh_row:sh_row + 1, :]
        h_ref[r * tb:(r + 1) * tb, :] = h.astype(BF16)


def _in_kernel(x_ref, mods_ref, wm_ref, wg_ref, um_ref, ug_ref, h_ref, *, nsub, tb):
    @pl.when(pl.program_id(1) == 0)
    def _():
        _ln_mod_to(h_ref, x_ref, mods_ref, nsub, tb, 0, 1)
        ug_ref[...] = _dot(h_ref[...], wg_ref[...])

    um_ref[...] = _dot(h_ref[...], wm_ref[...]).astype(BF16)


def _in_call(xs, mods, wm, wg, tm, tb):
    t, d = xs.shape
    n = wm.shape[1]
    tn = 1024
    nsub = tm // tb
    return pl.pallas_call(
        functools.partial(_in_kernel, nsub=nsub, tb=tb),
        out_shape=(jax.ShapeDtypeStruct((t, n), BF16), jax.ShapeDtypeStruct((t, GATE_W), F32)),
        grid=(t // tm, n // tn),
        in_specs=[pl.BlockSpec((tm, d), lambda i, j: (i, 0)),
                  pl.BlockSpec((nsub, 8, d), lambda i, j: (i, 0, 0)),
                  pl.BlockSpec((d, tn), lambda i, j: (0, j)),
                  pl.BlockSpec((d, GATE_W), lambda i, j: (0, 0))],
        out_specs=(pl.BlockSpec((tm, tn), lambda i, j: (i, j)),
                   pl.BlockSpec((tm, GATE_W), lambda i, j: (i, 0))),
        scratch_shapes=[pltpu.VMEM((tm, d), BF16)],
        compiler_params=_cparams(("parallel", "arbitrary"), 48),
        name="ln_in_proj",
    )(xs, mods, wm, wg)


def _up_kernel(x_ref, mods_ref, wa_ref, wg_ref, o_ref, h_ref, *, nsub, tb):
    @pl.when(pl.program_id(1) == 0)
    def _():
        _ln_mod_to(h_ref, x_ref, mods_ref, nsub, tb, 3, 4)

    a = _dot(h_ref[...], wa_ref[...])
    g = _dot(h_ref[...], wg_ref[...])
    o_ref[...] = (_silu(a) * g).astype(BF16)


def _up_call(xs, mods, mods_off, w_up, layer, tm, tb):
    t, d = xs.shape
    dff = w_up.shape[2] // 2
    tn = 512
    nj = dff // tn
    nsub = tm // tb
    return pl.pallas_call(
        functools.partial(_up_kernel, nsub=nsub, tb=tb),
        out_shape=jax.ShapeDtypeStruct((t, dff), BF16),
        grid=(t // tm, nj),
        in_specs=[pl.BlockSpec((tm, d), lambda i, j: (i, 0)),
                  pl.BlockSpec((nsub, 8, d), lambda i, j: (i + mods_off, 0, 0)),
                  pl.BlockSpec((None, d, tn), lambda i, j: (layer, 0, j)),
                  pl.BlockSpec((None, d, tn), lambda i, j: (layer, 0, j + nj))],
        out_specs=pl.BlockSpec((tm, tn), lambda i, j: (i, j)),
        scratch_shapes=[pltpu.VMEM((tm, d), BF16)],
        compiler_params=_cparams(("parallel", "arbitrary"), 48),
        name="ln_ffn_up",
    )(xs, mods, w_up, w_up)


def _res_kernel(a_ref, w_ref, x_ref, mods_ref, pg_ref, pb_ref, o_ref, *, nsub, tb, g_row, alpha):
    y = _dot(a_ref[...], w_ref[...])
    for r in range(nsub):
        rows = slice(r * tb, (r + 1) * tb)
        z = alpha * x_ref[rows, :] + mods_ref[r, g_row:g_row + 1, :] * y[rows, :]
        o_ref[rows, :] = _ln(z) * pg_ref[...] + pb_ref[...]


def _res_call(act, w, layer, xs, mods, mods_off, pg, pb, tm, tb, g_row, alpha, name):
    t, d = xs.shape
    ka = act.shape[1]
    nsub = tm // tb
    return pl.pallas_call(
        functools.partial(_res_kernel, nsub=nsub, tb=tb, g_row=g_row, alpha=alpha),
        out_shape=jax.ShapeDtypeStruct((t, d), F32),
        grid=(t // tm,),
        in_specs=[pl.BlockSpec((tm, ka), lambda i: (i, 0)),
                  pl.BlockSpec((None, ka, d), lambda i: (layer, 0, 0), pipeline_mode=pl.Buffered(1)),
                  pl.BlockSpec((tm, d), lambda i: (i, 0)),
                  pl.BlockSpec((nsub, 8, d), lambda i: (i + mods_off, 0, 0)),
                  pl.BlockSpec((1, d), lambda i: (0, 0)),
                  pl.BlockSpec((1, d), lambda i: (0, 0))],
        out_specs=pl.BlockSpec((tm, d), lambda i: (i, 0)),
        compiler_params=_cparams(("parallel",), 52),
        name=name,
    )(act, w, xs, mods, pg, pb)


def _head_norm(y, center):
    if center:
        y = y - jnp.mean(y, axis=-1, keepdims=True)
    return y * lax.rsqrt(jnp.mean(y * y, axis=-1, keepdims=True) + EPS)


def _mix_finalize(act_ref, rows, ys_ref, mo_ref, sz_ref, rg_ref, gg_ref, nrm_ref):
    gw = 4 * LANES
    for h in range(M_HEADS):
        hc = slice(h * LANES, (h + 1) * LANES)
        yn = _head_norm(ys_ref[rows, hc].astype(F32), True)
        act_ref[rows, hc] = (_sigmoid(mo_ref[rows, hc].astype(F32)) * (yn * nrm_ref[0:1, hc])).astype(BF16)
    halves = [slice(g * (gw // S_G), (g + 1) * (gw // S_G)) for g in range(S_G)]
    ys = [ys_ref[rows, gw + gc.start:gw + gc.stop].astype(F32) * _silu(sz_ref[rows, gc].astype(F32)) for gc in halves]
    inv = lax.rsqrt(sum(jnp.sum(y * y, axis=-1, keepdims=True) for y in ys) * (1.0 / gw) + EPS)
    for y, gc in zip(ys, halves):
        act_ref[rows, gw + gc.start:gw + gc.stop] = (y * inv * nrm_ref[1:2, gc]).astype(BF16)
    for h in range(R_HEADS):
        hc = slice(h * LANES, (h + 1) * LANES)
        yn = _head_norm(ys_ref[rows, 2 * gw + h * LANES:2 * gw + (h + 1) * LANES].astype(F32), True)
        act_ref[rows, 2 * gw + h * LANES:2 * gw + (h + 1) * LANES] = (
            yn * nrm_ref[2:3, hc] * _silu(rg_ref[rows, hc].astype(F32))).astype(BF16)
    for h in range(G_HEADS):
        hc = slice(h * LANES, (h + 1) * LANES)
        yn = _head_norm(ys_ref[rows, 3 * gw + h * LANES:3 * gw + (h + 1) * LANES].astype(F32), False)
        act_ref[rows, 3 * gw + h * LANES:3 * gw + (h + 1) * LANES] = (
            yn * nrm_ref[3:4, hc] * _silu(gg_ref[rows, hc].astype(F32))).astype(BF16)


def _out_kernel(ys_ref, mo_ref, sz_ref, rg_ref, gg_ref, nrm_ref, w_ref, x_ref, mods_ref, pg_ref, pb_ref,
                o_ref, act_ref, *, nsub, tb, alpha):
    for r in range(nsub):
        rows = slice(r * tb, (r + 1) * tb)
        _mix_finalize(act_ref, rows, ys_ref, mo_ref, sz_ref, rg_ref, gg_ref, nrm_ref)
        y = _dot(act_ref[rows, :], w_ref[...])
        z = alpha * x_ref[rows, :] + mods_ref[r, 2:3, :] * y
        o_ref[rows, :] = _ln(z) * pg_ref[...] + pb_ref[...]


def _out_call(ysum, um, nrm, w, layer, xs, mods, skip, pg, pb, tm, tb, alpha):
    t, d = xs.shape
    gw = d // N_GROUPS
    nsub = tm // tb

    def gate(cb):
        return pl.BlockSpec((tm, gw), lambda i: (i + skip, cb))

    return pl.pallas_call(
        functools.partial(_out_kernel, nsub=nsub, tb=tb, alpha=alpha),
        out_shape=jax.ShapeDtypeStruct((t - skip * tm, d), F32),
        grid=(t // tm - skip,),
        in_specs=[pl.BlockSpec((tm, d), lambda i: (i + skip, 0)), gate(3), gate(10), gate(9), gate(13),
                  pl.BlockSpec(nrm.shape, lambda i: (0, 0)),
                  pl.BlockSpec((None, d, d), lambda i: (layer, 0, 0), pipeline_mode=pl.Buffered(1)),
                  pl.BlockSpec((tm, d), lambda i: (i + skip, 0)),
                  pl.BlockSpec((nsub, 8, d), lambda i: (i + skip, 0, 0)),
                  pl.BlockSpec((1, d), lambda i: (0, 0)),
                  pl.BlockSpec((1, d), lambda i: (0, 0))],
        out_specs=pl.BlockSpec((tm, d), lambda i: (i, 0)),
        scratch_shapes=[pltpu.VMEM((tm, d), BF16)],
        compiler_params=_cparams(("parallel",), 52),
        name="mix_out_proj_res_ln",
    )(ysum, um, um, um, um, nrm, w, xs, mods, pg, pb)


def _prep_kernel(cur_ref, prev_ref, next_ref, rqk_ref, cw_ref, cb_ref, rope_ref, o_ref, *, nctx, nblk, tb):
    t = pl.program_id(1)
    seg_start = jnp.logical_or(t == 0, t == nctx)
    seg_end = jnp.logical_or(t == nctx - 1, t == nblk - 1)
    nc = cur_ref.shape[1]
    ridx = lax.broadcasted_iota(jnp.int32, (tb, 1), 0)
    for blk in range(nc // LANES):
        cs = slice(blk * LANES, (blk + 1) * LANES)
        x = cur_ref[:, cs].astype(F32)
        prev_row = jnp.where(seg_start, 0.0, prev_ref[:, cs].astype(F32)[15:16, :])
        next_row = jnp.where(seg_end, 0.0, next_ref[:, cs].astype(F32)[0:1, :])
        xp = jnp.where(ridx == 0, prev_row, pltpu.roll(x, 1, 0))
        xn = jnp.where(ridx == tb - 1, next_row, pltpu.roll(x, tb - 1, 0))
        y = cb_ref[:, cs] + xp * cw_ref[0:1, cs] + x * cw_ref[1:2, cs] + xn * cw_ref[2:3, cs]
        o_ref[:, cs] = _silu(y).astype(BF16)

    cos = rope_ref[:, 0:LANES]
    sin = rope_ref[:, LANES:2 * LANES]
    lane = lax.broadcasted_iota(jnp.int32, (1, LANES), 1)
    first = (lane % 64) < 32
    for blk in range(rqk_ref.shape[1] // LANES):
        xh = rqk_ref[:, blk * LANES:(blk + 1) * LANES].astype(F32)
        partner = jnp.where(first, pltpu.roll(xh, 96, 1), pltpu.roll(xh, 32, 1))
        o_ref[:, nc + blk * LANES:nc + (blk + 1) * LANES] = (xh * cos + partner * sin).astype(BF16)


def _flat_block(b, nctx, nblk):
    nlat = nblk - nctx
    return lambda i, t: jnp.where(t < nctx, i * nctx + t, b * nctx + i * nlat + (t - nctx))


def _prep_call(um, conv_w, conv_b, rope, b, nctx, tb):
    t_rows = um.shape[0]
    nblk = t_rows // (b * tb)
    hb = tb // 16
    nc = conv_w.shape[1]
    fb = _flat_block(b, nctx, nblk)
    return pl.pallas_call(
        functools.partial(_prep_kernel, nctx=nctx, nblk=nblk, tb=tb),
        out_shape=jax.ShapeDtypeStruct((t_rows, 2 * nc), BF16),
        grid=(b, nblk),
        in_specs=[pl.BlockSpec((tb, nc), lambda i, t: (fb(i, t), 2)),
                  pl.BlockSpec((16, nc), lambda i, t: (jnp.maximum(fb(i, t) * hb - 1, 0), 2)),
                  pl.BlockSpec((16, nc), lambda i, t: (jnp.minimum((fb(i, t) + 1) * hb, t_rows // 16 - 1), 2)),
                  pl.BlockSpec((tb, nc), lambda i, t: (fb(i, t), 3)),
                  pl.BlockSpec((8, nc), lambda i, t: (0, 0)),
                  pl.BlockSpec((1, nc), lambda i, t: (0, 0)),
                  pl.BlockSpec((tb, 2 * LANES), lambda i, t: (t, 0))],
        out_specs=pl.BlockSpec((tb, 2 * nc), lambda i, t: (fb(i, t), 0)),
        compiler_params=_cparams(("parallel", "parallel"), 32),
        name="prep_conv_rope",
    )(um, um, um, um, conv_w, conv_b, rope)


def _mlstm_a(rev, L, gate_ref, bias_ref):
    lane = lax.broadcasted_iota(jnp.int32, (1, LANES), 1)
    is_f = jnp.logical_and(lane >= 8, lane < 16)
    g = gate_ref[...] + bias_ref[0:1, :]
    gp = jnp.where(is_f, _log_sigmoid(g), g)
    return gp, _tri_cumsum(_causal(L, rev).astype(BF16), gp)


def _mlstm_b(qkvo_ref, caug_ref):
    gw = 4 * LANES
    out = []
    for h in range(M_HEADS):
        q = qkvo_ref[:, h * LANES:(h + 1) * LANES]
        k = qkvo_ref[:, gw + h * LANES:gw + (h + 1) * LANES]
        out.append((_dot_nt(q, k), _dot(q, caug_ref[h].astype(BF16))))
    return out


def _mlstm_c(rev, final, L, pa, pb, qkvo_ref, caug_ref, m_ref, o_ref, oc, yb_ref):
    d = 1 if rev else 0
    gw = 4 * LANES
    gp, cum = pa
    mask = _causal(L, rev)
    e_idx = 0 if rev else L - 1
    ones = jnp.ones((L, LANES), BF16)
    gp_t = gp.T
    cum_t = cum.T
    for h in range(M_HEADS):
        ci = 4 * d + h
        cf = 8 + 4 * d + h
        b_col = cum[:, cf:cf + 1]
        b_row = cum_t[cf:cf + 1, :]
        ig_col = gp[:, ci:ci + 1]
        ig_row = gp_t[ci:ci + 1, :]
        m_st = m_ref[h, 0:1, 0:1]
        log_intra = jnp.where(mask, b_col - b_row + ig_row, NEG)
        log_inter = b_col + m_st
        m_i = jnp.maximum(log_inter, jnp.max(log_intra, axis=1, keepdims=True))
        w = jnp.exp(log_intra - m_i)
        w_inter = jnp.exp(log_inter - m_i)
        b_end = cum[e_idx:e_idx + 1, cf:cf + 1]
        ls_row = b_end - b_row + ig_row
        m_new = jnp.maximum(b_end + m_st, jnp.max(ls_row, axis=1, keepdims=True))
        ws_col = jnp.exp(b_end - b_col + ig_col - m_new)
        decay = jnp.exp(b_end + m_st - m_new)
        k = qkvo_ref[:, gw + h * LANES:gw + (h + 1) * LANES]
        v = qkvo_ref[:, 2 * gw + h * LANES:2 * gw + (h + 1) * LANES]
        vaug = jnp.concatenate([v, ones], axis=1)
        qk, qc = pb[h]
        s = (qk * w).astype(BF16)
        res = _dot(s, vaug) + w_inter * qc
        kws = (k.astype(F32) * ws_col).astype(BF16)
        caug_ref[h] = decay * caug_ref[h] + _dot_tn(kws, vaug)
        m_ref[h] = jnp.broadcast_to(m_new, (8, LANES))
        hh = res[:, 0:LANES] / jnp.maximum(jnp.abs(res[:, LANES:2 * LANES]), jnp.exp(-m_i))
        cols = slice(oc + h * LANES, oc + (h + 1) * LANES)
        if final:
            o_ref[:, cols] = (hh + yb_ref[:, cols]).astype(BF16)
        else:
            o_ref[:, cols] = hh


def _ssd_a(rev, L, gate_ref, prm_ref):
    dt_all = _softplus(gate_ref[...] + prm_ref[0:1, :])
    return dt_all, _tri_cumsum(_causal(L, rev).astype(BF16), dt_all * (-jnp.exp(prm_ref[1:2, :])))


def _ssd_b(xbc_ref, st_ref):
    gw = 4 * LANES
    out = []
    for g in range(S_G):
        bg = xbc_ref[:, gw + g * S_N:gw + (g + 1) * S_N]
        cg = xbc_ref[:, gw + S_G * S_N + g * S_N:gw + S_G * S_N + (g + 1) * S_N]
        out.append((_dot_nt(cg, bg), _dot(cg, st_ref[g].astype(BF16))))
    return out


def _ssd_c(rev, final, L, pa, pb, xbc_ref, st_ref, o_ref, oc, yb_ref, dsk_ref):
    d = 1 if rev else 0
    gw = 4 * LANES
    hpg = S_HEADS // S_G
    gcols = gw // S_G
    dt_all, cum = pa
    mask = _causal(L, rev)
    e_idx = 0 if rev else L - 1
    lane_head = lax.broadcasted_iota(jnp.int32, (1, gcols), 1) // (gcols // hpg)
    cum_t = cum.T
    for g in range(S_G):
        wts, e_cols, dec_cols, dt_cols, decays = [], [], [], [], []
        for hl in range(hpg):
            c = 16 + S_HEADS * d + g * hpg + hl
            b_col = cum[:, c:c + 1]
            b_row = cum_t[c:c + 1, :]
            b_end = cum[e_idx:e_idx + 1, c:c + 1]
            wts.append(jnp.exp(jnp.where(mask, b_col - b_row, NEG)))
            e_cols.append(jnp.exp(b_col))
            dec_cols.append(jnp.exp(b_end - b_col))
            dt_cols.append(dt_all[:, c:c + 1])
            decays.append(jnp.exp(b_end))
        gc = slice(g * gcols, (g + 1) * gcols)
        xs_g = xbc_ref[:, gc].astype(F32)
        bg = xbc_ref[:, gw + g * S_N:gw + (g + 1) * S_N]
        xdt = xs_g * _expand4(dt_cols, lane_head)
        gm, inter = pb[g]
        acc = _expand4(e_cols, lane_head) * inter
        for hl in range(hpg):
            xh = jnp.where(lane_head == hl, xdt, 0.0).astype(BF16)
            acc = acc + _dot((gm * wts[hl]).astype(BF16), xh)
        xdec = (xdt * _expand4(dec_cols, lane_head)).astype(BF16)
        st_ref[g] = _expand4(decays, lane_head) * st_ref[g] + _dot_tn(bg, xdec)
        cols = slice(oc + g * gcols, oc + (g + 1) * gcols)
        if final:
            o_ref[:, cols] = (acc + yb_ref[:, cols] + dsk_ref[0:1, gc] * xs_g).astype(BF16)
        else:
            o_ref[:, cols] = acc


def _ret_b(qk_ref, qc, st_ref):
    gw = 4 * LANES
    out = []
    for h in range(R_HEADS):
        q = qk_ref[:, qc + h * LANES:qc + (h + 1) * LANES]
        k = qk_ref[:, qc + gw + h * LANES:qc + gw + (h + 1) * LANES]
        out.append((_dot_nt(q, k), _dot(q, st_ref[h].astype(BF16))))
    return out


def _ret_c(rev, final, L, pb, qk_ref, qc, v_ref, prm_ref, st_ref, o_ref, oc, yb_ref):
    d = 1 if rev else 0
    gw = 4 * LANES
    mask = _causal(L, rev)
    icol = lax.broadcasted_iota(jnp.int32, (L, 1), 0).astype(F32)
    row = lax.broadcasted_iota(jnp.int32, (L, L), 0)
    col = lax.broadcasted_iota(jnp.int32, (L, L), 1)
    dist = ((col - row) if rev else (row - col)).astype(F32)
    lg_all = -jnp.exp(prm_ref[d:d + 1, :])
    for h in range(R_HEADS):
        lg = lg_all[:, h:h + 1]
        w = jnp.exp(jnp.where(mask, dist * lg, NEG))
        if rev:
            e_col = jnp.exp((float(L) - icol) * lg)
            dec_col = jnp.exp(icol * lg)
        else:
            e_col = jnp.exp((icol + 1.0) * lg)
            dec_col = jnp.exp((float(L - 1) - icol) * lg)
        decay = jnp.exp(float(L) * lg)
        hc = slice(h * LANES, (h + 1) * LANES)
        k = qk_ref[:, qc + gw + h * LANES:qc + gw + (h + 1) * LANES]
        v = v_ref[:, hc]
        qk, qst = pb[h]
        y = _dot((qk * w).astype(BF16), v) + e_col * qst
        kd = (k.astype(F32) * dec_col).astype(BF16)
        st_ref[h] = decay * st_ref[h] + _dot_tn(kd, v)
        cols = slice(oc + h * LANES, oc + (h + 1) * LANES)
        if final:
            o_ref[:, cols] = (y + yb_ref[:, cols]).astype(BF16)
        else:
            o_ref[:, cols] = y


def _gla_a(rev, L, tb, gate_ref, gw_ref, gb_ref):
    row = lax.broadcasted_iota(jnp.int32, (tb, tb), 0)
    col = lax.broadcasted_iota(jnp.int32, (tb, tb), 1)
    blockmask = jnp.logical_and((row // L) == (col // L), (col >= row) if rev else (col <= row))
    z = _dot_f32(gate_ref[...], gw_ref[...]) + gb_ref[...]
    return blockmask, _tri_cumsum(blockmask.astype(BF16), _log_sigmoid(z) * (1.0 / G_TAU))


def _gla_c(rev, final, L, nsub, pa, qk_ref, v_ref, st_ref, o_ref, oc, yb_ref):
    kw = 2 * LANES
    blockmask, cum = pa
    e_idx = 0 if rev else L - 1
    lane_head = lax.broadcasted_iota(jnp.int32, (1, kw), 1) // (kw // G_HEADS)
    sub = [slice(c * L, (c + 1) * L) for c in range(nsub)]
    qf = qk_ref[:, 0:kw].astype(F32)
    kf = qk_ref[:, kw:2 * kw].astype(F32)
    qg = qf * jnp.exp(cum)
    kg = (kf * jnp.exp(-cum)).astype(BF16)
    b_ends = [cum[c * L + e_idx:c * L + e_idx + 1, :] for c in range(nsub)]
    kd = jnp.concatenate([kf[sub[c], :] * jnp.exp(b_ends[c] - cum[sub[c], :]) for c in range(nsub)], axis=0)
    qgh = [jnp.where(lane_head == h, qg, 0.0).astype(BF16) for h in range(G_HEADS)]
    kdh = [jnp.where(lane_head == h, kd, 0.0).astype(BF16) for h in range(G_HEADS)]
    s_all = [_dot_nt(qgh[h], kg) for h in range(G_HEADS)]
    upd = [sum(_dot_tn(v_ref[sub[c], h * LANES:(h + 1) * LANES], kdh[h][sub[c], :]) for h in range(G_HEADS))
           for c in range(nsub)]
    st = st_ref[...]
    st_at = [None] * nsub
    for c in (reversed(range(nsub)) if rev else range(nsub)):
        st_at[c] = st.astype(BF16)
        st = jnp.exp(b_ends[c]) * st + upd[c]
    st_ref[...] = st
    for h in range(G_HEADS):
        hc = slice(h * LANES, (h + 1) * LANES)
        cols = slice(oc + h * LANES, oc + (h + 1) * LANES)
        inter = jnp.concatenate([_dot_nt(qgh[h][sub[c], :], st_at[c]) for c in range(nsub)], axis=0)
        y = _dot(jnp.where(blockmask, s_all[h], 0.0).astype(BF16), v_ref[:, hc]) + inter
        if final:
            o_ref[:, cols] = (y + yb_ref[:, cols]).astype(BF16)
        else:
            o_ref[:, cols] = y


def _scan_kernel(*refs, rev, final, tb):
    (um_m, ug, prep, r_v, g_qk, g_v, m_bias, s_prm, r_prm, gwp, gb) = refs[:11]
    if final:
        yb, dsk = refs[11:13]
        o_ref = refs[13]
        scratch = refs[14:]
    else:
        yb = dsk = None
        o_ref = refs[11]
        scratch = refs[12:]
    caug, m_st, s_st, r_st, g_st = scratch
    gw = 4 * LANES

    @pl.when(pl.program_id(1) == 0)
    def _():
        for ref in scratch:
            ref[...] = jnp.zeros_like(ref)

    m_a = _mlstm_a(rev, tb, ug, m_bias)
    s_a = _ssd_a(rev, tb, ug, s_prm)
    g_a = _gla_a(rev, GLA_CHUNK, tb, ug, gwp, gb)
    r_b = _ret_b(prep, 2 * gw, r_st)
    m_b = _mlstm_b(um_m, caug)
    s_b = _ssd_b(prep, s_st)
    _ret_c(rev, final, tb, r_b, prep, 2 * gw, r_v, r_prm, r_st, o_ref, 2 * gw, yb)
    _gla_c(rev, final, GLA_CHUNK, tb // GLA_CHUNK, g_a, g_qk, g_v, g_st, o_ref, 3 * gw, yb)
    _mlstm_c(rev, final, tb, m_a, m_b, um_m, caug, m_st, o_ref, 0, yb)
    _ssd_c(rev, final, tb, s_a, s_b, prep, s_st, o_ref, gw, yb, dsk)


def _scan_call(rev, final, b, tb, nctx, um, ug, prep, params, yb, dsk):
    t_rows = um.shape[0]
    gw = 4 * LANES
    nblk = t_rows // (b * tb)
    fb = _flat_block(b, nctx, nblk)
    if rev:
        def order(c):
            return jnp.where(c < nctx, nctx - 1 - c, nblk + nctx - 1 - c)
    else:
        def order(c):
            return c

    def tok(width, cb):
        return pl.BlockSpec((tb, width), lambda i, c: (fb(i, order(c)), cb))

    def full(arr):
        return pl.BlockSpec(arr.shape, lambda i, c: (0,) * arr.ndim)

    ins = [um, ug, prep, um, um, um] + list(params)
    in_specs = [tok(4 * gw, 0), tok(GATE_W, 0), tok(4 * gw, 0), tok(gw, 8), tok(gw, 11), tok(gw, 12)]
    in_specs += [full(p) for p in params]
    if final:
        ins += [yb, dsk]
        in_specs += [tok(4 * gw, 0), full(dsk)]
    scratch = [pltpu.VMEM((M_HEADS, LANES, 2 * LANES), F32), pltpu.VMEM((M_HEADS, 8, LANES), F32),
               pltpu.VMEM((S_G, S_N, gw // S_G), F32), pltpu.VMEM((R_HEADS, LANES, LANES), F32),
               pltpu.VMEM((LANES, 2 * LANES), F32)]
    return pl.pallas_call(
        functools.partial(_scan_kernel, rev=rev, final=final, tb=tb),
        out_shape=jax.ShapeDtypeStruct((t_rows, 4 * gw), BF16 if final else F32),
        grid=(b, nblk),
        in_specs=in_specs,
        out_specs=tok(4 * gw, 0),
        scratch_shapes=scratch,
        compiler_params=_cparams(("parallel", "arbitrary"), 48),
        name="scan_fwd" if final else "scan_bwd",
    )(*ins)


def _in_col_layout(d_model):
    gw = d_model // N_GROUPS
    conv_ch = gw + 2 * S_G * S_N
    names = [('m_q', gw), ('m_k', gw), ('m_v', gw), ('m_o', gw), ('m_i', 2 * M_HEADS), ('m_f', 2 * M_HEADS),
             ('s_z', gw), ('s_xbc', conv_ch), ('s_dt', 2 * S_HEADS),
             ('r_q', gw), ('r_k', gw), ('r_v', gw), ('r_g', gw),
             ('g_q', gw // 2), ('g_k', gw // 2), ('g_v', gw), ('g_g', gw), ('g_a', 2 * G_RANK)]
    off, o = {}, 0
    for nm, n in names:
        off[nm] = (o, n)
        o += n
    return off


def kernel(x, c, ctx, c_ctx, ada_w, ada_b, w_in, m_ig_b, m_fg_b, m_norm, s_conv_w, s_conv_b, s_dt_bias, s_a_log, s_d, s_norm, r_decay, r_norm, g_gate_w, g_gate_b, g_norm, w_out, post_g, post_b, ffn_w_up, ffn_w_down):
    b, seq, d = x.shape
    n_ctx_tok = ctx.shape[1]
    depth = ada_w.shape[0]
    gw = d // N_GROUPS
    assert gw == 4 * LANES and b + 1 <= 8
    s_tot = n_ctx_tok + seq
    t = b * s_tot
    tb = math.gcd(MAX_TOKEN_BLOCK, math.gcd(n_ctx_tok, seq))
    assert tb % 16 == 0 and tb % GLA_CHUNK == 0
    nctx = n_ctx_tok // tb
    nblk = s_tot // tb
    nsb = t // tb
    t_ctx = b * n_ctx_tok
    t_lat = b * seq

    def row_tile(cap):
        return max(m for m in (1024, 512, 256, 128, 64, 32, 16)
                   if m <= cap and t_ctx % m == 0 and t_lat % m == 0 and m % tb == 0)

    tm = row_tile(1024)
    tm_out = row_tile(512)
    alpha = (2.0 * depth) ** 0.25

    off = _in_col_layout(d)
    main_order = ['m_q', 'm_k', 'm_v', 'm_o', 's_xbc', 'r_q', 'r_k', 'r_v', 'r_g', 's_z', 'g_q', 'g_k', 'g_v', 'g_g']
    gate_order = ['m_i', 'm_f', 's_dt', 'g_a']
    col_scale = {'m_q': float(LANES) ** -0.5, 'r_k': float(LANES) ** -0.5, 'g_q': float(gw // 2 // G_HEADS) ** -0.5}
    n_gate = sum(off[n][1] for n in gate_order)

    def cols_of(wl, name):
        o, n = off[name]
        w = wl[:, o:o + n]
        return w * col_scale[name] if name in col_scale else w

    pos = jnp.arange(seq)
    quarter = LANES // 4
    freqs = 1.0 / (ROPE_BASE ** (jnp.arange(quarter, dtype=F32) / quarter))
    ang_r = (pos // GRID_W).astype(F32)[:, None] * freqs[None, :]
    ang_c = (pos % GRID_W).astype(F32)[:, None] * freqs[None, :]
    cos_t = jnp.concatenate([jnp.cos(ang_r)] * 2 + [jnp.cos(ang_c)] * 2, axis=1)
    sin_t = jnp.concatenate([-jnp.sin(ang_r), jnp.sin(ang_r), -jnp.sin(ang_c), jnp.sin(ang_c)], axis=1)
    rope = jnp.concatenate([
        jnp.concatenate([jnp.ones((n_ctx_tok, LANES), F32), jnp.zeros((n_ctx_tok, LANES), F32)], axis=1),
        jnp.concatenate([cos_t, sin_t], axis=1)], axis=0)

    xs = jnp.concatenate([ctx.reshape(t_ctx, d), x.reshape(t_lat, d)], axis=0)
    cc = jnp.concatenate([c, c_ctx[None, :], jnp.zeros((8 - b - 1, d), F32)], axis=0)
    mod = _mod_call(cc, ada_w, ada_b)
    sb = np.arange(nsb)
    mod_row = np.where(sb < b * nctx, b, (sb - b * nctx) // (nblk - nctx))

    def pad_lanes(v, start, width=LANES):
        return jnp.zeros((width,), F32).at[start:start + v.shape[0]].set(v)

    w_out_b = w_out.astype(BF16)
    w_up_b = ffn_w_up.astype(BF16)
    w_down_b = ffn_w_down.astype(BF16)
    for l in range(depth):
        mods = jnp.pad(mod[l][mod_row].reshape(nsb, 6, d), ((0, 0), (0, 2), (0, 0)))
        wl = w_in[l]
        wm = jnp.concatenate([cols_of(wl, n) for n in main_order], axis=1).astype(BF16)
        wg = jnp.concatenate([cols_of(wl, n) for n in gate_order] + [jnp.zeros((d, GATE_W - n_gate), F32)],
                             axis=1).astype(BF16)
        um, ug = _in_call(xs, mods, wm, wg, tm, tb)

        conv_w = jnp.pad(s_conv_w[l], ((0, 8 - CONV_W), (0, 0)))
        prep = _prep_call(um, conv_w, s_conv_b[l][None, :], rope, b, nctx, tb)

        m_bias = (pad_lanes(m_ig_b[l].reshape(-1), 0) + pad_lanes(m_fg_b[l].reshape(-1), 2 * M_HEADS))[None, :]
        s_prm = jnp.zeros((8, LANES), F32).at[0].set(pad_lanes(s_dt_bias[l].reshape(-1), 16)).at[1].set(
            pad_lanes(s_a_log[l].reshape(-1), 16))
        r_prm = jnp.zeros((8, LANES), F32).at[0].set(pad_lanes(r_decay[l][0], 0)).at[1].set(pad_lanes(r_decay[l][1], 0))
        nrm = jnp.zeros((8, gw), F32).at[0].set(m_norm[l]).at[1].set(s_norm[l]).at[2].set(r_norm[l]).at[3].set(
            g_norm[l])
        dsk = jnp.repeat(s_d[l], gw // S_HEADS)[None, :]

        def dir_params(dd):
            gwp = jnp.zeros((GATE_W, gw // 2), F32).at[32 + G_RANK * dd:32 + G_RANK * (dd + 1)].set(g_gate_w[l][dd])
            return [m_bias, s_prm, r_prm, gwp, g_gate_b[l][dd][None, :]]

        yb = _scan_call(True, False, b, tb, nctx, um, ug, prep, dir_params(1), None, None)
        ysum = _scan_call(False, True, b, tb, nctx, um, ug, prep, dir_params(0), yb, dsk)

        skip = t_ctx if l == depth - 1 else 0
        x1 = _out_call(ysum, um, nrm, w_out_b, l, xs, mods, skip // tm_out,
                       post_g[l, 0][None, :], post_b[l, 0][None, :], tm_out, tb, alpha)
        act = _up_call(x1, mods, skip // tm, w_up_b, l, tm, tb)
        xs = _res_call(act, w_down_b, l, x1, mods, skip // tb,
                       post_g[l, 1][None, :], post_b[l, 1][None, :], tb, tb, 5, alpha, "ffn_down_res_ln")

    return xs.reshape(b, seq, d)
```

```python
import functools
import math

import numpy as np
import jax
import jax.numpy as jnp
from jax import lax
from jax.experimental import pallas as pl
from jax.experimental.pallas import tpu as pltpu

F32 = jnp.float32
BF16 = jnp.bfloat16

EPS = 1e-5
ROPE_BASE = 10000.0
GRID_W = 64
N_GROUPS = 4
M_HEADS = 4
S_HEADS = 8
S_G = 2
S_N = 128
CONV_W = 3
R_HEADS = 4
G_HEADS = 4
G_RANK = 16
G_TAU = 16.0
NEG = -1e30

LANES = 128
GATE_W = LANES
MAX_TOKEN_BLOCK = 256
GLA_CHUNK = 64


def _cparams(sem, vmem_mb):
    return pltpu.CompilerParams(dimension_semantics=sem, vmem_limit_bytes=vmem_mb * 1024 * 1024)


def _dot(a, b):
    return jnp.dot(a, b, preferred_element_type=F32)


def _dot_nt(a, b):
    return lax.dot_general(a, b, (((1,), (1,)), ((), ())), preferred_element_type=F32)


def _dot_tn(a, b):
    return lax.dot_general(a, b, (((0,), (0,)), ((), ())), preferred_element_type=F32)


def _split3(x):
    hi = x.astype(BF16)
    r1 = x - hi.astype(F32)
    mid = r1.astype(BF16)
    lo = (r1 - mid.astype(F32)).astype(BF16)
    return hi, mid, lo


def _tri_cumsum(tri, x):
    hi, mid, lo = _split3(x)
    return _dot(tri, hi) + _dot(tri, mid) + _dot(tri, lo)


def _dot_f32(a, b):
    ah = a.astype(BF16)
    al = (a - ah.astype(F32)).astype(BF16)
    bh = b.astype(BF16)
    bl = (b - bh.astype(F32)).astype(BF16)
    return _dot(ah, bh) + _dot(ah, bl) + _dot(al, bh)


def _sigmoid(x):
    return 1.0 / (1.0 + jnp.exp(-x))


def _silu(x):
    return x * _sigmoid(x)


def _softplus(x):
    return jnp.maximum(x, 0.0) + jnp.log1p(jnp.exp(-jnp.abs(x)))


def _log_sigmoid(x):
    return jnp.minimum(x, 0.0) - jnp.log1p(jnp.exp(-jnp.abs(x)))


def _ln(x):
    mu = jnp.mean(x, axis=-1, keepdims=True)
    xc = x - mu
    var = jnp.mean(xc * xc, axis=-1, keepdims=True)
    return xc * lax.rsqrt(var + EPS)


def _causal(n, rev):
    row = lax.broadcasted_iota(jnp.int32, (n, n), 0)
    col = lax.broadcasted_iota(jnp.int32, (n, n), 1)
    return (col >= row) if rev else (col <= row)


def _expand4(cols, lane_head):
    return jnp.where(lane_head == 0, cols[0],
                     jnp.where(lane_head == 1, cols[1],
                               jnp.where(lane_head == 2, cols[2], cols[3])))


MOD_SPLIT = 4


def _mod_kernel(c_ref, *refs):
    w_refs = refs[:MOD_SPLIT]
    b_ref, o_ref = refs[MOD_SPLIT:]
    s = _silu(c_ref[...]).astype(BF16)
    wn = w_refs[0].shape[2]
    for q, w_ref in enumerate(w_refs):
        cols = slice(q * wn, (q + 1) * wn)
        part = _dot(s, w_ref[0].astype(BF16))

        @pl.when(pl.program_id(1) == 0)
        def _():
            o_ref[0, :, cols] = part + b_ref[0, :, cols]

        @pl.when(pl.program_id(1) > 0)
        def _():
            o_ref[0, :, cols] += part


def _mod_call(cc, ada_w, ada_b):
    depth, d, n = ada_w.shape
    tk = 256
    wn = n // MOD_SPLIT
    w_specs = [pl.BlockSpec((1, tk, wn), functools.partial(lambda l, k, q: (l, k, q), q=q))
               for q in range(MOD_SPLIT)]
    return pl.pallas_call(
        _mod_kernel,
        out_shape=jax.ShapeDtypeStruct((depth, 8, n), F32),
        grid=(depth, d // tk),
        in_specs=[pl.BlockSpec((8, tk), lambda l, k: (0, k))] + w_specs
                 + [pl.BlockSpec((1, 1, n), lambda l, k: (l, 0, 0))],
        out_specs=pl.BlockSpec((1, 8, n), lambda l, k: (l, 0, 0)),
        compiler_params=_cparams(("parallel", "arbitrary"), 40),
        name="adaln_mod",
    )(cc, *([ada_w] * MOD_SPLIT), ada_b.reshape(depth, 1, n))


def _stream_specs(srcs, tm, d, row_tile):
    if len(srcs) == 1:
        return [pl.BlockSpec((tm, d), lambda *g: (row_tile(*g), 0))], 0
    lead_tiles = srcs[0].shape[0] // tm
    return [pl.BlockSpec((tm, d), lambda *g: (jnp.minimum(row_tile(*g), lead_tiles - 1), 0),
                         pipeline_mode=pl.Buffered(1)),
            pl.BlockSpec((tm, d), lambda *g: (jnp.maximum(row_tile(*g) - lead_tiles, 0), 0))], lead_tiles


def _stream_rows(x_refs, lead_tiles, tile, rows):
    if len(x_refs) == 1:
        return x_refs[0][rows, :]
    return jnp.where(tile < lead_tiles, x_refs[0][rows, :], x_refs[1][rows, :])


def _ln_mod_to(h_ref, x_ref, mods_ref, nsub, tb, sh_row, sc_row):
    for r in range(nsub):
        x = x_ref[r * tb:(r + 1) * tb, :]
        h = _ln(x) * (1.0 + mods_ref[r, sc_row:sc_row + 1, :]) + mods_ref[r, sh_row:sh_row + 1, :]
        h_ref[r * tb:(r + 1) * tb, :] = h.astype(BF16)


def _in_kernel(*refs, n_src, lead_tiles, nsub, tb):
    x_refs = refs[:n_src]
    mods_ref, wm_ref, wg_ref, um_ref, ug_ref, h_ref = refs[n_src:]

    @pl.when(pl.program_id(1) == 0)
    def _():
        for r in range(nsub):
            rows = slice(r * tb, (r + 1) * tb)
            x = _stream_rows(x_refs, lead_tiles, pl.program_id(0), rows)
            h = _ln(x) * (1.0 + mods_ref[r, 1:2, :]) + mods_ref[r, 0:1, :]
            h_ref[rows, :] = h.astype(BF16)
        ug_ref[...] = _dot(h_ref[...], wg_ref[...])

    um_ref[...] = _dot(h_ref[...], wm_ref[...]).astype(BF16)


def _in_call(srcs, mods, wm, wg, tm, tb):
    t = sum(s.shape[0] for s in srcs)
    d = srcs[0].shape[1]
    n = wm.shape[1]
    tn = 1024
    nsub = tm // tb
    x_specs, lead_tiles = _stream_specs(srcs, tm, d, lambda i, j: i)
    return pl.pallas_call(
        functools.partial(_in_kernel, n_src=len(srcs), lead_tiles=lead_tiles, nsub=nsub, tb=tb),
        out_shape=(jax.ShapeDtypeStruct((t, n), BF16), jax.ShapeDtypeStruct((t, GATE_W), F32)),
        grid=(t // tm, n // tn),
        in_specs=x_specs + [pl.BlockSpec((nsub, 8, d), lambda i, j: (i, 0, 0)),
                            pl.BlockSpec((d, tn), lambda i, j: (0, j)),
                            pl.BlockSpec((d, GATE_W), lambda i, j: (0, 0))],
        out_specs=(pl.BlockSpec((tm, tn), lambda i, j: (i, j)),
                   pl.BlockSpec((tm, GATE_W), lambda i, j: (i, 0))),
        scratch_shapes=[pltpu.VMEM((tm, d), BF16)],
        compiler_params=_cparams(("parallel", "arbitrary"), 48 + 8 * (len(srcs) - 1)),
        name="ln_in_proj",
    )(*srcs, mods, wm, wg)


def _up_kernel(x_ref, mods_ref, wa_ref, wg_ref, o_ref, h_ref, *, nsub, tb):
    @pl.when(pl.program_id(1) == 0)
    def _():
        _ln_mod_to(h_ref, x_ref, mods_ref, nsub, tb, 3, 4)

    a = _dot(h_ref[...], wa_ref[...])
    g = _dot(h_ref[...], wg_ref[...])
    o_ref[...] = (_silu(a) * g).astype(BF16)


def _up_call(xs, mods, mods_off, w_up, layer, tm, tb):
    t, d = xs.shape
    dff = w_up.shape[2] // 2
    tn = 512
    nj = dff // tn
    nsub = tm // tb
    return pl.pallas_call(
        functools.partial(_up_kernel, nsub=nsub, tb=tb),
        out_shape=jax.ShapeDtypeStruct((t, dff), BF16),
        grid=(t // tm, nj),
        in_specs=[pl.BlockSpec((tm, d), lambda i, j: (i, 0)),
                  pl.BlockSpec((nsub, 8, d), lambda i, j: (i + mods_off, 0, 0)),
                  pl.BlockSpec((None, d, tn), lambda i, j: (layer, 0, j)),
                  pl.BlockSpec((None, d, tn), lambda i, j: (layer, 0, j + nj))],
        out_specs=pl.BlockSpec((tm, tn), lambda i, j: (i, j)),
        scratch_shapes=[pltpu.VMEM((tm, d), BF16)],
        compiler_params=_cparams(("parallel", "arbitrary"), 48),
        name="ln_ffn_up",
    )(xs, mods, w_up, w_up)


def _res_kernel(a_ref, w_ref, x_ref, mods_ref, pg_ref, pb_ref, o_ref, *, nsub, tb, g_row, alpha):
    y = _dot(a_ref[...], w_ref[...])
    for r in range(nsub):
        rows = slice(r * tb, (r + 1) * tb)
        z = alpha * x_ref[rows, :] + mods_ref[r, g_row:g_row + 1, :] * y[rows, :]
        o_ref[rows, :] = _ln(z) * pg_ref[...] + pb_ref[...]


def _res_call(act, w, layer, xs, mods, mods_off, pg, pb, tm, tb, g_row, alpha, name):
    t, d = xs.shape
    ka = act.shape[1]
    nsub = tm // tb
    return pl.pallas_call(
        functools.partial(_res_kernel, nsub=nsub, tb=tb, g_row=g_row, alpha=alpha),
        out_shape=jax.ShapeDtypeStruct((t, d), F32),
        grid=(t // tm,),
        in_specs=[pl.BlockSpec((tm, ka), lambda i: (i, 0)),
                  pl.BlockSpec((None, ka, d), lambda i: (layer, 0, 0), pipeline_mode=pl.Buffered(1)),
                  pl.BlockSpec((tm, d), lambda i: (i, 0)),
                  pl.BlockSpec((nsub, 8, d), lambda i: (i + mods_off, 0, 0)),
                  pl.BlockSpec((1, d), lambda i: (0, 0)),
                  pl.BlockSpec((1, d), lambda i: (0, 0))],
        out_specs=pl.BlockSpec((tm, d), lambda i: (i, 0)),
        compiler_params=_cparams(("parallel",), 52),
        name=name,
    )(act, w, xs, mods, pg, pb)


def _head_norm(y, center):
    if center:
        y = y - jnp.mean(y, axis=-1, keepdims=True)
    return y * lax.rsqrt(jnp.mean(y * y, axis=-1, keepdims=True) + EPS)


def _mix_finalize(act_ref, rows, ys_ref, mo_ref, sz_ref, rg_ref, gg_ref, nrm_ref):
    gw = 4 * LANES
    for h in range(M_HEADS):
        hc = slice(h * LANES, (h + 1) * LANES)
        yn = _head_norm(ys_ref[rows, hc].astype(F32), True)
        act_ref[rows, hc] = (_sigmoid(mo_ref[rows, hc].astype(F32)) * (yn * nrm_ref[0:1, hc])).astype(BF16)
    halves = [slice(g * (gw // S_G), (g + 1) * (gw // S_G)) for g in range(S_G)]
    ys = [ys_ref[rows, gw + gc.start:gw + gc.stop].astype(F32) * _silu(sz_ref[rows, gc].astype(F32)) for gc in halves]
    inv = lax.rsqrt(sum(jnp.sum(y * y, axis=-1, keepdims=True) for y in ys) * (1.0 / gw) + EPS)
    for y, gc in zip(ys, halves):
        act_ref[rows, gw + gc.start:gw + gc.stop] = (y * inv * nrm_ref[1:2, gc]).astype(BF16)
    for h in range(R_HEADS):
        hc = slice(h * LANES, (h + 1) * LANES)
        yn = _head_norm(ys_ref[rows, 2 * gw + h * LANES:2 * gw + (h + 1) * LANES].astype(F32), True)
        act_ref[rows, 2 * gw + h * LANES:2 * gw + (h + 1) * LANES] = (
            yn * nrm_ref[2:3, hc] * _silu(rg_ref[rows, hc].astype(F32))).astype(BF16)
    for h in range(G_HEADS):
        hc = slice(h * LANES, (h + 1) * LANES)
        yn = _head_norm(ys_ref[rows, 3 * gw + h * LANES:3 * gw + (h + 1) * LANES].astype(F32), False)
        act_ref[rows, 3 * gw + h * LANES:3 * gw + (h + 1) * LANES] = (
            yn * nrm_ref[3:4, hc] * _silu(gg_ref[rows, hc].astype(F32))).astype(BF16)


def _out_kernel(*refs, n_src, lead_tiles, skip, nsub, tb, alpha):
    ys_ref, mo_ref, sz_ref, rg_ref, gg_ref, nrm_ref, w_ref = refs[:7]
    x_refs = refs[7:7 + n_src]
    mods_ref, pg_ref, pb_ref, o_ref, act_ref = refs[7 + n_src:]
    for r in range(nsub):
        rows = slice(r * tb, (r + 1) * tb)
        _mix_finalize(act_ref, rows, ys_ref, mo_ref, sz_ref, rg_ref, gg_ref, nrm_ref)
        y = _dot(act_ref[rows, :], w_ref[...])
        x = _stream_rows(x_refs, lead_tiles, pl.program_id(0) + skip, rows)
        z = alpha * x + mods_ref[r, 2:3, :] * y
        o_ref[rows, :] = _ln(z) * pg_ref[...] + pb_ref[...]


def _out_call(ysum, um, nrm, w, layer, srcs, mods, skip, pg, pb, tm, tb, alpha):
    t = sum(s.shape[0] for s in srcs)
    d = srcs[0].shape[1]
    gw = d // N_GROUPS
    nsub = tm // tb
    x_specs, lead_tiles = _stream_specs(srcs, tm, d, lambda i: i + skip)

    def gate(cb):
        return pl.BlockSpec((tm, gw), lambda i: (i + skip, cb))

    return pl.pallas_call(
        functools.partial(_out_kernel, n_src=len(srcs), lead_tiles=lead_tiles, skip=skip, nsub=nsub, tb=tb,
                          alpha=alpha),
        out_shape=jax.ShapeDtypeStruct((t - skip * tm, d), F32),
        grid=(t // tm - skip,),
        in_specs=[pl.BlockSpec((tm, d), lambda i: (i + skip, 0)), gate(3), gate(10), gate(9), gate(13),
                  pl.BlockSpec(nrm.shape, lambda i: (0, 0)),
                  pl.BlockSpec((None, d, d), lambda i: (layer, 0, 0), pipeline_mode=pl.Buffered(1))]
                 + x_specs
                 + [pl.BlockSpec((nsub, 8, d), lambda i: (i + skip, 0, 0)),
                    pl.BlockSpec((1, d), lambda i: (0, 0)),
                    pl.BlockSpec((1, d), lambda i: (0, 0))],
        out_specs=pl.BlockSpec((tm, d), lambda i: (i, 0)),
        scratch_shapes=[pltpu.VMEM((tm, d), BF16)],
        compiler_params=_cparams(("parallel",), 52),
        name="mix_out_proj_res_ln",
    )(ysum, um, um, um, um, nrm, w, *srcs, mods, pg, pb)


def _prep_kernel(cur_ref, prev_ref, next_ref, rqk_ref, cw_ref, cb_ref, rope_ref, o_ref, *, nctx, nblk, tb):
    t = pl.program_id(1)
    seg_start = jnp.logical_or(t == 0, t == nctx)
    seg_end = jnp.logical_or(t == nctx - 1, t == nblk - 1)
    nc = cur_ref.shape[1]
    ridx = lax.broadcasted_iota(jnp.int32, (tb, 1), 0)
    for blk in range(nc // LANES):
        cs = slice(blk * LANES, (blk + 1) * LANES)
        x = cur_ref[:, cs].astype(F32)
        prev_row = jnp.where(seg_start, 0.0, prev_ref[:, cs].astype(F32)[15:16, :])
        next_row = jnp.where(seg_end, 0.0, next_ref[:, cs].astype(F32)[0:1, :])
        xp = jnp.where(ridx == 0, prev_row, pltpu.roll(x, 1, 0))
        xn = jnp.where(ridx == tb - 1, next_row, pltpu.roll(x, tb - 1, 0))
        y = cb_ref[:, cs] + xp * cw_ref[0:1, cs] + x * cw_ref[1:2, cs] + xn * cw_ref[2:3, cs]
        o_ref[:, cs] = _silu(y).astype(BF16)

    cos = rope_ref[:, 0:LANES]
    sin = rope_ref[:, LANES:2 * LANES]
    lane = lax.broadcasted_iota(jnp.int32, (1, LANES), 1)
    first = (lane % 64) < 32
    for blk in range(rqk_ref.shape[1] // LANES):
        xh = rqk_ref[:, blk * LANES:(blk + 1) * LANES].astype(F32)
        partner = jnp.where(first, pltpu.roll(xh, 96, 1), pltpu.roll(xh, 32, 1))
        o_ref[:, nc + blk * LANES:nc + (blk + 1) * LANES] = (xh * cos + partner * sin).astype(BF16)


def _flat_block(b, nctx, nblk):
    nlat = nblk - nctx
    return lambda i, t: jnp.where(t < nctx, i * nctx + t, b * nctx + i * nlat + (t - nctx))


def _prep_call(um, conv_w, conv_b, rope, b, nctx, tb):
    t_rows = um.shape[0]
    nblk = t_rows // (b * tb)
    hb = tb // 16
    nc = conv_w.shape[1]
    fb = _flat_block(b, nctx, nblk)
    return pl.pallas_call(
        functools.partial(_prep_kernel, nctx=nctx, nblk=nblk, tb=tb),
        out_shape=jax.ShapeDtypeStruct((t_rows, 2 * nc), BF16),
        grid=(b, nblk),
        in_specs=[pl.BlockSpec((tb, nc), lambda i, t: (fb(i, t), 2)),
                  pl.BlockSpec((16, nc), lambda i, t: (jnp.maximum(fb(i, t) * hb - 1, 0), 2)),
                  pl.BlockSpec((16, nc), lambda i, t: (jnp.minimum((fb(i, t) + 1) * hb, t_rows // 16 - 1), 2)),
                  pl.BlockSpec((tb, nc), lambda i, t: (fb(i, t), 3)),
                  pl.BlockSpec((8, nc), lambda i, t: (0, 0)),
                  pl.BlockSpec((1, nc), lambda i, t: (0, 0)),
                  pl.BlockSpec((tb, 2 * LANES), lambda i, t: (t, 0))],
        out_specs=pl.BlockSpec((tb, 2 * nc), lambda i, t: (fb(i, t), 0)),
        compiler_params=_cparams(("parallel", "parallel"), 32),
        name="prep_conv_rope",
    )(um, um, um, um, conv_w, conv_b, rope)


def _mlstm_a(rev, L, gate_ref, bias_ref):
    lane = lax.broadcasted_iota(jnp.int32, (1, LANES), 1)
    is_f = jnp.logical_and(lane >= 8, lane < 16)
    g = gate_ref[...] + bias_ref[0:1, :]
    gp = jnp.where(is_f, _log_sigmoid(g), g)
    return gp, _tri_cumsum(_causal(L, rev).astype(BF16), gp)


def _mlstm_b(qkvo_ref, caug_ref):
    gw = 4 * LANES
    out = []
    for h in range(M_HEADS):
        q = qkvo_ref[:, h * LANES:(h + 1) * LANES]
        k = qkvo_ref[:, gw + h * LANES:gw + (h + 1) * LANES]
        out.append((_dot_nt(q, k), _dot(q, caug_ref[h].astype(BF16))))
    return out


def _mlstm_c(rev, final, L, pa, pb, qkvo_ref, caug_ref, m_ref, o_ref, oc, yb_ref):
    d = 1 if rev else 0
    gw = 4 * LANES
    gp, cum = pa
    mask = _causal(L, rev)
    e_idx = 0 if rev else L - 1
    ones = jnp.ones((L, LANES), BF16)
    gp_t = gp.T
    cum_t = cum.T
    for h in range(M_HEADS):
        ci = 4 * d + h
        cf = 8 + 4 * d + h
        b_col = cum[:, cf:cf + 1]
        b_row = cum_t[cf:cf + 1, :]
        ig_col = gp[:, ci:ci + 1]
        ig_row = gp_t[ci:ci + 1, :]
        m_st = m_ref[h, 0:1, 0:1]
        log_intra = jnp.where(mask, b_col - b_row + ig_row, NEG)
        log_inter = b_col + m_st
        m_i = jnp.maximum(log_inter, jnp.max(log_intra, axis=1, keepdims=True))
        w = jnp.exp(log_intra - m_i)
        w_inter = jnp.exp(log_inter - m_i)
        b_end = cum[e_idx:e_idx + 1, cf:cf + 1]
        ls_row = b_end - b_row + ig_row
        m_new = jnp.maximum(b_end + m_st, jnp.max(ls_row, axis=1, keepdims=True))
        ws_col = jnp.exp(b_end - b_col + ig_col - m_new)
        decay = jnp.exp(b_end + m_st - m_new)
        k = qkvo_ref[:, gw + h * LANES:gw + (h + 1) * LANES]
        v = qkvo_ref[:, 2 * gw + h * LANES:2 * gw + (h + 1) * LANES]
        vaug = jnp.concatenate([v, ones], axis=1)
        qk, qc = pb[h]
        s = (qk * w).astype(BF16)
        res = _dot(s, vaug) + w_inter * qc
        kws = (k.astype(F32) * ws_col).astype(BF16)
        caug_ref[h] = decay * caug_ref[h] + _dot_tn(kws, vaug)
        m_ref[h] = jnp.broadcast_to(m_new, (8, LANES))
        hh = res[:, 0:LANES] / jnp.maximum(jnp.abs(res[:, LANES:2 * LANES]), jnp.exp(-m_i))
        cols = slice(oc + h * LANES, oc + (h + 1) * LANES)
        if final:
            o_ref[:, cols] = (hh + yb_ref[:, cols]).astype(BF16)
        else:
            o_ref[:, cols] = hh


def _ssd_a(rev, L, gate_ref, prm_ref):
    dt_all = _softplus(gate_ref[...] + prm_ref[0:1, :])
    return dt_all, _tri_cumsum(_causal(L, rev).astype(BF16), dt_all * (-jnp.exp(prm_ref[1:2, :])))


def _ssd_b(xbc_ref, st_ref):
    gw = 4 * LANES
    out = []
    for g in range(S_G):
        bg = xbc_ref[:, gw + g * S_N:gw + (g + 1) * S_N]
        cg = xbc_ref[:, gw + S_G * S_N + g * S_N:gw + S_G * S_N + (g + 1) * S_N]
        out.append((_dot_nt(cg, bg), _dot(cg, st_ref[g].astype(BF16))))
    return out


def _ssd_c(rev, final, L, pa, pb, xbc_ref, st_ref, o_ref, oc, yb_ref, dsk_ref):
    d = 1 if rev else 0
    gw = 4 * LANES
    hpg = S_HEADS // S_G
    gcols = gw // S_G
    dt_all, cum = pa
    mask = _causal(L, rev)
    e_idx = 0 if rev else L - 1
    lane_head = lax.broadcasted_iota(jnp.int32, (1, gcols), 1) // (gcols // hpg)
    cum_t = cum.T
    for g in range(S_G):
        wts, e_cols, dec_cols, dt_cols, decays = [], [], [], [], []
        for hl in range(hpg):
            c = 16 + S_HEADS * d + g * hpg + hl
            b_col = cum[:, c:c + 1]
            b_row = cum_t[c:c + 1, :]
            b_end = cum[e_idx:e_idx + 1, c:c + 1]
            wts.append(jnp.exp(jnp.where(mask, b_col - b_row, NEG)))
            e_cols.append(jnp.exp(b_col))
            dec_cols.append(jnp.exp(b_end - b_col))
            dt_cols.append(dt_all[:, c:c + 1])
            decays.append(jnp.exp(b_end))
        gc = slice(g * gcols, (g + 1) * gcols)
        xs_g = xbc_ref[:, gc].astype(F32)
        bg = xbc_ref[:, gw + g * S_N:gw + (g + 1) * S_N]
        xdt = xs_g * _expand4(dt_cols, lane_head)
        gm, inter = pb[g]
        acc = _expand4(e_cols, lane_head) * inter
        for hl in range(hpg):
            xh = jnp.where(lane_head == hl, xdt, 0.0).astype(BF16)
            acc = acc + _dot((gm * wts[hl]).astype(BF16), xh)
        xdec = (xdt * _expand4(dec_cols, lane_head)).astype(BF16)
        st_ref[g] = _expand4(decays, lane_head) * st_ref[g] + _dot_tn(bg, xdec)
        cols = slice(oc + g * gcols, oc + (g + 1) * gcols)
        if final:
            o_ref[:, cols] = (acc + yb_ref[:, cols] + dsk_ref[0:1, gc] * xs_g).astype(BF16)
        else:
            o_ref[:, cols] = acc


def _ret_b(qk_ref, qc, st_ref):
    gw = 4 * LANES
    out = []
    for h in range(R_HEADS):
        q = qk_ref[:, qc + h * LANES:qc + (h + 1) * LANES]
        k = qk_ref[:, qc + gw + h * LANES:qc + gw + (h + 1) * LANES]
        out.append((_dot_nt(q, k), _dot(q, st_ref[h].astype(BF16))))
    return out


def _ret_c(rev, final, L, pb, qk_ref, qc, v_ref, prm_ref, st_ref, o_ref, oc, yb_ref):
    d = 1 if rev else 0
    gw = 4 * LANES
    mask = _causal(L, rev)
    icol = lax.broadcasted_iota(jnp.int32, (L, 1), 0).astype(F32)
    row = lax.broadcasted_iota(jnp.int32, (L, L), 0)
    col = lax.broadcasted_iota(jnp.int32, (L, L), 1)
    dist = ((col - row) if rev else (row - col)).astype(F32)
    lg_all = -jnp.exp(prm_ref[d:d + 1, :])
    for h in range(R_HEADS):
        lg = lg_all[:, h:h + 1]
        w = jnp.exp(jnp.where(mask, dist * lg, NEG))
        if rev:
            e_col = jnp.exp((float(L) - icol) * lg)
            dec_col = jnp.exp(icol * lg)
        else:
            e_col = jnp.exp((icol + 1.0) * lg)
            dec_col = jnp.exp((float(L - 1) - icol) * lg)
        decay = jnp.exp(float(L) * lg)
        hc = slice(h * LANES, (h + 1) * LANES)
        k = qk_ref[:, qc + gw + h * LANES:qc + gw + (h + 1) * LANES]
        v = v_ref[:, hc]
        qk, qst = pb[h]
        y = _dot((qk * w).astype(BF16), v) + e_col * qst
        kd = (k.astype(F32) * dec_col).astype(BF16)
        st_ref[h] = decay * st_ref[h] + _dot_tn(kd, v)
        cols = slice(oc + h * LANES, oc + (h + 1) * LANES)
        if final:
            o_ref[:, cols] = (y + yb_ref[:, cols]).astype(BF16)
        else:
            o_ref[:, cols] = y


def _gla_a(rev, L, tb, gate_ref, gw_ref, gb_ref):
    row = lax.broadcasted_iota(jnp.int32, (tb, tb), 0)
    col = lax.broadcasted_iota(jnp.int32, (tb, tb), 1)
    blockmask = jnp.logical_and((row // L) == (col // L), (col >= row) if rev else (col <= row))
    z = _dot_f32(gate_ref[...], gw_ref[...]) + gb_ref[...]
    return blockmask, _tri_cumsum(blockmask.astype(BF16), _log_sigmoid(z) * (1.0 / G_TAU))


def _gla_c(rev, final, L, nsub, pa, qk_ref, v_ref, st_ref, o_ref, oc, yb_ref):
    kw = 2 * LANES
    blockmask, cum = pa
    e_idx = 0 if rev else L - 1
    lane_head = lax.broadcasted_iota(jnp.int32, (1, kw), 1) // (kw // G_HEADS)
    sub = [slice(c * L, (c + 1) * L) for c in range(nsub)]
    qf = qk_ref[:, 0:kw].astype(F32)
    kf = qk_ref[:, kw:2 * kw].astype(F32)
    qg = qf * jnp.exp(cum)
    kg = (kf * jnp.exp(-cum)).astype(BF16)
    b_ends = [cum[c * L + e_idx:c * L + e_idx + 1, :] for c in range(nsub)]
    kd = jnp.concatenate([kf[sub[c], :] * jnp.exp(b_ends[c] - cum[sub[c], :]) for c in range(nsub)], axis=0)
    qgh = [jnp.where(lane_head == h, qg, 0.0).astype(BF16) for h in range(G_HEADS)]
    kdh = [jnp.where(lane_head == h, kd, 0.0).astype(BF16) for h in range(G_HEADS)]
    s_all = [_dot_nt(qgh[h], kg) for h in range(G_HEADS)]
    upd = [sum(_dot_tn(v_ref[sub[c], h * LANES:(h + 1) * LANES], kdh[h][sub[c], :]) for h in range(G_HEADS))
           for c in range(nsub)]
    st = st_ref[...]
    st_at = [None] * nsub
    for c in (reversed(range(nsub)) if rev else range(nsub)):
        st_at[c] = st.astype(BF16)
        st = jnp.exp(b_ends[c]) * st + upd[c]
    st_ref[...] = st
    for h in range(G_HEADS):
        hc = slice(h * LANES, (h + 1) * LANES)
        cols = slice(oc + h * LANES, oc + (h + 1) * LANES)
        inter = jnp.concatenate([_dot_nt(qgh[h][sub[c], :], st_at[c]) for c in range(nsub)], axis=0)
        y = _dot(jnp.where(blockmask, s_all[h], 0.0).astype(BF16), v_ref[:, hc]) + inter
        if final:
            o_ref[:, cols] = (y + yb_ref[:, cols]).astype(BF16)
        else:
            o_ref[:, cols] = y


def _scan_kernel(*refs, rev, final, tb):
    (um_m, ug, prep, r_v, g_qk, g_v, m_bias, s_prm, r_prm, gwp, gb) = refs[:11]
    if final:
        yb, dsk = refs[11:13]
        o_ref = refs[13]
        scratch = refs[14:]
    else:
        yb = dsk = None
        o_ref = refs[11]
        scratch = refs[12:]
    caug, m_st, s_st, r_st, g_st = scratch
    gw = 4 * LANES

    @pl.when(pl.program_id(1) == 0)
    def _():
        for ref in scratch:
            ref[...] = jnp.zeros_like(ref)

    m_a = _mlstm_a(rev, tb, ug, m_bias)
    s_a = _ssd_a(rev, tb, ug, s_prm)
    g_a = _gla_a(rev, GLA_CHUNK, tb, ug, gwp, gb)
    r_b = _ret_b(prep, 2 * gw, r_st)
    m_b = _mlstm_b(um_m, caug)
    s_b = _ssd_b(prep, s_st)
    _ret_c(rev, final, tb, r_b, prep, 2 * gw, r_v, r_prm, r_st, o_ref, 2 * gw, yb)
    _gla_c(rev, final, GLA_CHUNK, tb // GLA_CHUNK, g_a, g_qk, g_v, g_st, o_ref, 3 * gw, yb)
    _mlstm_c(rev, final, tb, m_a, m_b, um_m, caug, m_st, o_ref, 0, yb)
    _ssd_c(rev, final, tb, s_a, s_b, prep, s_st, o_ref, gw, yb, dsk)


def _scan_call(rev, final, b, tb, nctx, um, ug, prep, params, yb, dsk):
    t_rows = um.shape[0]
    gw = 4 * LANES
    nblk = t_rows // (b * tb)
    fb = _flat_block(b, nctx, nblk)
    if rev:
        def order(c):
            return jnp.where(c < nctx, nctx - 1 - c, nblk + nctx - 1 - c)
    else:
        def order(c):
            return c

    def tok(width, cb):
        return pl.BlockSpec((tb, width), lambda i, c: (fb(i, order(c)), cb))

    def full(arr):
        return pl.BlockSpec(arr.shape, lambda i, c: (0,) * arr.ndim)

    ins = [um, ug, prep, um, um, um] + list(params)
    in_specs = [tok(4 * gw, 0), tok(GATE_W, 0), tok(4 * gw, 0), tok(gw, 8), tok(gw, 11), tok(gw, 12)]
    in_specs += [full(p) for p in params]
    if final:
        ins += [yb, dsk]
        in_specs += [tok(4 * gw, 0), full(dsk)]
    scratch = [pltpu.VMEM((M_HEADS, LANES, 2 * LANES), F32), pltpu.VMEM((M_HEADS, 8, LANES), F32),
               pltpu.VMEM((S_G, S_N, gw // S_G), F32), pltpu.VMEM((R_HEADS, LANES, LANES), F32),
               pltpu.VMEM((LANES, 2 * LANES), F32)]
    return pl.pallas_call(
        functools.partial(_scan_kernel, rev=rev, final=final, tb=tb),
        out_shape=jax.ShapeDtypeStruct((t_rows, 4 * gw), BF16 if final else F32),
        grid=(b, nblk),
        in_specs=in_specs,
        out_specs=tok(4 * gw, 0),
        scratch_shapes=scratch,
        compiler_params=_cparams(("parallel", "arbitrary"), 48),
        name="scan_fwd" if final else "scan_bwd",
    )(*ins)


def _in_col_layout(d_model):
    gw = d_model // N_GROUPS
    conv_ch = gw + 2 * S_G * S_N
    names = [('m_q', gw), ('m_k', gw), ('m_v', gw), ('m_o', gw), ('m_i', 2 * M_HEADS), ('m_f', 2 * M_HEADS),
             ('s_z', gw), ('s_xbc', conv_ch), ('s_dt', 2 * S_HEADS),
             ('r_q', gw), ('r_k', gw), ('r_v', gw), ('r_g', gw),
             ('g_q', gw // 2), ('g_k', gw // 2), ('g_v', gw), ('g_g', gw), ('g_a', 2 * G_RANK)]
    off, o = {}, 0
    for nm, n in names:
        off[nm] = (o, n)
        o += n
    return off


def kernel(x, c, ctx, c_ctx, ada_w, ada_b, w_in, m_ig_b, m_fg_b, m_norm, s_conv_w, s_conv_b, s_dt_bias, s_a_log, s_d, s_norm, r_decay, r_norm, g_gate_w, g_gate_b, g_norm, w_out, post_g, post_b, ffn_w_up, ffn_w_down):
    b, seq, d = x.shape
    n_ctx_tok = ctx.shape[1]
    depth = ada_w.shape[0]
    gw = d // N_GROUPS
    assert gw == 4 * LANES and b + 1 <= 8
    s_tot = n_ctx_tok + seq
    t = b * s_tot
    tb = math.gcd(MAX_TOKEN_BLOCK, math.gcd(n_ctx_tok, seq))
    assert tb % 16 == 0 and tb % GLA_CHUNK == 0
    nctx = n_ctx_tok // tb
    nblk = s_tot // tb
    nsb = t // tb
    t_ctx = b * n_ctx_tok
    t_lat = b * seq

    def row_tile(cap):
        return max(m for m in (1024, 512, 256, 128, 64, 32, 16)
                   if m <= cap and t_ctx % m == 0 and t_lat % m == 0 and m % tb == 0)

    tm = row_tile(1024)
    tm_out = row_tile(512)
    alpha = (2.0 * depth) ** 0.25

    off = _in_col_layout(d)
    main_order = ['m_q', 'm_k', 'm_v', 'm_o', 's_xbc', 'r_q', 'r_k', 'r_v', 'r_g', 's_z', 'g_q', 'g_k', 'g_v', 'g_g']
    gate_order = ['m_i', 'm_f', 's_dt', 'g_a']
    col_scale = {'m_q': float(LANES) ** -0.5, 'r_k': float(LANES) ** -0.5, 'g_q': float(gw // 2 // G_HEADS) ** -0.5}
    n_gate = sum(off[n][1] for n in gate_order)

    def cols_of(wl, name):
        o, n = off[name]
        w = wl[:, o:o + n]
        return w * col_scale[name] if name in col_scale else w

    pos = jnp.arange(seq)
    quarter = LANES // 4
    freqs = 1.0 / (ROPE_BASE ** (jnp.arange(quarter, dtype=F32) / quarter))
    ang_r = (pos // GRID_W).astype(F32)[:, None] * freqs[None, :]
    ang_c = (pos % GRID_W).astype(F32)[:, None] * freqs[None, :]
    cos_t = jnp.concatenate([jnp.cos(ang_r)] * 2 + [jnp.cos(ang_c)] * 2, axis=1)
    sin_t = jnp.concatenate([-jnp.sin(ang_r), jnp.sin(ang_r), -jnp.sin(ang_c), jnp.sin(ang_c)], axis=1)
    rope = jnp.concatenate([
        jnp.concatenate([jnp.ones((n_ctx_tok, LANES), F32), jnp.zeros((n_ctx_tok, LANES), F32)], axis=1),
        jnp.concatenate([cos_t, sin_t], axis=1)], axis=0)

    xs = (ctx.reshape(t_ctx, d), x.reshape(t_lat, d))
    cc =jnp.concatenate([c, c_ctx[None, :], jnp.zeros((8 - b - 1, d), F32)], axis=0)
    mod = _mod_call(cc, ada_w, ada_b)
    sb = np.arange(nsb)
    mod_row = np.where(sb < b * nctx, b, (sb - b * nctx) // (nblk - nctx))

    def lane_rows(vecs, start, n_rows):
        rows = jnp.stack(vecs)
        return jnp.pad(rows, ((0, n_rows - rows.shape[0]), (start, LANES - start - rows.shape[1])))

    w_out_b = w_out.astype(BF16)
    w_up_b = ffn_w_up.astype(BF16)
    w_down_b = ffn_w_down.astype(BF16)
    for l in range(depth):
        mods = jnp.pad(mod[l][mod_row].reshape(nsb, 6, d), ((0, 0), (0, 2), (0, 0)))
        wl = w_in[l]
        wm = jnp.concatenate([cols_of(wl, n) for n in main_order], axis=1).astype(BF16)
        wg = jnp.concatenate([cols_of(wl, n) for n in gate_order] + [jnp.zeros((d, GATE_W - n_gate), F32)],
                             axis=1).astype(BF16)
        um, ug = _in_call(xs, mods, wm, wg, tm, tb)

        conv_w = jnp.pad(s_conv_w[l], ((0, 8 - CONV_W), (0, 0)))
        prep = _prep_call(um, conv_w, s_conv_b[l][None, :], rope, b, nctx, tb)

        m_bias = lane_rows([jnp.concatenate([m_ig_b[l].reshape(-1), m_fg_b[l].reshape(-1)])], 0, 1)
        s_prm = lane_rows([s_dt_bias[l].reshape(-1), s_a_log[l].reshape(-1)], 16, 8)
        r_prm = lane_rows([r_decay[l][0], r_decay[l][1]], 0, 8)
        nrm = jnp.concatenate([jnp.stack([m_norm[l], s_norm[l], r_norm[l], g_norm[l]]), jnp.zeros((4, gw), F32)])
        dsk = jnp.repeat(s_d[l], gw // S_HEADS)[None, :]

        def dir_params(dd):
            lo = 32 + G_RANK * dd
            gwp = jnp.pad(g_gate_w[l][dd], ((lo, GATE_W - lo - G_RANK), (0, 0)))
            return [m_bias, s_prm, r_prm, gwp, g_gate_b[l][dd][None, :]]

        yb = _scan_call(True, False, b, tb, nctx, um, ug, prep, dir_params(1), None, None)
        ysum = _scan_call(False, True, b, tb, nctx, um, ug, prep, dir_params(0), yb, dsk)

        skip = t_ctx if l == depth - 1 else 0
        x1 = _out_call(ysum, um, nrm, w_out_b, l, xs, mods, skip // tm_out,
                       post_g[l, 0][None, :], post_b[l, 0][None, :], tm_out, tb, alpha)
        act = _up_call(x1, mods, skip // tm, w_up_b, l, tm, tb)
        xs = (_res_call(act, w_down_b, l, x1, mods, skip // tb,
                        post_g[l, 1][None, :], post_b[l, 1][None, :], tb, tb, 5, alpha, "ffn_down_res_ln"),)

    return xs[0].reshape(b, seq, d)
```

```python
import functools
import math

import numpy as np
import jax
import jax.numpy as jnp
from jax import lax
from jax.experimental import pallas as pl
from jax.experimental.pallas import tpu as pltpu

F32 = jnp.float32
BF16 = jnp.bfloat16

EPS = 1e-5
ROPE_BASE = 10000.0
GRID_W = 64
N_GROUPS = 4
M_HEADS = 4
S_HEADS = 8
S_G = 2
S_N = 128
CONV_W = 3
R_HEADS = 4
G_HEADS = 4
G_RANK = 16
G_TAU = 16.0
NEG = -1e30

LANES = 128
GATE_W = LANES
MAX_TOKEN_BLOCK = 256
GLA_CHUNK = 64


def _cparams(sem, vmem_mb):
    return pltpu.CompilerParams(dimension_semantics=sem, vmem_limit_bytes=vmem_mb * 1024 * 1024)


def _dot(a, b):
    return jnp.dot(a, b, preferred_element_type=F32)


def _dot_nt(a, b):
    return lax.dot_general(a, b, (((1,), (1,)), ((), ())), preferred_element_type=F32)


def _dot_tn(a, b):
    return lax.dot_general(a, b, (((0,), (0,)), ((), ())), preferred_element_type=F32)


def _split3(x):
    hi = x.astype(BF16)
    r1 = x - hi.astype(F32)
    mid = r1.astype(BF16)
    lo = (r1 - mid.astype(F32)).astype(BF16)
    return hi, mid, lo


def _tri_cumsum(tri, x):
    hi, mid, lo = _split3(x)
    return _dot(tri, hi) + _dot(tri, mid) + _dot(tri, lo)


def _dot_f32(a, b):
    ah = a.astype(BF16)
    al = (a - ah.astype(F32)).astype(BF16)
    bh = b.astype(BF16)
    bl = (b - bh.astype(F32)).astype(BF16)
    return _dot(ah, bh) + _dot(ah, bl) + _dot(al, bh)


def _sigmoid(x):
    return 1.0 / (1.0 + jnp.exp(-x))


def _silu(x):
    return x * _sigmoid(x)


def _softplus(x):
    return jnp.maximum(x, 0.0) + jnp.log1p(jnp.exp(-jnp.abs(x)))


def _log_sigmoid(x):
    return jnp.minimum(x, 0.0) - jnp.log1p(jnp.exp(-jnp.abs(x)))


def _ln(x):
    mu = jnp.mean(x, axis=-1, keepdims=True)
    xc = x - mu
    var = jnp.mean(xc * xc, axis=-1, keepdims=True)
    return xc * lax.rsqrt(var + EPS)


def _causal(n, rev):
    row = lax.broadcasted_iota(jnp.int32, (n, n), 0)
    col = lax.broadcasted_iota(jnp.int32, (n, n), 1)
    return (col >= row) if rev else (col <= row)


def _expand4(cols, lane_head):
    return jnp.where(lane_head == 0, cols[0],
                     jnp.where(lane_head == 1, cols[1],
                               jnp.where(lane_head == 2, cols[2], cols[3])))


MOD_SPLIT = 4


def _mod_kernel(c_ref, *refs):
    w_refs = refs[:MOD_SPLIT]
    b_ref, o_ref = refs[MOD_SPLIT:]
    s = _silu(c_ref[...]).astype(BF16)
    wn = w_refs[0].shape[2]
    for q, w_ref in enumerate(w_refs):
        cols = slice(q * wn, (q + 1) * wn)
        part = _dot(s, w_ref[0].astype(BF16))

        @pl.when(pl.program_id(1) == 0)
        def _():
            o_ref[0, :, cols] = part + b_ref[0, :, cols]

        @pl.when(pl.program_id(1) > 0)
        def _():
            o_ref[0, :, cols] += part


def _mod_call(cc, ada_w, ada_b):
    depth, d, n = ada_w.shape
    tk = 256
    wn = n // MOD_SPLIT
    w_specs = [pl.BlockSpec((1, tk, wn), functools.partial(lambda l, k, q: (l, k, q), q=q))
               for q in range(MOD_SPLIT)]
    return pl.pallas_call(
        _mod_kernel,
        out_shape=jax.ShapeDtypeStruct((depth, 8, n), F32),
        grid=(depth, d // tk),
        in_specs=[pl.BlockSpec((8, tk), lambda l, k: (0, k))] + w_specs
                 + [pl.BlockSpec((1, 1, n), lambda l, k: (l, 0, 0))],
        out_specs=pl.BlockSpec((1, 8, n), lambda l, k: (l, 0, 0)),
        compiler_params=_cparams(("parallel", "arbitrary"), 40),
        name="adaln_mod",
    )(cc, *([ada_w] * MOD_SPLIT), ada_b.reshape(depth, 1, n))


def _stream_specs(srcs, tm, d, row_tile):
    if len(srcs) == 1:
        return [pl.BlockSpec((tm, d), lambda *g: (row_tile(*g), 0))], 0
    lead_tiles = srcs[0].shape[0] // tm
    return [pl.BlockSpec((tm, d), lambda *g: (jnp.minimum(row_tile(*g), lead_tiles - 1), 0),
                         pipeline_mode=pl.Buffered(1)),
            pl.BlockSpec((tm, d), lambda *g: (jnp.maximum(row_tile(*g) - lead_tiles, 0), 0))], lead_tiles


def _stream_rows(x_refs, lead_tiles, tile, rows):
    if len(x_refs) == 1:
        return x_refs[0][rows, :]
    return jnp.where(tile < lead_tiles, x_refs[0][rows, :], x_refs[1][rows, :])


def _ln_mod_to(h_ref, x_ref, mods_ref, nsub, tb, sh_row, sc_row):
    for r in range(nsub):
        x = x_ref[r * tb:(r + 1) * tb, :]
        h = _ln(x) * (1.0 + mods_ref[r, sc_row:sc_row + 1, :]) + mods_ref[r, sh_row:sh_row + 1, :]
        h_ref[r * tb:(r + 1) * tb, :] = h.astype(BF16)


def _in_kernel(*refs, n_src, lead_tiles, nsub, tb):
    x_refs = refs[:n_src]
    mods_ref, wm_ref, wg_ref, um_ref, ug_ref, h_ref = refs[n_src:]

    @pl.when(pl.program_id(1) == 0)
    def _():
        for r in range(nsub):
            rows = slice(r * tb, (r + 1) * tb)
            x = _stream_rows(x_refs, lead_tiles, pl.program_id(0), rows)
            h = _ln(x) * (1.0 + mods_ref[r, 1:2, :]) + mods_ref[r, 0:1, :]
            h_ref[rows, :] = h.astype(BF16)
        ug_ref[...] = _dot(h_ref[...], wg_ref[...])

    um_ref[...] = _dot(h_ref[...], wm_ref[...]).astype(BF16)


def _in_call(srcs, mods, wm, wg, layer, tm, tb):
    t = sum(s.shape[0] for s in srcs)
    d = srcs[0].shape[1]
    n = wm.shape[2]
    tn = 1024
    nsub = tm // tb
    x_specs, lead_tiles = _stream_specs(srcs, tm, d, lambda i, j: i)
    return pl.pallas_call(
        functools.partial(_in_kernel, n_src=len(srcs), lead_tiles=lead_tiles, nsub=nsub, tb=tb),
        out_shape=(jax.ShapeDtypeStruct((t, n), BF16), jax.ShapeDtypeStruct((t, GATE_W), F32)),
        grid=(t // tm, n // tn),
        in_specs=x_specs + [pl.BlockSpec((nsub, 8, d), lambda i, j: (i, 0, 0)),
                            pl.BlockSpec((None, d, tn), lambda i, j: (layer, 0, j)),
                            pl.BlockSpec((None, d, GATE_W), lambda i, j: (layer, 0, 0))],
        out_specs=(pl.BlockSpec((tm, tn), lambda i, j: (i, j)),
                   pl.BlockSpec((tm, GATE_W), lambda i, j: (i, 0))),
        scratch_shapes=[pltpu.VMEM((tm, d), BF16)],
        compiler_params=_cparams(("parallel", "arbitrary"), 48 + 8 * (len(srcs) - 1)),
        name="ln_in_proj",
    )(*srcs, mods, wm, wg)


MAIN_ORDER = ('m_q', 'm_k', 'm_v', 'm_o', 's_xbc', 'r_q', 'r_k', 'r_v', 'r_g', 's_z', 'g_q', 'g_k', 'g_v', 'g_g')
GATE_ORDER = ('m_i', 'm_f', 's_dt', 'g_a')


def _win_plan(d_model):
    off = _in_col_layout(d_model)
    gw = d_model // N_GROUPS
    scale = {'m_q': float(gw // M_HEADS) ** -0.5, 'r_k': float(gw // R_HEADS) ** -0.5,
             'g_q': float(gw // 2 // G_HEADS) ** -0.5}
    slabs = [(off[name][0] + k * LANES, scale.get(name, 1.0))
             for name in MAIN_ORDER for k in range(off[name][1] // LANES)]
    return slabs, [off[name] for name in GATE_ORDER]


def _win_kernel(w_ref, tail_ref, wm_ref, wg_ref, *, slabs, gates, n_full):
    lane = lax.broadcasted_iota(jnp.int32, (1, LANES), 1)
    rolled = {}

    def col(v):
        return tail_ref[...] if v == n_full else w_ref[:, v * LANES:(v + 1) * LANES]

    def rolled_col(v, s):
        if (v, s) not in rolled:
            rolled[(v, s)] = pltpu.roll(col(v), LANES - s, 1)
        return rolled[(v, s)]

    for j, (src, sc) in enumerate(slabs):
        v, s = divmod(src, LANES)
        val = col(v) if s == 0 else jnp.where(lane < LANES - s, rolled_col(v, s), rolled_col(v + 1, s))
        wm_ref[:, j * LANES:(j + 1) * LANES] = (val if sc == 1.0 else val * sc).astype(BF16)
    g = jnp.zeros(wg_ref.shape, F32)
    dst = 0
    for src, n in gates:
        v, s = divmod(src, LANES)
        assert s == dst, "gate group must already sit at its destination lane"
        g = jnp.where(jnp.logical_and(lane >= dst, lane < dst + n), col(v), g)
        dst += n
    wg_ref[...] = g.astype(BF16)


def _win_call(w_in):
    depth, d, n_in = w_in.shape
    slabs, gates = _win_plan(d)
    n_full = n_in // LANES
    assert n_in % LANES != 0 and sum(n for _, n in gates) <= GATE_W
    tr = 256
    nm = len(slabs) * LANES
    return pl.pallas_call(
        functools.partial(_win_kernel, slabs=slabs, gates=gates, n_full=n_full),
        out_shape=(jax.ShapeDtypeStruct((depth, d, nm), BF16), jax.ShapeDtypeStruct((depth, d, GATE_W), BF16)),
        grid=(depth, d // tr),
        in_specs=[pl.BlockSpec((None, tr, n_in), lambda l, r: (l, r, 0)),
                  pl.BlockSpec((None, tr, LANES), lambda l, r: (l, r, n_full))],
        out_specs=(pl.BlockSpec((None, tr, nm), lambda l, r: (l, r, 0)),
                   pl.BlockSpec((None, tr, GATE_W), lambda l, r: (l, r, 0))),
        compiler_params=_cparams(("parallel", "parallel"), 40),
        name="w_in_relayout",
    )(w_in, w_in)


def _up_kernel(x_ref, mods_ref, wa_ref, wg_ref, o_ref, h_ref, *, nsub, tb):
    @pl.when(pl.program_id(1) == 0)
    def _():
        _ln_mod_to(h_ref, x_ref, mods_ref, nsub, tb, 3, 4)

    a = _dot(h_ref[...], wa_ref[...])
    g = _dot(h_ref[...], wg_ref[...])
    o_ref[...] = (_silu(a) * g).astype(BF16)


def _up_call(xs, mods, mods_off, w_up, layer, tm, tb):
    t, d = xs.shape
    dff = w_up.shape[2] // 2
    tn = 512
    nj = dff // tn
    nsub = tm // tb
    return pl.pallas_call(
        functools.partial(_up_kernel, nsub=nsub, tb=tb),
        out_shape=jax.ShapeDtypeStruct((t, dff), BF16),
        grid=(t // tm, nj),
        in_specs=[pl.BlockSpec((tm, d), lambda i, j: (i, 0)),
                  pl.BlockSpec((nsub, 8, d), lambda i, j: (i + mods_off, 0, 0)),
                  pl.BlockSpec((None, d, tn), lambda i, j: (layer, 0, j)),
                  pl.BlockSpec((None, d, tn), lambda i, j: (layer, 0, j + nj))],
        out_specs=pl.BlockSpec((tm, tn), lambda i, j: (i, j)),
        scratch_shapes=[pltpu.VMEM((tm, d), BF16)],
        compiler_params=_cparams(("parallel", "arbitrary"), 48),
        name="ln_ffn_up",
    )(xs, mods, w_up, w_up)


def _res_kernel(a_ref, w_ref, x_ref, mods_ref, pg_ref, pb_ref, o_ref, *, nsub, tb, g_row, alpha):
    y = _dot(a_ref[...], w_ref[...])
    for r in range(nsub):
        rows = slice(r * tb, (r + 1) * tb)
        z = alpha * x_ref[rows, :] + mods_ref[r, g_row:g_row + 1, :] * y[rows, :]
        o_ref[rows, :] = _ln(z) * pg_ref[...] + pb_ref[...]


def _res_call(act, w, layer, xs, mods, mods_off, pg, pb, tm, tb, g_row, alpha, name):
    t, d = xs.shape
    ka = act.shape[1]
    nsub = tm // tb
    return pl.pallas_call(
        functools.partial(_res_kernel, nsub=nsub, tb=tb, g_row=g_row, alpha=alpha),
        out_shape=jax.ShapeDtypeStruct((t, d), F32),
        grid=(t // tm,),
        in_specs=[pl.BlockSpec((tm, ka), lambda i: (i, 0)),
                  pl.BlockSpec((None, ka, d), lambda i: (layer, 0, 0), pipeline_mode=pl.Buffered(1)),
                  pl.BlockSpec((tm, d), lambda i: (i, 0)),
                  pl.BlockSpec((nsub, 8, d), lambda i: (i + mods_off, 0, 0)),
                  pl.BlockSpec((1, d), lambda i: (0, 0)),
                  pl.BlockSpec((1, d), lambda i: (0, 0))],
        out_specs=pl.BlockSpec((tm, d), lambda i: (i, 0)),
        compiler_params=_cparams(("parallel",), 52),
        name=name,
    )(act, w, xs, mods, pg, pb)


def _head_norm(y, center):
    if center:
        y = y - jnp.mean(y, axis=-1, keepdims=True)
    return y * lax.rsqrt(jnp.mean(y * y, axis=-1, keepdims=True) + EPS)


def _mix_finalize(act_ref, rows, ys_ref, mo_ref, sz_ref, rg_ref, gg_ref, nrm_ref):
    gw = 4 * LANES
    for h in range(M_HEADS):
        hc = slice(h * LANES, (h + 1) * LANES)
        yn = _head_norm(ys_ref[rows, hc].astype(F32), True)
        act_ref[rows, hc] = (_sigmoid(mo_ref[rows, hc].astype(F32)) * (yn * nrm_ref[0:1, hc])).astype(BF16)
    halves = [slice(g * (gw // S_G), (g + 1) * (gw // S_G)) for g in range(S_G)]
    ys = [ys_ref[rows, gw + gc.start:gw + gc.stop].astype(F32) * _silu(sz_ref[rows, gc].astype(F32)) for gc in halves]
    inv = lax.rsqrt(sum(jnp.sum(y * y, axis=-1, keepdims=True) for y in ys) * (1.0 / gw) + EPS)
    for y, gc in zip(ys, halves):
        act_ref[rows, gw + gc.start:gw + gc.stop] = (y * inv * nrm_ref[1:2, gc]).astype(BF16)
    for h in range(R_HEADS):
        hc = slice(h * LANES, (h + 1) * LANES)
        yn = _head_norm(ys_ref[rows, 2 * gw + h * LANES:2 * gw + (h + 1) * LANES].astype(F32), True)
        act_ref[rows, 2 * gw + h * LANES:2 * gw + (h + 1) * LANES] = (
            yn * nrm_ref[2:3, hc] * _silu(rg_ref[rows, hc].astype(F32))).astype(BF16)
    for h in range(G_HEADS):
        hc = slice(h * LANES, (h + 1) * LANES)
        yn = _head_norm(ys_ref[rows, 3 * gw + h * LANES:3 * gw + (h + 1) * LANES].astype(F32), False)
        act_ref[rows, 3 * gw + h * LANES:3 * gw + (h + 1) * LANES] = (
            yn * nrm_ref[3:4, hc] * _silu(gg_ref[rows, hc].astype(F32))).astype(BF16)


def _out_kernel(*refs, n_src, lead_tiles, skip, nsub, tb, alpha):
    ys_ref, mo_ref, sz_ref, rg_ref, gg_ref, nrm_ref, w_ref = refs[:7]
    x_refs = refs[7:7 + n_src]
    mods_ref, pg_ref, pb_ref, o_ref, act_ref = refs[7 + n_src:]
    for r in range(nsub):
        rows = slice(r * tb, (r + 1) * tb)
        _mix_finalize(act_ref, rows, ys_ref, mo_ref, sz_ref, rg_ref, gg_ref, nrm_ref)
        y = _dot(act_ref[rows, :], w_ref[...])
        x = _stream_rows(x_refs, lead_tiles, pl.program_id(0) + skip, rows)
        z = alpha * x + mods_ref[r, 2:3, :] * y
        o_ref[rows, :] = _ln(z) * pg_ref[...] + pb_ref[...]


def _out_call(ysum, um, nrm, w, layer, srcs, mods, skip, pg, pb, tm, tb, alpha):
    t = sum(s.shape[0] for s in srcs)
    d = srcs[0].shape[1]
    gw = d // N_GROUPS
    nsub = tm // tb
    x_specs, lead_tiles = _stream_specs(srcs, tm, d, lambda i: i + skip)

    def gate(cb):
        return pl.BlockSpec((tm, gw), lambda i: (i + skip, cb))

    return pl.pallas_call(
        functools.partial(_out_kernel, n_src=len(srcs), lead_tiles=lead_tiles, skip=skip, nsub=nsub, tb=tb,
                          alpha=alpha),
        out_shape=jax.ShapeDtypeStruct((t - skip * tm, d), F32),
        grid=(t // tm - skip,),
        in_specs=[pl.BlockSpec((tm, d), lambda i: (i + skip, 0)), gate(3), gate(10), gate(9), gate(13),
                  pl.BlockSpec(nrm.shape, lambda i: (0, 0)),
                  pl.BlockSpec((None, d, d), lambda i: (layer, 0, 0), pipeline_mode=pl.Buffered(1))]
                 + x_specs
                 + [pl.BlockSpec((nsub, 8, d), lambda i: (i + skip, 0, 0)),
                    pl.BlockSpec((1, d), lambda i: (0, 0)),
                    pl.BlockSpec((1, d), lambda i: (0, 0))],
        out_specs=pl.BlockSpec((tm, d), lambda i: (i, 0)),
        scratch_shapes=[pltpu.VMEM((tm, d), BF16)],
        compiler_params=_cparams(("parallel",), 52),
        name="mix_out_proj_res_ln",
    )(ysum, um, um, um, um, nrm, w, *srcs, mods, pg, pb)


def _prep_kernel(cur_ref, prev_ref, next_ref, rqk_ref, cw_ref, cb_ref, rope_ref, o_ref, *, nctx, nblk, tb):
    t = pl.program_id(1)
    seg_start = jnp.logical_or(t == 0, t == nctx)
    seg_end = jnp.logical_or(t == nctx - 1, t == nblk - 1)
    nc = cur_ref.shape[1]
    ridx = lax.broadcasted_iota(jnp.int32, (tb, 1), 0)
    for blk in range(nc // LANES):
        cs = slice(blk * LANES, (blk + 1) * LANES)
        x = cur_ref[:, cs].astype(F32)
        prev_row = jnp.where(seg_start, 0.0, prev_ref[:, cs].astype(F32)[15:16, :])
        next_row = jnp.where(seg_end, 0.0, next_ref[:, cs].astype(F32)[0:1, :])
        xp = jnp.where(ridx == 0, prev_row, pltpu.roll(x, 1, 0))
        xn = jnp.where(ridx == tb - 1, next_row, pltpu.roll(x, tb - 1, 0))
        y = cb_ref[:, cs] + xp * cw_ref[0:1, cs] + x * cw_ref[1:2, cs] + xn * cw_ref[2:3, cs]
        o_ref[:, cs] = _silu(y).astype(BF16)

    cos = rope_ref[:, 0:LANES]
    sin = rope_ref[:, LANES:2 * LANES]
    lane = lax.broadcasted_iota(jnp.int32, (1, LANES), 1)
    first = (lane % 64) < 32
    for blk in range(rqk_ref.shape[1] // LANES):
        xh = rqk_ref[:, blk * LANES:(blk + 1) * LANES].astype(F32)
        partner = jnp.where(first, pltpu.roll(xh, 96, 1), pltpu.roll(xh, 32, 1))
        o_ref[:, nc + blk * LANES:nc + (blk + 1) * LANES] = (xh * cos + partner * sin).astype(BF16)


def _flat_block(b, nctx, nblk):
    nlat = nblk - nctx
    return lambda i, t: jnp.where(t < nctx, i * nctx + t, b * nctx + i * nlat + (t - nctx))


def _prep_call(um, conv_w, conv_b, rope, b, nctx, tb):
    t_rows = um.shape[0]
    nblk = t_rows // (b * tb)
    hb = tb // 16
    nc = conv_w.shape[1]
    fb = _flat_block(b, nctx, nblk)
    return pl.pallas_call(
        functools.partial(_prep_kernel, nctx=nctx, nblk=nblk, tb=tb),
        out_shape=jax.ShapeDtypeStruct((t_rows, 2 * nc), BF16),
        grid=(b, nblk),
        in_specs=[pl.BlockSpec((tb, nc), lambda i, t: (fb(i, t), 2)),
                  pl.BlockSpec((16, nc), lambda i, t: (jnp.maximum(fb(i, t) * hb - 1, 0), 2)),
                  pl.BlockSpec((16, nc), lambda i, t: (jnp.minimum((fb(i, t) + 1) * hb, t_rows // 16 - 1), 2)),
                  pl.BlockSpec((tb, nc), lambda i, t: (fb(i, t), 3)),
                  pl.BlockSpec((8, nc), lambda i, t: (0, 0)),
                  pl.BlockSpec((1, nc), lambda i, t: (0, 0)),
                  pl.BlockSpec((tb, 2 * LANES), lambda i, t: (t, 0))],
        out_specs=pl.BlockSpec((tb, 2 * nc), lambda i, t: (fb(i, t), 0)),
        compiler_params=_cparams(("parallel", "parallel"), 32),
        name="prep_conv_rope",
    )(um, um, um, um, conv_w, conv_b, rope)


def _mlstm_a(rev, L, gate_ref, bias_ref):
    lane = lax.broadcasted_iota(jnp.int32, (1, LANES), 1)
    is_f = jnp.logical_and(lane >= 8, lane < 16)
    g = gate_ref[...] + bias_ref[0:1, :]
    gp = jnp.where(is_f, _log_sigmoid(g), g)
    return gp, _tri_cumsum(_causal(L, rev).astype(BF16), gp)


def _mlstm_b(qkvo_ref, caug_ref):
    gw = 4 * LANES
    out = []
    for h in range(M_HEADS):
        q = qkvo_ref[:, h * LANES:(h + 1) * LANES]
        k = qkvo_ref[:, gw + h * LANES:gw + (h + 1) * LANES]
        out.append((_dot_nt(q, k), _dot(q, caug_ref[h].astype(BF16))))
    return out


def _mlstm_c(rev, final, L, pa, pb, qkvo_ref, caug_ref, m_ref, o_ref, oc, yb_ref):
    d = 1 if rev else 0
    gw = 4 * LANES
    gp, cum = pa
    mask = _causal(L, rev)
    e_idx = 0 if rev else L - 1
    ones = jnp.ones((L, LANES), BF16)
    gp_t = gp.T
    cum_t = cum.T
    for h in range(M_HEADS):
        ci = 4 * d + h
        cf = 8 + 4 * d + h
        b_col = cum[:, cf:cf + 1]
        b_row = cum_t[cf:cf + 1, :]
        ig_col = gp[:, ci:ci + 1]
        ig_row = gp_t[ci:ci + 1, :]
        m_st = m_ref[h, 0:1, 0:1]
        log_intra = jnp.where(mask, b_col - b_row + ig_row, NEG)
        log_inter = b_col + m_st
        m_i = jnp.maximum(log_inter, jnp.max(log_intra, axis=1, keepdims=True))
        w = jnp.exp(log_intra - m_i)
        w_inter = jnp.exp(log_inter - m_i)
        b_end = cum[e_idx:e_idx + 1, cf:cf + 1]
        ls_row = b_end - b_row + ig_row
        m_new = jnp.maximum(b_end + m_st, jnp.max(ls_row, axis=1, keepdims=True))
        ws_col = jnp.exp(b_end - b_col + ig_col - m_new)
        decay = jnp.exp(b_end + m_st - m_new)
        k = qkvo_ref[:, gw + h * LANES:gw + (h + 1) * LANES]
        v = qkvo_ref[:, 2 * gw + h * LANES:2 * gw + (h + 1) * LANES]
        vaug = jnp.concatenate([v, ones], axis=1)
        qk, qc = pb[h]
        s = (qk * w).astype(BF16)
        res = _dot(s, vaug) + w_inter * qc
        kws = (k.astype(F32) * ws_col).astype(BF16)
        caug_ref[h] = decay * caug_ref[h] + _dot_tn(kws, vaug)
        m_ref[h] = jnp.broadcast_to(m_new, (8, LANES))
        hh = res[:, 0:LANES] / jnp.maximum(jnp.abs(res[:, LANES:2 * LANES]), jnp.exp(-m_i))
        cols = slice(oc + h * LANES, oc + (h + 1) * LANES)
        if final:
            o_ref[:, cols] = (hh + yb_ref[:, cols]).astype(BF16)
        else:
            o_ref[:, cols] = hh


def _ssd_a(rev, L, gate_ref, prm_ref):
    dt_all = _softplus(gate_ref[...] + prm_ref[0:1, :])
    return dt_all, _tri_cumsum(_causal(L, rev).astype(BF16), dt_all * (-jnp.exp(prm_ref[1:2, :])))


def _ssd_b(xbc_ref, st_ref):
    gw = 4 * LANES
    out = []
    for g in range(S_G):
        bg = xbc_ref[:, gw + g * S_N:gw + (g + 1) * S_N]
        cg = xbc_ref[:, gw + S_G * S_N + g * S_N:gw + S_G * S_N + (g + 1) * S_N]
        out.append((_dot_nt(cg, bg), _dot(cg, st_ref[g].astype(BF16))))
    return out


def _ssd_c(rev, final, L, pa, pb, xbc_ref, st_ref, o_ref, oc, yb_ref, dsk_ref):
    d = 1 if rev else 0
    gw = 4 * LANES
    hpg = S_HEADS // S_G
    gcols = gw // S_G
    dt_all, cum = pa
    mask = _causal(L, rev)
    e_idx = 0 if rev else L - 1
    lane_head = lax.broadcasted_iota(jnp.int32, (1, gcols), 1) // (gcols // hpg)
    cum_t = cum.T
    for g in range(S_G):
        wts, e_cols, dec_cols, dt_cols, decays = [], [], [], [], []
        for hl in range(hpg):
            c = 16 + S_HEADS * d + g * hpg + hl
            b_col = cum[:, c:c + 1]
            b_row = cum_t[c:c + 1, :]
            b_end = cum[e_idx:e_idx + 1, c:c + 1]
            wts.append(jnp.exp(jnp.where(mask, b_col - b_row, NEG)))
            e_cols.append(jnp.exp(b_col))
            dec_cols.append(jnp.exp(b_end - b_col))
            dt_cols.append(dt_all[:, c:c + 1])
            decays.append(jnp.exp(b_end))
        gc = slice(g * gcols, (g + 1) * gcols)
        xs_g = xbc_ref[:, gc].astype(F32)
        bg = xbc_ref[:, gw + g * S_N:gw + (g + 1) * S_N]
        xdt = xs_g * _expand4(dt_cols, lane_head)
        gm, inter = pb[g]
        acc = _expand4(e_cols, lane_head) * inter
        for hl in range(hpg):
            xh = jnp.where(lane_head == hl, xdt, 0.0).astype(BF16)
            acc = acc + _dot((gm * wts[hl]).astype(BF16), xh)
        xdec = (xdt * _expand4(dec_cols, lane_head)).astype(BF16)
        st_ref[g] = _expand4(decays, lane_head) * st_ref[g] + _dot_tn(bg, xdec)
        cols = slice(oc + g * gcols, oc + (g + 1) * gcols)
        if final:
            o_ref[:, cols] = (acc + yb_ref[:, cols] + dsk_ref[0:1, gc] * xs_g).astype(BF16)
        else:
            o_ref[:, cols] = acc


def _ret_b(qk_ref, qc, st_ref):
    gw = 4 * LANES
    out = []
    for h in range(R_HEADS):
        q = qk_ref[:, qc + h * LANES:qc + (h + 1) * LANES]
        k = qk_ref[:, qc + gw + h * LANES:qc + gw + (h + 1) * LANES]
        out.append((_dot_nt(q, k), _dot(q, st_ref[h].astype(BF16))))
    return out


def _ret_c(rev, final, L, pb, qk_ref, qc, v_ref, prm_ref, st_ref, o_ref, oc, yb_ref):
    d = 1 if rev else 0
    gw = 4 * LANES
    mask = _causal(L, rev)
    icol = lax.broadcasted_iota(jnp.int32, (L, 1), 0).astype(F32)
    row = lax.broadcasted_iota(jnp.int32, (L, L), 0)
    col = lax.broadcasted_iota(jnp.int32, (L, L), 1)
    dist = ((col - row) if rev else (row - col)).astype(F32)
    lg_all = -jnp.exp(prm_ref[d:d + 1, :])
    for h in range(R_HEADS):
        lg = lg_all[:, h:h + 1]
        w = jnp.exp(jnp.where(mask, dist * lg, NEG))
        if rev:
            e_col = jnp.exp((float(L) - icol) * lg)
            dec_col = jnp.exp(icol * lg)
        else:
            e_col = jnp.exp((icol + 1.0) * lg)
            dec_col = jnp.exp((float(L - 1) - icol) * lg)
        decay = jnp.exp(float(L) * lg)
        hc = slice(h * LANES, (h + 1) * LANES)
        k = qk_ref[:, qc + gw + h * LANES:qc + gw + (h + 1) * LANES]
        v = v_ref[:, hc]
        qk, qst = pb[h]
        y = _dot((qk * w).astype(BF16), v) + e_col * qst
        kd = (k.astype(F32) * dec_col).astype(BF16)
        st_ref[h] = decay * st_ref[h] + _dot_tn(kd, v)
        cols = slice(oc + h * LANES, oc + (h + 1) * LANES)
        if final:
            o_ref[:, cols] = (y + yb_ref[:, cols]).astype(BF16)
        else:
            o_ref[:, cols] = y


def _gla_a(rev, L, tb, gate_ref, gw_ref, gb_ref):
    row = lax.broadcasted_iota(jnp.int32, (tb, tb), 0)
    col = lax.broadcasted_iota(jnp.int32, (tb, tb), 1)
    blockmask = jnp.logical_and((row // L) == (col // L), (col >= row) if rev else (col <= row))
    z = _dot_f32(gate_ref[...], gw_ref[...]) + gb_ref[...]
    return blockmask, _tri_cumsum(blockmask.astype(BF16), _log_sigmoid(z) * (1.0 / G_TAU))


def _gla_c(rev, final, L, nsub, pa, qk_ref, v_ref, st_ref, o_ref, oc, yb_ref):
    kw = 2 * LANES
    blockmask, cum = pa
    e_idx = 0 if rev else L - 1
    lane_head = lax.broadcasted_iota(jnp.int32, (1, kw), 1) // (kw // G_HEADS)
    sub = [slice(c * L, (c + 1) * L) for c in range(nsub)]
    qf = qk_ref[:, 0:kw].astype(F32)
    kf = qk_ref[:, kw:2 * kw].astype(F32)
    qg = qf * jnp.exp(cum)
    kg = (kf * jnp.exp(-cum)).astype(BF16)
    b_ends = [cum[c * L + e_idx:c * L + e_idx + 1, :] for c in range(nsub)]
    kd = jnp.concatenate([kf[sub[c], :] * jnp.exp(b_ends[c] - cum[sub[c], :]) for c in range(nsub)], axis=0)
    qgh = [jnp.where(lane_head == h, qg, 0.0).astype(BF16) for h in range(G_HEADS)]
    kdh = [jnp.where(lane_head == h, kd, 0.0).astype(BF16) for h in range(G_HEADS)]
    s_all = [_dot_nt(qgh[h], kg) for h in range(G_HEADS)]
    upd = [sum(_dot_tn(v_ref[sub[c], h * LANES:(h + 1) * LANES], kdh[h][sub[c], :]) for h in range(G_HEADS))
           for c in range(nsub)]
    st = st_ref[...]
    st_at = [None] * nsub
    for c in (reversed(range(nsub)) if rev else range(nsub)):
        st_at[c] = st.astype(BF16)
        st = jnp.exp(b_ends[c]) * st + upd[c]
    st_ref[...] = st
    for h in range(G_HEADS):
        hc = slice(h * LANES, (h + 1) * LANES)
        cols = slice(oc + h * LANES, oc + (h + 1) * LANES)
        inter = jnp.concatenate([_dot_nt(qgh[h][sub[c], :], st_at[c]) for c in range(nsub)], axis=0)
        y = _dot(jnp.where(blockmask, s_all[h], 0.0).astype(BF16), v_ref[:, hc]) + inter
        if final:
            o_ref[:, cols] = (y + yb_ref[:, cols]).astype(BF16)
        else:
            o_ref[:, cols] = y


def _scan_kernel(*refs, rev, final, tb):
    (um_m, ug, prep, r_v, g_qk, g_v, m_bias, s_prm, r_prm, gwp, gb) = refs[:11]
    if final:
        yb, dsk = refs[11:13]
        o_ref = refs[13]
        scratch = refs[14:]
    else:
        yb = dsk = None
        o_ref = refs[11]
        scratch = refs[12:]
    caug, m_st, s_st, r_st, g_st = scratch
    gw = 4 * LANES

    @pl.when(pl.program_id(1) == 0)
    def _():
        for ref in scratch:
            ref[...] = jnp.zeros_like(ref)

    m_a = _mlstm_a(rev, tb, ug, m_bias)
    s_a = _ssd_a(rev, tb, ug, s_prm)
    g_a = _gla_a(rev, GLA_CHUNK, tb, ug, gwp, gb)
    r_b = _ret_b(prep, 2 * gw, r_st)
    m_b = _mlstm_b(um_m, caug)
    s_b = _ssd_b(prep, s_st)
    _ret_c(rev, final, tb, r_b, prep, 2 * gw, r_v, r_prm, r_st, o_ref, 2 * gw, yb)
    _gla_c(rev, final, GLA_CHUNK, tb // GLA_CHUNK, g_a, g_qk, g_v, g_st, o_ref, 3 * gw, yb)
    _mlstm_c(rev, final, tb, m_a, m_b, um_m, caug, m_st, o_ref, 0, yb)
    _ssd_c(rev, final, tb, s_a, s_b, prep, s_st, o_ref, gw, yb, dsk)


def _scan_call(rev, final, b, tb, nctx, um, ug, prep, params, yb, dsk):
    t_rows = um.shape[0]
    gw = 4 * LANES
    nblk = t_rows // (b * tb)
    fb = _flat_block(b, nctx, nblk)
    if rev:
        def order(c):
            return jnp.where(c < nctx, nctx - 1 - c, nblk + nctx - 1 - c)
    else:
        def order(c):
            return c

    def tok(width, cb):
        return pl.BlockSpec((tb, width), lambda i, c: (fb(i, order(c)), cb))

    def full(arr):
        return pl.BlockSpec(arr.shape, lambda i, c: (0,) * arr.ndim)

    ins = [um, ug, prep, um, um, um] + list(params)
    in_specs = [tok(4 * gw, 0), tok(GATE_W, 0), tok(4 * gw, 0), tok(gw, 8), tok(gw, 11), tok(gw, 12)]
    in_specs += [full(p) for p in params]
    if final:
        ins += [yb, dsk]
        in_specs += [tok(4 * gw, 0), full(dsk)]
    scratch = [pltpu.VMEM((M_HEADS, LANES, 2 * LANES), F32), pltpu.VMEM((M_HEADS, 8, LANES), F32),
               pltpu.VMEM((S_G, S_N, gw // S_G), F32), pltpu.VMEM((R_HEADS, LANES, LANES), F32),
               pltpu.VMEM((LANES, 2 * LANES), F32)]
    return pl.pallas_call(
        functools.partial(_scan_kernel, rev=rev, final=final, tb=tb),
        out_shape=jax.ShapeDtypeStruct((t_rows, 4 * gw), BF16 if final else F32),
        grid=(b, nblk),
        in_specs=in_specs,
        out_specs=tok(4 * gw, 0),
        scratch_shapes=scratch,
        compiler_params=_cparams(("parallel", "arbitrary"), 48),
        name="scan_fwd" if final else "scan_bwd",
    )(*ins)


def _in_col_layout(d_model):
    gw = d_model // N_GROUPS
    conv_ch = gw + 2 * S_G * S_N
    names = [('m_q', gw), ('m_k', gw), ('m_v', gw), ('m_o', gw), ('m_i', 2 * M_HEADS), ('m_f', 2 * M_HEADS),
             ('s_z', gw), ('s_xbc', conv_ch), ('s_dt', 2 * S_HEADS),
             ('r_q', gw), ('r_k', gw), ('r_v', gw), ('r_g', gw),
             ('g_q', gw // 2), ('g_k', gw // 2), ('g_v', gw), ('g_g', gw), ('g_a', 2 * G_RANK)]
    off, o = {}, 0
    for nm, n in names:
        off[nm] = (o, n)
        o += n
    return off


def kernel(x, c, ctx, c_ctx, ada_w, ada_b, w_in, m_ig_b, m_fg_b, m_norm, s_conv_w, s_conv_b, s_dt_bias, s_a_log, s_d, s_norm, r_decay, r_norm, g_gate_w, g_gate_b, g_norm, w_out, post_g, post_b, ffn_w_up, ffn_w_down):
    b, seq, d = x.shape
    n_ctx_tok = ctx.shape[1]
    depth = ada_w.shape[0]
    gw = d // N_GROUPS
    assert gw == 4 * LANES and b + 1 <= 8
    s_tot = n_ctx_tok + seq
    t = b * s_tot
    tb = math.gcd(MAX_TOKEN_BLOCK, math.gcd(n_ctx_tok, seq))
    assert tb % 16 == 0 and tb % GLA_CHUNK == 0
    nctx = n_ctx_tok // tb
    nblk = s_tot // tb
    nsb = t // tb
    t_ctx = b * n_ctx_tok
    t_lat = b * seq

    def row_tile(cap):
        return max(m for m in (1024, 512, 256, 128, 64, 32, 16)
                   if m <= cap and t_ctx % m == 0 and t_lat % m == 0 and m % tb == 0)

    tm = row_tile(1024)
    tm_out = row_tile(512)
    alpha = (2.0 * depth) ** 0.25

    pos = jnp.arange(seq)
    quarter = LANES // 4
    freqs = 1.0 / (ROPE_BASE ** (jnp.arange(quarter, dtype=F32) / quarter))
    ang_r = (pos // GRID_W).astype(F32)[:, None] * freqs[None, :]
    ang_c = (pos % GRID_W).astype(F32)[:, None] * freqs[None, :]
    cos_t = jnp.concatenate([jnp.cos(ang_r)] * 2 + [jnp.cos(ang_c)] * 2, axis=1)
    sin_t = jnp.concatenate([-jnp.sin(ang_r), jnp.sin(ang_r), -jnp.sin(ang_c), jnp.sin(ang_c)], axis=1)
    rope = jnp.concatenate([
        jnp.concatenate([jnp.ones((n_ctx_tok, LANES), F32), jnp.zeros((n_ctx_tok, LANES), F32)], axis=1),
        jnp.concatenate([cos_t, sin_t], axis=1)], axis=0)

    xs = (ctx.reshape(t_ctx, d), x.reshape(t_lat, d))
    cc =jnp.concatenate([c, c_ctx[None, :], jnp.zeros((8 - b - 1, d), F32)], axis=0)
    mod = _mod_call(cc, ada_w, ada_b)
    sb = np.arange(nsb)
    mod_row = np.where(sb < b * nctx, b, (sb - b * nctx) // (nblk - nctx))

    def lane_rows(vecs, start, n_rows):
        rows = jnp.stack(vecs)
        return jnp.pad(rows, ((0, n_rows - rows.shape[0]), (start, LANES - start - rows.shape[1])))

    wm, wg = _win_call(w_in)
    w_out_b = w_out.astype(BF16)
    w_up_b = ffn_w_up.astype(BF16)
    w_down_b = ffn_w_down.astype(BF16)
    for l in range(depth):
        mods = jnp.pad(mod[l][mod_row].reshape(nsb, 6, d), ((0, 0), (0, 2), (0, 0)))
        um, ug = _in_call(xs, mods, wm, wg, l, tm, tb)

        conv_w = jnp.pad(s_conv_w[l], ((0, 8 - CONV_W), (0, 0)))
        prep = _prep_call(um, conv_w, s_conv_b[l][None, :], rope, b, nctx, tb)

        m_bias = lane_rows([jnp.concatenate([m_ig_b[l].reshape(-1), m_fg_b[l].reshape(-1)])], 0, 1)
        s_prm = lane_rows([s_dt_bias[l].reshape(-1), s_a_log[l].reshape(-1)], 16, 8)
        r_prm = lane_rows([r_decay[l][0], r_decay[l][1]], 0, 8)
        nrm = jnp.concatenate([jnp.stack([m_norm[l], s_norm[l], r_norm[l], g_norm[l]]), jnp.zeros((4, gw), F32)])
        dsk = jnp.repeat(s_d[l], gw // S_HEADS)[None, :]

        def dir_params(dd):
            lo = 32 + G_RANK * dd
            gwp = jnp.pad(g_gate_w[l][dd], ((lo, GATE_W - lo - G_RANK), (0, 0)))
            return [m_bias, s_prm, r_prm, gwp, g_gate_b[l][dd][None, :]]

        yb = _scan_call(True, False, b, tb, nctx, um, ug, prep, dir_params(1), None, None)
        ysum = _scan_call(False, True, b, tb, nctx, um, ug, prep, dir_params(0), yb, dsk)

        skip = t_ctx if l == depth - 1 else 0
        x1 = _out_call(ysum, um, nrm, w_out_b, l, xs, mods, skip // tm_out,
                       post_g[l, 0][None, :], post_b[l, 0][None, :], tm_out, tb, alpha)
        act = _up_call(x1, mods, skip // tm, w_up_b, l, tm, tb)
        xs = (_res_call(act, w_down_b, l, x1, mods, skip // tb,
                        post_g[l, 1][None, :], post_b[l, 1][None, :], tb, tb, 5, alpha, "ffn_down_res_ln"),)

    return xs[0].reshape(b, seq, d)
```

```python
import functools
import math

import numpy as np
import jax
import jax.numpy as jnp
from jax import lax
from jax.experimental import pallas as pl
from jax.experimental.pallas import tpu as pltpu

F32 = jnp.float32
BF16 = jnp.bfloat16

EPS = 1e-5
ROPE_BASE = 10000.0
GRID_W = 64
N_GROUPS = 4
M_HEADS = 4
S_HEADS = 8
S_G = 2
S_N = 128
CONV_W = 3
R_HEADS = 4
G_HEADS = 4
G_RANK = 16
G_TAU = 16.0
NEG = -1e30

LANES = 128
GATE_W = LANES
MAX_TOKEN_BLOCK = 256
GLA_CHUNK = 64


def _cparams(sem, vmem_mb):
    return pltpu.CompilerParams(dimension_semantics=sem, vmem_limit_bytes=vmem_mb * 1024 * 1024)


def _dot(a, b):
    return jnp.dot(a, b, preferred_element_type=F32)


def _dot_nt(a, b):
    return lax.dot_general(a, b, (((1,), (1,)), ((), ())), preferred_element_type=F32)


def _dot_tn(a, b):
    return lax.dot_general(a, b, (((0,), (0,)), ((), ())), preferred_element_type=F32)


def _split3(x):
    hi = x.astype(BF16)
    r1 = x - hi.astype(F32)
    mid = r1.astype(BF16)
    lo = (r1 - mid.astype(F32)).astype(BF16)
    return hi, mid, lo


def _tri_cumsum(tri, x):
    hi, mid, lo = _split3(x)
    return _dot(tri, hi) + _dot(tri, mid) + _dot(tri, lo)


def _dot_f32(a, b):
    ah = a.astype(BF16)
    al = (a - ah.astype(F32)).astype(BF16)
    bh = b.astype(BF16)
    bl = (b - bh.astype(F32)).astype(BF16)
    return _dot(ah, bh) + _dot(ah, bl) + _dot(al, bh)


def _sigmoid(x):
    return 1.0 / (1.0 + jnp.exp(-x))


def _silu(x):
    return x * _sigmoid(x)


def _softplus(x):
    return jnp.maximum(x, 0.0) + jnp.log1p(jnp.exp(-jnp.abs(x)))


def _log_sigmoid(x):
    return jnp.minimum(x, 0.0) - jnp.log1p(jnp.exp(-jnp.abs(x)))


def _ln(x):
    mu = jnp.mean(x, axis=-1, keepdims=True)
    xc = x - mu
    var = jnp.mean(xc * xc, axis=-1, keepdims=True)
    return xc * lax.rsqrt(var + EPS)


def _causal(n, rev):
    row = lax.broadcasted_iota(jnp.int32, (n, n), 0)
    col = lax.broadcasted_iota(jnp.int32, (n, n), 1)
    return (col >= row) if rev else (col <= row)


def _expand4(cols, lane_head):
    return jnp.where(lane_head == 0, cols[0],
                     jnp.where(lane_head == 1, cols[1],
                               jnp.where(lane_head == 2, cols[2], cols[3])))


MOD_SPLIT = 4


def _mod_kernel(c_ref, *refs):
    w_refs = refs[:MOD_SPLIT]
    b_ref, o_ref = refs[MOD_SPLIT:]
    s = _silu(c_ref[...]).astype(BF16)
    wn = w_refs[0].shape[2]
    for q, w_ref in enumerate(w_refs):
        cols = slice(q * wn, (q + 1) * wn)
        part = _dot(s, w_ref[0].astype(BF16))

        @pl.when(pl.program_id(1) == 0)
        def _():
            o_ref[0, :, cols] = part + b_ref[0, :, cols]

        @pl.when(pl.program_id(1) > 0)
        def _():
            o_ref[0, :, cols] += part


def _mod_call(cc, ada_w, ada_b):
    depth, d, n = ada_w.shape
    tk = 256
    wn = n // MOD_SPLIT
    w_specs = [pl.BlockSpec((1, tk, wn), functools.partial(lambda l, k, q: (l, k, q), q=q))
               for q in range(MOD_SPLIT)]
    return pl.pallas_call(
        _mod_kernel,
        out_shape=jax.ShapeDtypeStruct((depth, 8, n), F32),
        grid=(depth, d // tk),
        in_specs=[pl.BlockSpec((8, tk), lambda l, k: (0, k))] + w_specs
                 + [pl.BlockSpec((1, 1, n), lambda l, k: (l, 0, 0))],
        out_specs=pl.BlockSpec((1, 8, n), lambda l, k: (l, 0, 0)),
        compiler_params=_cparams(("parallel", "arbitrary"), 40),
        name="adaln_mod",
    )(cc, *([ada_w] * MOD_SPLIT), ada_b.reshape(depth, 1, n))


def _stream_specs(srcs, tm, d, row_tile):
    if len(srcs) == 1:
        return [pl.BlockSpec((tm, d), lambda *g: (row_tile(*g), 0))], 0
    lead_tiles = srcs[0].shape[0] // tm
    return [pl.BlockSpec((tm, d), lambda *g: (jnp.minimum(row_tile(*g), lead_tiles - 1), 0),
                         pipeline_mode=pl.Buffered(1)),
            pl.BlockSpec((tm, d), lambda *g: (jnp.maximum(row_tile(*g) - lead_tiles, 0), 0))], lead_tiles


def _stream_rows(x_refs, lead_tiles, tile, rows):
    if len(x_refs) == 1:
        return x_refs[0][rows, :]
    return jnp.where(tile < lead_tiles, x_refs[0][rows, :], x_refs[1][rows, :])


def _ln_mod_to(h_ref, x_ref, mods_ref, nsub, tb, sh_row, sc_row):
    for r in range(nsub):
        x = x_ref[r * tb:(r + 1) * tb, :]
        h = _ln(x) * (1.0 + mods_ref[r, sc_row:sc_row + 1, :]) + mods_ref[r, sh_row:sh_row + 1, :]
        h_ref[r * tb:(r + 1) * tb, :] = h.astype(BF16)


def _in_kernel(*refs, n_src, lead_tiles, nsub, tb):
    x_refs = refs[:n_src]
    mods_ref, wm_ref, wg_ref, um_ref, ug_ref, h_ref = refs[n_src:]

    @pl.when(pl.program_id(1) == 0)
    def _():
        for r in range(nsub):
            rows = slice(r * tb, (r + 1) * tb)
            x = _stream_rows(x_refs, lead_tiles, pl.program_id(0), rows)
            h = _ln(x) * (1.0 + mods_ref[r, 1:2, :]) + mods_ref[r, 0:1, :]
            h_ref[rows, :] = h.astype(BF16)
        ug_ref[...] = _dot(h_ref[...], wg_ref[...])

    um_ref[...] = _dot(h_ref[...], wm_ref[...]).astype(BF16)


def _in_call(srcs, mods, wm, wg, layer, tm, tb):
    t = sum(s.shape[0] for s in srcs)
    d = srcs[0].shape[1]
    n = wm.shape[2]
    tn = 1024
    nsub = tm // tb
    x_specs, lead_tiles = _stream_specs(srcs, tm, d, lambda i, j: i)
    return pl.pallas_call(
        functools.partial(_in_kernel, n_src=len(srcs), lead_tiles=lead_tiles, nsub=nsub, tb=tb),
        out_shape=(jax.ShapeDtypeStruct((t, n), BF16), jax.ShapeDtypeStruct((t, GATE_W), F32)),
        grid=(t // tm, n // tn),
        in_specs=x_specs + [pl.BlockSpec((nsub, 8, d), lambda i, j: (i, 0, 0)),
                            pl.BlockSpec((None, d, tn), lambda i, j: (layer, 0, j)),
                            pl.BlockSpec((None, d, GATE_W), lambda i, j: (layer, 0, 0))],
        out_specs=(pl.BlockSpec((tm, tn), lambda i, j: (i, j)),
                   pl.BlockSpec((tm, GATE_W), lambda i, j: (i, 0))),
        scratch_shapes=[pltpu.VMEM((tm, d), BF16)],
        compiler_params=_cparams(("parallel", "arbitrary"), 48 + 8 * (len(srcs) - 1)),
        name="ln_in_proj",
    )(*srcs, mods, wm, wg)


MAIN_ORDER = ('m_q', 'm_k', 'm_v', 'm_o', 's_xbc', 'r_q', 'r_k', 'r_v', 'r_g', 's_z', 'g_q', 'g_k', 'g_v', 'g_g')
GATE_ORDER = ('m_i', 'm_f', 's_dt', 'g_a')


def _win_plan(d_model):
    off = _in_col_layout(d_model)
    gw = d_model // N_GROUPS
    scale = {'m_q': float(gw // M_HEADS) ** -0.5, 'r_k': float(gw // R_HEADS) ** -0.5,
             'g_q': float(gw // 2 // G_HEADS) ** -0.5}
    slabs = [(off[name][0] + k * LANES, scale.get(name, 1.0))
             for name in MAIN_ORDER for k in range(off[name][1] // LANES)]
    return slabs, [off[name] for name in GATE_ORDER]


def _win_kernel(w_ref, tail_ref, wm_ref, wg_ref, *, slabs, gates, n_full):
    lane = lax.broadcasted_iota(jnp.int32, (1, LANES), 1)
    rolled = {}

    def col(v):
        return tail_ref[...] if v == n_full else w_ref[:, v * LANES:(v + 1) * LANES]

    def rolled_col(v, s):
        if (v, s) not in rolled:
            rolled[(v, s)] = pltpu.roll(col(v), LANES - s, 1)
        return rolled[(v, s)]

    for j, (src, sc) in enumerate(slabs):
        v, s = divmod(src, LANES)
        val = col(v) if s == 0 else jnp.where(lane < LANES - s, rolled_col(v, s), rolled_col(v + 1, s))
        wm_ref[:, j * LANES:(j + 1) * LANES] = (val if sc == 1.0 else val * sc).astype(BF16)
    g = jnp.zeros(wg_ref.shape, F32)
    dst = 0
    for src, n in gates:
        v, s = divmod(src, LANES)
        assert s == dst, "gate group must already sit at its destination lane"
        g = jnp.where(jnp.logical_and(lane >= dst, lane < dst + n), col(v), g)
        dst += n
    wg_ref[...] = g.astype(BF16)


def _win_call(w_in):
    depth, d, n_in = w_in.shape
    slabs, gates = _win_plan(d)
    n_full = n_in // LANES
    assert n_in % LANES != 0 and sum(n for _, n in gates) <= GATE_W
    tr = 256
    nm = len(slabs) * LANES
    return pl.pallas_call(
        functools.partial(_win_kernel, slabs=slabs, gates=gates, n_full=n_full),
        out_shape=(jax.ShapeDtypeStruct((depth, d, nm), BF16), jax.ShapeDtypeStruct((depth, d, GATE_W), BF16)),
        grid=(depth, d // tr),
        in_specs=[pl.BlockSpec((None, tr, n_in), lambda l, r: (l, r, 0)),
                  pl.BlockSpec((None, tr, LANES), lambda l, r: (l, r, n_full))],
        out_specs=(pl.BlockSpec((None, tr, nm), lambda l, r: (l, r, 0)),
                   pl.BlockSpec((None, tr, GATE_W), lambda l, r: (l, r, 0))),
        compiler_params=_cparams(("parallel", "parallel"), 40),
        name="w_in_relayout",
    )(w_in, w_in)


def _up_kernel(x_ref, mods_ref, wa_ref, wg_ref, o_ref, h_ref, *, nsub, tb):
    @pl.when(pl.program_id(1) == 0)
    def _():
        _ln_mod_to(h_ref, x_ref, mods_ref, nsub, tb, 3, 4)

    a = _dot(h_ref[...], wa_ref[...])
    g = _dot(h_ref[...], wg_ref[...])
    o_ref[...] = (_silu(a) * g).astype(BF16)


def _up_call(xs, mods, mods_off, w_up, layer, tm, tb):
    t, d = xs.shape
    dff = w_up.shape[2] // 2
    tn = 512
    nj = dff // tn
    nsub = tm // tb
    return pl.pallas_call(
        functools.partial(_up_kernel, nsub=nsub, tb=tb),
        out_shape=jax.ShapeDtypeStruct((t, dff), BF16),
        grid=(t // tm, nj),
        in_specs=[pl.BlockSpec((tm, d), lambda i, j: (i, 0)),
                  pl.BlockSpec((nsub, 8, d), lambda i, j: (i + mods_off, 0, 0)),
                  pl.BlockSpec((None, d, tn), lambda i, j: (layer, 0, j)),
                  pl.BlockSpec((None, d, tn), lambda i, j: (layer, 0, j + nj))],
        out_specs=pl.BlockSpec((tm, tn), lambda i, j: (i, j)),
        scratch_shapes=[pltpu.VMEM((tm, d), BF16)],
        compiler_params=_cparams(("parallel", "arbitrary"), 48),
        name="ln_ffn_up",
    )(xs, mods, w_up, w_up)


def _res_kernel(a_ref, w_ref, x_ref, mods_ref, pg_ref, pb_ref, o_ref, *, nsub, tb, g_row, alpha):
    chunk = tb // 2
    for r in range(nsub * 2):
        rows = slice(r * chunk, (r + 1) * chunk)
        y = _dot(a_ref[rows, :], w_ref[...])
        z = alpha * x_ref[rows, :] + mods_ref[r // 2, g_row:g_row + 1, :] * y
        o_ref[rows, :] = _ln(z) * pg_ref[...] + pb_ref[...]


def _res_call(act, w, layer, xs, mods, mods_off, pg, pb, tm, tb, g_row, alpha, name):
    t, d = xs.shape
    ka = act.shape[1]
    nsub = tm // tb
    return pl.pallas_call(
        functools.partial(_res_kernel, nsub=nsub, tb=tb, g_row=g_row, alpha=alpha),
        out_shape=jax.ShapeDtypeStruct((t, d), F32),
        grid=(t // tm,),
        in_specs=[pl.BlockSpec((tm, ka), lambda i: (i, 0)),
                  pl.BlockSpec((None, ka, d), lambda i: (layer, 0, 0), pipeline_mode=pl.Buffered(1)),
                  pl.BlockSpec((tm, d), lambda i: (i, 0)),
                  pl.BlockSpec((nsub, 8, d), lambda i: (i + mods_off, 0, 0)),
                  pl.BlockSpec((1, d), lambda i: (0, 0)),
                  pl.BlockSpec((1, d), lambda i: (0, 0))],
        out_specs=pl.BlockSpec((tm, d), lambda i: (i, 0)),
        compiler_params=_cparams(("parallel",), 52),
        name=name,
    )(act, w, xs, mods, pg, pb)


def _head_norm(y, center):
    if center:
        y = y - jnp.mean(y, axis=-1, keepdims=True)
    return y * lax.rsqrt(jnp.mean(y * y, axis=-1, keepdims=True) + EPS)


def _mix_finalize(act_ref, rows, ys_ref, mo_ref, sz_ref, rg_ref, gg_ref, nrm_ref):
    gw = 4 * LANES
    for h in range(M_HEADS):
        hc = slice(h * LANES, (h + 1) * LANES)
        yn = _head_norm(ys_ref[rows, hc].astype(F32), True)
        act_ref[rows, hc] = (_sigmoid(mo_ref[rows, hc].astype(F32)) * (yn * nrm_ref[0:1, hc])).astype(BF16)
    halves = [slice(g * (gw // S_G), (g + 1) * (gw // S_G)) for g in range(S_G)]
    ys = [ys_ref[rows, gw + gc.start:gw + gc.stop].astype(F32) * _silu(sz_ref[rows, gc].astype(F32)) for gc in halves]
    inv = lax.rsqrt(sum(jnp.sum(y * y, axis=-1, keepdims=True) for y in ys) * (1.0 / gw) + EPS)
    for y, gc in zip(ys, halves):
        act_ref[rows, gw + gc.start:gw + gc.stop] = (y * inv * nrm_ref[1:2, gc]).astype(BF16)
    for h in range(R_HEADS):
        hc = slice(h * LANES, (h + 1) * LANES)
        yn = _head_norm(ys_ref[rows, 2 * gw + h * LANES:2 * gw + (h + 1) * LANES].astype(F32), True)
        act_ref[rows, 2 * gw + h * LANES:2 * gw + (h + 1) * LANES] = (
            yn * nrm_ref[2:3, hc] * _silu(rg_ref[rows, hc].astype(F32))).astype(BF16)
    for h in range(G_HEADS):
        hc = slice(h * LANES, (h + 1) * LANES)
        yn = _head_norm(ys_ref[rows, 3 * gw + h * LANES:3 * gw + (h + 1) * LANES].astype(F32), False)
        act_ref[rows, 3 * gw + h * LANES:3 * gw + (h + 1) * LANES] = (
            yn * nrm_ref[3:4, hc] * _silu(gg_ref[rows, hc].astype(F32))).astype(BF16)


def _out_kernel(*refs, n_src, lead_tiles, skip, nsub, tb, alpha):
    ys_ref, mo_ref, sz_ref, rg_ref, gg_ref, nrm_ref, w_ref = refs[:7]
    x_refs = refs[7:7 + n_src]
    mods_ref, pg_ref, pb_ref, o_ref, act_ref = refs[7 + n_src:]
    for r in range(nsub):
        rows = slice(r * tb, (r + 1) * tb)
        _mix_finalize(act_ref, rows, ys_ref, mo_ref, sz_ref, rg_ref, gg_ref, nrm_ref)
        y = _dot(act_ref[rows, :], w_ref[...])
        x = _stream_rows(x_refs, lead_tiles, pl.program_id(0) + skip, rows)
        z = alpha * x + mods_ref[r, 2:3, :] * y
        o_ref[rows, :] = _ln(z) * pg_ref[...] + pb_ref[...]


def _out_call(ysum, um, nrm, w, layer, srcs, mods, skip, pg, pb, tm, tb, alpha):
    t = sum(s.shape[0] for s in srcs)
    d = srcs[0].shape[1]
    gw = d // N_GROUPS
    nsub = tm // tb
    x_specs, lead_tiles = _stream_specs(srcs, tm, d, lambda i: i + skip)

    def gate(cb):
        return pl.BlockSpec((tm, gw), lambda i: (i + skip, cb))

    return pl.pallas_call(
        functools.partial(_out_kernel, n_src=len(srcs), lead_tiles=lead_tiles, skip=skip, nsub=nsub, tb=tb,
                          alpha=alpha),
        out_shape=jax.ShapeDtypeStruct((t - skip * tm, d), F32),
        grid=(t // tm - skip,),
        in_specs=[pl.BlockSpec((tm, d), lambda i: (i + skip, 0)), gate(3), gate(10), gate(9), gate(13),
                  pl.BlockSpec(nrm.shape, lambda i: (0, 0)),
                  pl.BlockSpec((None, d, d), lambda i: (layer, 0, 0), pipeline_mode=pl.Buffered(1))]
                 + x_specs
                 + [pl.BlockSpec((nsub, 8, d), lambda i: (i + skip, 0, 0)),
                    pl.BlockSpec((1, d), lambda i: (0, 0)),
                    pl.BlockSpec((1, d), lambda i: (0, 0))],
        out_specs=pl.BlockSpec((tm, d), lambda i: (i, 0)),
        scratch_shapes=[pltpu.VMEM((tm, d), BF16)],
        compiler_params=_cparams(("parallel",), 52),
        name="mix_out_proj_res_ln",
    )(ysum, um, um, um, um, nrm, w, *srcs, mods, pg, pb)


def _prep_kernel(cur_ref, prev_ref, next_ref, rqk_ref, cw_ref, cb_ref, rope_ref, o_ref, *, nctx, nblk, tb):
    t = pl.program_id(1)
    seg_start = jnp.logical_or(t == 0, t == nctx)
    seg_end = jnp.logical_or(t == nctx - 1, t == nblk - 1)
    nc = cur_ref.shape[1]
    ridx = lax.broadcasted_iota(jnp.int32, (tb, 1), 0)
    for blk in range(nc // LANES):
        cs = slice(blk * LANES, (blk + 1) * LANES)
        x = cur_ref[:, cs].astype(F32)
        prev_row = jnp.where(seg_start, 0.0, prev_ref[:, cs].astype(F32)[15:16, :])
        next_row = jnp.where(seg_end, 0.0, next_ref[:, cs].astype(F32)[0:1, :])
        xp = jnp.where(ridx == 0, prev_row, pltpu.roll(x, 1, 0))
        xn = jnp.where(ridx == tb - 1, next_row, pltpu.roll(x, tb - 1, 0))
        y = cb_ref[:, cs] + xp * cw_ref[0:1, cs] + x * cw_ref[1:2, cs] + xn * cw_ref[2:3, cs]
        o_ref[:, cs] = _silu(y).astype(BF16)

    cos = rope_ref[:, 0:LANES]
    sin = rope_ref[:, LANES:2 * LANES]
    lane = lax.broadcasted_iota(jnp.int32, (1, LANES), 1)
    first = (lane % 64) < 32
    for blk in range(rqk_ref.shape[1] // LANES):
        xh = rqk_ref[:, blk * LANES:(blk + 1) * LANES].astype(F32)
        partner = jnp.where(first, pltpu.roll(xh, 96, 1), pltpu.roll(xh, 32, 1))
        o_ref[:, nc + blk * LANES:nc + (blk + 1) * LANES] = (xh * cos + partner * sin).astype(BF16)


def _flat_block(b, nctx, nblk):
    nlat = nblk - nctx
    return lambda i, t: jnp.where(t < nctx, i * nctx + t, b * nctx + i * nlat + (t - nctx))


def _prep_call(um, conv_w, conv_b, rope, b, nctx, tb):
    t_rows = um.shape[0]
    nblk = t_rows // (b * tb)
    hb = tb // 16
    nc = conv_w.shape[1]
    fb = _flat_block(b, nctx, nblk)
    return pl.pallas_call(
        functools.partial(_prep_kernel, nctx=nctx, nblk=nblk, tb=tb),
        out_shape=jax.ShapeDtypeStruct((t_rows, 2 * nc), BF16),
        grid=(b, nblk),
        in_specs=[pl.BlockSpec((tb, nc), lambda i, t: (fb(i, t), 2)),
                  pl.BlockSpec((16, nc), lambda i, t: (jnp.maximum(fb(i, t) * hb - 1, 0), 2)),
                  pl.BlockSpec((16, nc), lambda i, t: (jnp.minimum((fb(i, t) + 1) * hb, t_rows // 16 - 1), 2)),
                  pl.BlockSpec((tb, nc), lambda i, t: (fb(i, t), 3)),
                  pl.BlockSpec((8, nc), lambda i, t: (0, 0)),
                  pl.BlockSpec((1, nc), lambda i, t: (0, 0)),
                  pl.BlockSpec((tb, 2 * LANES), lambda i, t: (t, 0))],
        out_specs=pl.BlockSpec((tb, 2 * nc), lambda i, t: (fb(i, t), 0)),
        compiler_params=_cparams(("parallel", "parallel"), 32),
        name="prep_conv_rope",
    )(um, um, um, um, conv_w, conv_b, rope)


def _mlstm_a(rev, L, gate_ref, bias_ref):
    lane = lax.broadcasted_iota(jnp.int32, (1, LANES), 1)
    is_f = jnp.logical_and(lane >= 8, lane < 16)
    g = gate_ref[...] + bias_ref[0:1, :]
    gp = jnp.where(is_f, _log_sigmoid(g), g)
    return gp, _tri_cumsum(_causal(L, rev).astype(BF16), gp)


def _mlstm_b(qkvo_ref, caug_ref):
    gw = 4 * LANES
    out = []
    for h in range(M_HEADS):
        q = qkvo_ref[:, h * LANES:(h + 1) * LANES]
        k = qkvo_ref[:, gw + h * LANES:gw + (h + 1) * LANES]
        out.append((_dot_nt(q, k), _dot(q, caug_ref[h].astype(BF16))))
    return out


def _mlstm_c(rev, final, L, pa, pb, qkvo_ref, caug_ref, m_ref, o_ref, oc, yb_ref):
    d = 1 if rev else 0
    gw = 4 * LANES
    gp, cum = pa
    mask = _causal(L, rev)
    e_idx = 0 if rev else L - 1
    ones = jnp.ones((L, LANES), BF16)
    r_all = pltpu.roll(gp, 2 * M_HEADS, 1) - cum
    r_all_t = r_all.T
    for h in range(M_HEADS):
        cf = 8 + 4 * d + h
        b_col = cum[:, cf:cf + 1]
        r_col = r_all[:, cf:cf + 1]
        r_row = r_all_t[cf:cf + 1, :]
        m_st = m_ref[h, 0:1, 0:1]
        rel = jnp.where(mask, r_row, NEG)
        a_col = jnp.maximum(m_st, jnp.max(rel, axis=1, keepdims=True))
        m_i = b_col + a_col
        w = jnp.exp(rel - a_col)
        w_inter = jnp.exp(m_st - a_col)
        b_end = cum[e_idx:e_idx + 1, cf:cf + 1]
        m_new = jnp.maximum(b_end + m_st, b_end + jnp.max(r_row, axis=1, keepdims=True))
        ws_col = jnp.exp(b_end + r_col - m_new)
        decay = jnp.exp(b_end + m_st - m_new)
        k = qkvo_ref[:, gw + h * LANES:gw + (h + 1) * LANES]
        v = qkvo_ref[:, 2 * gw + h * LANES:2 * gw + (h + 1) * LANES]
        vaug = jnp.concatenate([v, ones], axis=1)
        qk, qc = pb[h]
        s = (qk * w).astype(BF16)
        res = _dot(s, vaug) + w_inter * qc
        kws = (k.astype(F32) * ws_col).astype(BF16)
        caug_ref[h] = decay * caug_ref[h] + _dot_tn(kws, vaug)
        m_ref[h] = jnp.broadcast_to(m_new, (8, LANES))
        hh = res[:, 0:LANES] / jnp.maximum(jnp.abs(res[:, LANES:2 * LANES]), jnp.exp(-m_i))
        cols = slice(oc + h * LANES, oc + (h + 1) * LANES)
        if final:
            o_ref[:, cols] = (hh + yb_ref[:, cols]).astype(BF16)
        else:
            o_ref[:, cols] = hh


def _ssd_a(rev, L, gate_ref, prm_ref):
    dt_all = _softplus(gate_ref[...] + prm_ref[0:1, :])
    return dt_all, _tri_cumsum(_causal(L, rev).astype(BF16), dt_all * (-jnp.exp(prm_ref[1:2, :])))


def _ssd_b(xbc_ref, st_ref):
    gw = 4 * LANES
    out = []
    for g in range(S_G):
        bg = xbc_ref[:, gw + g * S_N:gw + (g + 1) * S_N]
        cg = xbc_ref[:, gw + S_G * S_N + g * S_N:gw + S_G * S_N + (g + 1) * S_N]
        out.append((_dot_nt(cg, bg), _dot(cg, st_ref[g].astype(BF16))))
    return out


def _ssd_c(rev, final, L, pa, pb, xbc_ref, st_ref, o_ref, oc, yb_ref, dsk_ref):
    d = 1 if rev else 0
    gw = 4 * LANES
    hpg = S_HEADS // S_G
    gcols = gw // S_G
    dt_all, cum = pa
    mask = _causal(L, rev)
    e_idx = 0 if rev else L - 1
    lane_head = lax.broadcasted_iota(jnp.int32, (1, gcols), 1) // (gcols // hpg)
    r_t = (cum - jnp.log(dt_all)).T
    for g in range(S_G):
        wts, e_cols, dec_cols, decays = [], [], [], []
        for hl in range(hpg):
            c = 16 + S_HEADS * d + g * hpg + hl
            b_col = cum[:, c:c + 1]
            b_end = cum[e_idx:e_idx + 1, c:c + 1]
            wts.append(jnp.exp(jnp.where(mask, b_col - r_t[c:c + 1, :], NEG)))
            e_cols.append(jnp.exp(b_col))
            dec_cols.append(dt_all[:, c:c + 1] * jnp.exp(b_end - b_col))
            decays.append(jnp.exp(b_end))
        gc = slice(g * gcols, (g + 1) * gcols)
        xs_g = xbc_ref[:, gc].astype(F32)
        bg = xbc_ref[:, gw + g * S_N:gw + (g + 1) * S_N]
        gm, inter = pb[g]
        acc = _expand4(e_cols, lane_head) * inter
        for hl in range(hpg):
            xh = jnp.where(lane_head == hl, xs_g, 0.0).astype(BF16)
            acc = acc + _dot((gm * wts[hl]).astype(BF16), xh)
        xdec = (xs_g * _expand4(dec_cols, lane_head)).astype(BF16)
        st_ref[g] = _expand4(decays, lane_head) * st_ref[g] + _dot_tn(bg, xdec)
        cols = slice(oc + g * gcols, oc + (g + 1) * gcols)
        if final:
            o_ref[:, cols] = (acc + yb_ref[:, cols] + dsk_ref[0:1, gc] * xs_g).astype(BF16)
        else:
            o_ref[:, cols] = acc


def _ret_b(qk_ref, qc, st_ref):
    gw = 4 * LANES
    out = []
    for h in range(R_HEADS):
        q = qk_ref[:, qc + h * LANES:qc + (h + 1) * LANES]
        k = qk_ref[:, qc + gw + h * LANES:qc + gw + (h + 1) * LANES]
        out.append((_dot_nt(q, k), _dot(q, st_ref[h].astype(BF16))))
    return out


def _ret_tables(rev, L, prm_ref, w_ref, e_ref, dec_ref):
    d = 1 if rev else 0
    mask = _causal(L, rev)
    icol = lax.broadcasted_iota(jnp.int32, (L, 1), 0).astype(F32)
    row = lax.broadcasted_iota(jnp.int32, (L, L), 0)
    col = lax.broadcasted_iota(jnp.int32, (L, L), 1)
    dist = ((col - row) if rev else (row - col)).astype(F32)
    lg_all = -jnp.exp(prm_ref[d:d + 1, :])
    for h in range(R_HEADS):
        lg = lg_all[:, h:h + 1]
        w_ref[h] = jnp.exp(jnp.where(mask, dist * lg, NEG))
        if rev:
            e_col = jnp.exp((float(L) - icol) * lg)
            dec_col = jnp.exp(icol * lg)
        else:
            e_col = jnp.exp((icol + 1.0) * lg)
            dec_col = jnp.exp((float(L - 1) - icol) * lg)
        e_ref[h] = jnp.broadcast_to(e_col, (L, LANES))
        dec_ref[h] = jnp.broadcast_to(dec_col, (L, LANES))


def _ret_c(rev, final, L, pb, qk_ref, qc, v_ref, prm_ref, tables, st_ref, o_ref, oc, yb_ref):
    d = 1 if rev else 0
    gw = 4 * LANES
    w_ref, e_ref, dec_ref = tables
    lg_all = -jnp.exp(prm_ref[d:d + 1, :])
    for h in range(R_HEADS):
        decay = jnp.exp(float(L) * lg_all[:, h:h + 1])
        hc = slice(h * LANES, (h + 1) * LANES)
        k = qk_ref[:, qc + gw + h * LANES:qc + gw + (h + 1) * LANES]
        v = v_ref[:, hc]
        qk, qst = pb[h]
        y = _dot((qk * w_ref[h]).astype(BF16), v) + e_ref[h] * qst
        kd = (k.astype(F32) * dec_ref[h]).astype(BF16)
        st_ref[h] = decay * st_ref[h] + _dot_tn(kd, v)
        cols = slice(oc + h * LANES, oc + (h + 1) * LANES)
        if final:
            o_ref[:, cols] = (y + yb_ref[:, cols]).astype(BF16)
        else:
            o_ref[:, cols] = y


def _gla_a(rev, L, tb, gate_ref, gw_ref, gb_ref):
    row = lax.broadcasted_iota(jnp.int32, (tb, tb), 0)
    col = lax.broadcasted_iota(jnp.int32, (tb, tb), 1)
    blockmask = jnp.logical_and((row // L) == (col // L), (col >= row) if rev else (col <= row))
    z = _dot_f32(gate_ref[...], gw_ref[...]) + gb_ref[...]
    return blockmask, _tri_cumsum(blockmask.astype(BF16), _log_sigmoid(z) * (1.0 / G_TAU))


def _gla_c(rev, final, L, nsub, pa, qk_ref, v_ref, st_ref, o_ref, oc, yb_ref):
    kw = 2 * LANES
    blockmask, cum = pa
    e_idx = 0 if rev else L - 1
    lane_head = lax.broadcasted_iota(jnp.int32, (1, kw), 1) // (kw // G_HEADS)
    sub = [slice(c * L, (c + 1) * L) for c in range(nsub)]
    qf = qk_ref[:, 0:kw].astype(F32)
    kf = qk_ref[:, kw:2 * kw].astype(F32)
    qg = qf * jnp.exp(cum)
    kg = (kf * jnp.exp(-cum)).astype(BF16)
    b_ends = [cum[c * L + e_idx:c * L + e_idx + 1, :] for c in range(nsub)]
    kd = jnp.concatenate([kf[sub[c], :] * jnp.exp(b_ends[c] - cum[sub[c], :]) for c in range(nsub)], axis=0)
    qgh = [jnp.where(lane_head == h, qg, 0.0).astype(BF16) for h in range(G_HEADS)]
    kdh = [jnp.where(lane_head == h, kd, 0.0).astype(BF16) for h in range(G_HEADS)]
    s_all = [_dot_nt(qgh[h], kg) for h in range(G_HEADS)]
    upd = [sum(_dot_tn(v_ref[sub[c], h * LANES:(h + 1) * LANES], kdh[h][sub[c], :]) for h in range(G_HEADS))
           for c in range(nsub)]
    st = st_ref[...]
    st_at = [None] * nsub
    for c in (reversed(range(nsub)) if rev else range(nsub)):
        st_at[c] = st.astype(BF16)
        st = jnp.exp(b_ends[c]) * st + upd[c]
    st_ref[...] = st
    for h in range(G_HEADS):
        hc = slice(h * LANES, (h + 1) * LANES)
        cols = slice(oc + h * LANES, oc + (h + 1) * LANES)
        inter = jnp.concatenate([_dot_nt(qgh[h][sub[c], :], st_at[c]) for c in range(nsub)], axis=0)
        y = _dot(jnp.where(blockmask, s_all[h], 0.0).astype(BF16), v_ref[:, hc]) + inter
        if final:
            o_ref[:, cols] = (y + yb_ref[:, cols]).astype(BF16)
        else:
            o_ref[:, cols] = y


def _scan_kernel(*refs, rev, final, tb):
    (um_m, ug, prep, r_v, g_qk, g_v, m_bias, s_prm, r_prm, gwp, gb) = refs[:11]
    if final:
        yb, dsk = refs[11:13]
        o_ref = refs[13]
        scratch = refs[14:]
    else:
        yb = dsk = None
        o_ref = refs[11]
        scratch = refs[12:]
    caug, m_st, s_st, r_st, g_st = scratch[:5]
    r_tables = scratch[5:]
    gw = 4 * LANES

    @pl.when(pl.program_id(1) == 0)
    def _():
        for ref in scratch[:5]:
            ref[...] = jnp.zeros_like(ref)
        _ret_tables(rev, tb, r_prm, *r_tables)

    m_a = _mlstm_a(rev, tb, ug, m_bias)
    s_a = _ssd_a(rev, tb, ug, s_prm)
    g_a = _gla_a(rev, GLA_CHUNK, tb, ug, gwp, gb)
    r_b = _ret_b(prep, 2 * gw, r_st)
    m_b = _mlstm_b(um_m, caug)
    s_b = _ssd_b(prep, s_st)
    _ret_c(rev, final, tb, r_b, prep, 2 * gw, r_v, r_prm, r_tables, r_st, o_ref, 2 * gw, yb)
    _gla_c(rev, final, GLA_CHUNK, tb // GLA_CHUNK, g_a, g_qk, g_v, g_st, o_ref, 3 * gw, yb)
    _mlstm_c(rev, final, tb, m_a, m_b, um_m, caug, m_st, o_ref, 0, yb)
    _ssd_c(rev, final, tb, s_a, s_b, prep, s_st, o_ref, gw, yb, dsk)


def _scan_call(rev, final, b, tb, nctx, um, ug, prep, params, yb, dsk):
    t_rows = um.shape[0]
    gw = 4 * LANES
    nblk = t_rows // (b * tb)
    fb = _flat_block(b, nctx, nblk)
    if rev:
        def order(c):
            return jnp.where(c < nctx, nctx - 1 - c, nblk + nctx - 1 - c)
    else:
        def order(c):
            return c

    def tok(width, cb):
        return pl.BlockSpec((tb, width), lambda i, c: (fb(i, order(c)), cb))

    def full(arr):
        return pl.BlockSpec(arr.shape, lambda i, c: (0,) * arr.ndim)

    ins = [um, ug, prep, um, um, um] + list(params)
    in_specs = [tok(4 * gw, 0), tok(GATE_W, 0), tok(4 * gw, 0), tok(gw, 8), tok(gw, 11), tok(gw, 12)]
    in_specs += [full(p) for p in params]
    if final:
        ins += [yb, dsk]
        in_specs += [tok(4 * gw, 0), full(dsk)]
    scratch = [pltpu.VMEM((M_HEADS, LANES, 2 * LANES), F32), pltpu.VMEM((M_HEADS, 8, LANES), F32),
               pltpu.VMEM((S_G, S_N, gw // S_G), F32), pltpu.VMEM((R_HEADS, LANES, LANES), F32),
               pltpu.VMEM((LANES, 2 * LANES), F32),
               pltpu.VMEM((R_HEADS, tb, tb), F32), pltpu.VMEM((R_HEADS, tb, LANES), F32),
               pltpu.VMEM((R_HEADS, tb, LANES), F32)]
    return pl.pallas_call(
        functools.partial(_scan_kernel, rev=rev, final=final, tb=tb),
        out_shape=jax.ShapeDtypeStruct((t_rows, 4 * gw), BF16 if final else F32),
        grid=(b, nblk),
        in_specs=in_specs,
        out_specs=tok(4 * gw, 0),
        scratch_shapes=scratch,
        compiler_params=_cparams(("parallel", "arbitrary"), 48),
        name="scan_fwd" if final else "scan_bwd",
    )(*ins)


def _in_col_layout(d_model):
    gw = d_model // N_GROUPS
    conv_ch = gw + 2 * S_G * S_N
    names = [('m_q', gw), ('m_k', gw), ('m_v', gw), ('m_o', gw), ('m_i', 2 * M_HEADS), ('m_f', 2 * M_HEADS),
             ('s_z', gw), ('s_xbc', conv_ch), ('s_dt', 2 * S_HEADS),
             ('r_q', gw), ('r_k', gw), ('r_v', gw), ('r_g', gw),
             ('g_q', gw // 2), ('g_k', gw // 2), ('g_v', gw), ('g_g', gw), ('g_a', 2 * G_RANK)]
    off, o = {}, 0
    for nm, n in names:
        off[nm] = (o, n)
        o += n
    return off


def kernel(x, c, ctx, c_ctx, ada_w, ada_b, w_in, m_ig_b, m_fg_b, m_norm, s_conv_w, s_conv_b, s_dt_bias, s_a_log, s_d, s_norm, r_decay, r_norm, g_gate_w, g_gate_b, g_norm, w_out, post_g, post_b, ffn_w_up, ffn_w_down):
    b, seq, d = x.shape
    n_ctx_tok = ctx.shape[1]
    depth = ada_w.shape[0]
    gw = d // N_GROUPS
    assert gw == 4 * LANES and b + 1 <= 8
    s_tot = n_ctx_tok + seq
    t = b * s_tot
    tb = math.gcd(MAX_TOKEN_BLOCK, math.gcd(n_ctx_tok, seq))
    assert tb % 16 == 0 and tb % GLA_CHUNK == 0
    nctx = n_ctx_tok // tb
    nblk = s_tot // tb
    nsb = t // tb
    t_ctx = b * n_ctx_tok
    t_lat = b * seq

    def row_tile(cap):
        return max(m for m in (1024, 512, 256, 128, 64, 32, 16)
                   if m <= cap and t_ctx % m == 0 and t_lat % m == 0 and m % tb == 0)

    tm = row_tile(1024)
    tm_out = row_tile(512)
    alpha = (2.0 * depth) ** 0.25

    pos = jnp.arange(seq)
    quarter = LANES // 4
    freqs = 1.0 / (ROPE_BASE ** (jnp.arange(quarter, dtype=F32) / quarter))
    ang_r = (pos // GRID_W).astype(F32)[:, None] * freqs[None, :]
    ang_c = (pos % GRID_W).astype(F32)[:, None] * freqs[None, :]
    cos_t = jnp.concatenate([jnp.cos(ang_r)] * 2 + [jnp.cos(ang_c)] * 2, axis=1)
    sin_t = jnp.concatenate([-jnp.sin(ang_r), jnp.sin(ang_r), -jnp.sin(ang_c), jnp.sin(ang_c)], axis=1)
    rope = jnp.concatenate([
        jnp.concatenate([jnp.ones((n_ctx_tok, LANES), F32), jnp.zeros((n_ctx_tok, LANES), F32)], axis=1),
        jnp.concatenate([cos_t, sin_t], axis=1)], axis=0)

    xs = (ctx.reshape(t_ctx, d), x.reshape(t_lat, d))
    cc =jnp.concatenate([c, c_ctx[None, :], jnp.zeros((8 - b - 1, d), F32)], axis=0)
    mod = _mod_call(cc, ada_w, ada_b)
    sb = np.arange(nsb)
    mod_row = np.where(sb < b * nctx, b, (sb - b * nctx) // (nblk - nctx))

    def lane_rows(vecs, start, n_rows):
        rows = jnp.stack(vecs)
        return jnp.pad(rows, ((0, n_rows - rows.shape[0]), (start, LANES - start - rows.shape[1])))

    wm, wg = _win_call(w_in)
    w_out_b = w_out.astype(BF16)
    w_up_b = ffn_w_up.astype(BF16)
    w_down_b = ffn_w_down.astype(BF16)
    for l in range(depth):
        mods = jnp.pad(mod[l][mod_row].reshape(nsb, 6, d), ((0, 0), (0, 2), (0, 0)))
        um, ug = _in_call(xs, mods, wm, wg, l, tm, tb)

        conv_w = jnp.pad(s_conv_w[l], ((0, 8 - CONV_W), (0, 0)))
        prep = _prep_call(um, conv_w, s_conv_b[l][None, :], rope, b, nctx, tb)

        m_bias = lane_rows([jnp.concatenate([m_ig_b[l].reshape(-1), m_fg_b[l].reshape(-1)])], 0, 1)
        s_prm = lane_rows([s_dt_bias[l].reshape(-1), s_a_log[l].reshape(-1)], 16, 8)
        r_prm = lane_rows([r_decay[l][0], r_decay[l][1]], 0, 8)
        nrm = jnp.concatenate([jnp.stack([m_norm[l], s_norm[l], r_norm[l], g_norm[l]]), jnp.zeros((4, gw), F32)])
        dsk = jnp.repeat(s_d[l], gw // S_HEADS)[None, :]

        def dir_params(dd):
            lo = 32 + G_RANK * dd
            gwp = jnp.pad(g_gate_w[l][dd], ((lo, GATE_W - lo - G_RANK), (0, 0)))
            return [m_bias, s_prm, r_prm, gwp, g_gate_b[l][dd][None, :]]

        yb = _scan_call(True, False, b, tb, nctx, um, ug, prep, dir_params(1), None, None)
        ysum = _scan_call(False, True, b, tb, nctx, um, ug, prep, dir_params(0), yb, dsk)

        skip = t_ctx if l == depth - 1 else 0
        x1 = _out_call(ysum, um, nrm, w_out_b, l, xs, mods, skip // tm_out,
                       post_g[l, 0][None, :], post_b[l, 0][None, :], tm_out, tb, alpha)
        act = _up_call(x1, mods, skip // tm, w_up_b, l, tm, tb)
        xs = (_res_call(act, w_down_b, l, x1, mods, skip // tb,
                        post_g[l, 1][None, :], post_b[l, 1][None, :], tb, tb, 5, alpha, "ffn_down_res_ln"),)

    return xs[0].reshape(b, seq, d)
```

```python
import functools
import math

import numpy as np
import jax
import jax.numpy as jnp
from jax import lax
from jax.experimental import pallas as pl
from jax.experimental.pallas import tpu as pltpu

F32 = jnp.float32
BF16 = jnp.bfloat16

EPS = 1e-5
ROPE_BASE = 10000.0
GRID_W = 64
N_GROUPS = 4
M_HEADS = 4
S_HEADS = 8
S_G = 2
S_N = 128
CONV_W = 3
R_HEADS = 4
G_HEADS = 4
G_RANK = 16
G_TAU = 16.0
NEG = -1e30

LANES = 128
GATE_W = LANES
MAX_TOKEN_BLOCK = 256
GLA_CHUNK = 64


def _cparams(sem, vmem_mb):
    return pltpu.CompilerParams(dimension_semantics=sem, vmem_limit_bytes=vmem_mb * 1024 * 1024)


def _dot(a, b):
    return jnp.dot(a, b, preferred_element_type=F32)


def _dot_nt(a, b):
    return lax.dot_general(a, b, (((1,), (1,)), ((), ())), preferred_element_type=F32)


def _dot_tn(a, b):
    return lax.dot_general(a, b, (((0,), (0,)), ((), ())), preferred_element_type=F32)


def _split3(x):
    hi = x.astype(BF16)
    r1 = x - hi.astype(F32)
    mid = r1.astype(BF16)
    lo = (r1 - mid.astype(F32)).astype(BF16)
    return hi, mid, lo


def _tri_cumsum(tri, x):
    hi, mid, lo = _split3(x)
    return _dot(tri, hi) + _dot(tri, mid) + _dot(tri, lo)


def _dot_f32(a, b):
    ah = a.astype(BF16)
    al = (a - ah.astype(F32)).astype(BF16)
    bh = b.astype(BF16)
    bl = (b - bh.astype(F32)).astype(BF16)
    return _dot(ah, bh) + _dot(ah, bl) + _dot(al, bh)


def _sigmoid(x):
    return 1.0 / (1.0 + jnp.exp(-x))


def _silu(x):
    return x * _sigmoid(x)


def _softplus(x):
    return jnp.maximum(x, 0.0) + jnp.log1p(jnp.exp(-jnp.abs(x)))


def _log_sigmoid(x):
    return jnp.minimum(x, 0.0) - jnp.log1p(jnp.exp(-jnp.abs(x)))


def _ln(x, eps=EPS):
    mu = jnp.mean(x, axis=-1, keepdims=True)
    xc = x - mu
    var = jnp.mean(xc * xc, axis=-1, keepdims=True)
    return xc * lax.rsqrt(var + eps)


def _res_ln(x, gate_over_alpha, y, alpha):
    return _ln(x + gate_over_alpha * y, EPS / (alpha * alpha))


def _causal(n, rev):
    row = lax.broadcasted_iota(jnp.int32, (n, n), 0)
    col = lax.broadcasted_iota(jnp.int32, (n, n), 1)
    return (col >= row) if rev else (col <= row)


def _expand4(cols, lane_head):
    return jnp.where(lane_head == 0, cols[0],
                     jnp.where(lane_head == 1, cols[1],
                               jnp.where(lane_head == 2, cols[2], cols[3])))


MOD_SPLIT = 4


def _mod_kernel(c_ref, *refs):
    w_refs = refs[:MOD_SPLIT]
    b_ref, o_ref = refs[MOD_SPLIT:]
    s = _silu(c_ref[...]).astype(BF16)
    wn = w_refs[0].shape[2]
    for q, w_ref in enumerate(w_refs):
        cols = slice(q * wn, (q + 1) * wn)
        part = _dot(s, w_ref[0].astype(BF16))

        @pl.when(pl.program_id(1) == 0)
        def _():
            o_ref[0, :, cols] = part + b_ref[0, :, cols]

        @pl.when(pl.program_id(1) > 0)
        def _():
            o_ref[0, :, cols] += part


def _mod_call(cc, ada_w, ada_b):
    depth, d, n = ada_w.shape
    tk = 256
    wn = n // MOD_SPLIT
    w_specs = [pl.BlockSpec((1, tk, wn), functools.partial(lambda l, k, q: (l, k, q), q=q))
               for q in range(MOD_SPLIT)]
    return pl.pallas_call(
        _mod_kernel,
        out_shape=jax.ShapeDtypeStruct((depth, 8, n), F32),
        grid=(depth, d // tk),
        in_specs=[pl.BlockSpec((8, tk), lambda l, k: (0, k))] + w_specs
                 + [pl.BlockSpec((1, 1, n), lambda l, k: (l, 0, 0))],
        out_specs=pl.BlockSpec((1, 8, n), lambda l, k: (l, 0, 0)),
        compiler_params=_cparams(("parallel", "arbitrary"), 40),
        name="adaln_mod",
    )(cc, *([ada_w] * MOD_SPLIT), ada_b.reshape(depth, 1, n))


def _stream_specs(srcs, tm, d, row_tile):
    if len(srcs) == 1:
        return [pl.BlockSpec((tm, d), lambda *g: (row_tile(*g), 0))], 0
    lead_tiles = srcs[0].shape[0] // tm
    return [pl.BlockSpec((tm, d), lambda *g: (jnp.minimum(row_tile(*g), lead_tiles - 1), 0),
                         pipeline_mode=pl.Buffered(1)),
            pl.BlockSpec((tm, d), lambda *g: (jnp.maximum(row_tile(*g) - lead_tiles, 0), 0))], lead_tiles


def _stream_rows(x_refs, lead_tiles, tile, rows):
    if len(x_refs) == 1:
        return x_refs[0][rows, :]
    return jnp.where(tile < lead_tiles, x_refs[0][rows, :], x_refs[1][rows, :])


def _ln_mod_to(h_ref, x_ref, mods_ref, nsub, tb, sh_row, sc_row):
    for r in range(nsub):
        x = x_ref[r * tb:(r + 1) * tb, :]
        h = _ln(x) * (1.0 + mods_ref[r, sc_row:sc_row + 1, :]) + mods_ref[r, sh_row:sh_row + 1, :]
        h_ref[r * tb:(r + 1) * tb, :] = h.astype(BF16)


def _in_kernel(*refs, n_src, lead_tiles, nsub, tb):
    x_refs = refs[:n_src]
    mods_ref, wm_ref, wg_ref, um_ref, ug_ref, h_ref = refs[n_src:]

    @pl.when(pl.program_id(1) == 0)
    def _():
        for r in range(nsub):
            rows = slice(r * tb, (r + 1) * tb)
            x = _stream_rows(x_refs, lead_tiles, pl.program_id(0), rows)
            h = _ln(x) * (1.0 + mods_ref[r, 1:2, :]) + mods_ref[r, 0:1, :]
            h_ref[rows, :] = h.astype(BF16)
        ug_ref[...] = _dot(h_ref[...], wg_ref[...])

    um_ref[...] = _dot(h_ref[...], wm_ref[...]).astype(BF16)


def _in_call(srcs, mods, wm, wg, layer, tm, tb):
    t = sum(s.shape[0] for s in srcs)
    d = srcs[0].shape[1]
    n = wm.shape[2]
    tn = 1024
    nsub = tm // tb
    x_specs, lead_tiles = _stream_specs(srcs, tm, d, lambda i, j: i)
    return pl.pallas_call(
        functools.partial(_in_kernel, n_src=len(srcs), lead_tiles=lead_tiles, nsub=nsub, tb=tb),
        out_shape=(jax.ShapeDtypeStruct((t, n), BF16), jax.ShapeDtypeStruct((t, GATE_W), F32)),
        grid=(t // tm, n // tn),
        in_specs=x_specs + [pl.BlockSpec((nsub, 8, d), lambda i, j: (i, 0, 0)),
                            pl.BlockSpec((None, d, tn), lambda i, j: (layer, 0, j)),
                            pl.BlockSpec((None, d, GATE_W), lambda i, j: (layer, 0, 0))],
        out_specs=(pl.BlockSpec((tm, tn), lambda i, j: (i, j)),
                   pl.BlockSpec((tm, GATE_W), lambda i, j: (i, 0))),
        scratch_shapes=[pltpu.VMEM((tm, d), BF16)],
        compiler_params=_cparams(("parallel", "arbitrary"), 48 + 8 * (len(srcs) - 1)),
        name="ln_in_proj",
    )(*srcs, mods, wm, wg)


MAIN_ORDER = ('m_q', 'm_k', 'm_v', 'm_o', 's_xbc', 'r_q', 'r_k', 'r_v', 'r_g', 's_z', 'g_q', 'g_k', 'g_v', 'g_g')
GATE_ORDER = ('m_i', 'm_f', 's_dt', 'g_a')


def _win_plan(d_model):
    off = _in_col_layout(d_model)
    gw = d_model // N_GROUPS
    scale = {'m_q': float(gw // M_HEADS) ** -0.5, 'r_k': float(gw // R_HEADS) ** -0.5,
             'g_q': float(gw // 2 // G_HEADS) ** -0.5}
    slabs = [(off[name][0] + k * LANES, scale.get(name, 1.0))
             for name in MAIN_ORDER for k in range(off[name][1] // LANES)]
    return slabs, [off[name] for name in GATE_ORDER]


def _win_kernel(w_ref, tail_ref, wm_ref, wg_ref, *, slabs, gates, n_full):
    lane = lax.broadcasted_iota(jnp.int32, (1, LANES), 1)
    rolled = {}

    def col(v):
        return tail_ref[...] if v == n_full else w_ref[:, v * LANES:(v + 1) * LANES]

    def rolled_col(v, s):
        if (v, s) not in rolled:
            rolled[(v, s)] = pltpu.roll(col(v), LANES - s, 1)
        return rolled[(v, s)]

    for j, (src, sc) in enumerate(slabs):
        v, s = divmod(src, LANES)
        val = col(v) if s == 0 else jnp.where(lane < LANES - s, rolled_col(v, s), rolled_col(v + 1, s))
        wm_ref[:, j * LANES:(j + 1) * LANES] = (val if sc == 1.0 else val * sc).astype(BF16)
    g = jnp.zeros(wg_ref.shape, F32)
    dst = 0
    for src, n in gates:
        v, s = divmod(src, LANES)
        assert s == dst, "gate group must already sit at its destination lane"
        g = jnp.where(jnp.logical_and(lane >= dst, lane < dst + n), col(v), g)
        dst += n
    wg_ref[...] = g.astype(BF16)


def _win_call(w_in):
    depth, d, n_in = w_in.shape
    slabs, gates = _win_plan(d)
    n_full = n_in // LANES
    assert n_in % LANES != 0 and sum(n for _, n in gates) <= GATE_W
    tr = 256
    nm = len(slabs) * LANES
    return pl.pallas_call(
        functools.partial(_win_kernel, slabs=slabs, gates=gates, n_full=n_full),
        out_shape=(jax.ShapeDtypeStruct((depth, d, nm), BF16), jax.ShapeDtypeStruct((depth, d, GATE_W), BF16)),
        grid=(depth, d // tr),
        in_specs=[pl.BlockSpec((None, tr, n_in), lambda l, r: (l, r, 0)),
                  pl.BlockSpec((None, tr, LANES), lambda l, r: (l, r, n_full))],
        out_specs=(pl.BlockSpec((None, tr, nm), lambda l, r: (l, r, 0)),
                   pl.BlockSpec((None, tr, GATE_W), lambda l, r: (l, r, 0))),
        compiler_params=_cparams(("parallel", "parallel"), 40),
        name="w_in_relayout",
    )(w_in, w_in)


def _up_kernel(x_ref, mods_ref, wa_ref, wg_ref, o_ref, h_ref, *, nsub, tb):
    @pl.when(pl.program_id(1) == 0)
    def _():
        _ln_mod_to(h_ref, x_ref, mods_ref, nsub, tb, 3, 4)

    a = _dot(h_ref[...], wa_ref[...])
    g = _dot(h_ref[...], wg_ref[...])
    o_ref[...] = (_silu(a) * g).astype(BF16)


def _up_call(xs, mods, mods_off, w_up, layer, tm, tb):
    t, d = xs.shape
    dff = w_up.shape[2] // 2
    tn = 512
    nj = dff // tn
    nsub = tm // tb
    return pl.pallas_call(
        functools.partial(_up_kernel, nsub=nsub, tb=tb),
        out_shape=jax.ShapeDtypeStruct((t, dff), BF16),
        grid=(t // tm, nj),
        in_specs=[pl.BlockSpec((tm, d), lambda i, j: (i, 0)),
                  pl.BlockSpec((nsub, 8, d), lambda i, j: (i + mods_off, 0, 0)),
                  pl.BlockSpec((None, d, tn), lambda i, j: (layer, 0, j)),
                  pl.BlockSpec((None, d, tn), lambda i, j: (layer, 0, j + nj))],
        out_specs=pl.BlockSpec((tm, tn), lambda i, j: (i, j)),
        scratch_shapes=[pltpu.VMEM((tm, d), BF16)],
        compiler_params=_cparams(("parallel", "arbitrary"), 48),
        name="ln_ffn_up",
    )(xs, mods, w_up, w_up)


def _res_kernel(a_ref, w_ref, x_ref, mods_ref, pg_ref, pb_ref, o_ref, *, nsub, tb, g_row, alpha):
    chunk = tb // 2
    for r in range(nsub * 2):
        rows = slice(r * chunk, (r + 1) * chunk)
        y = _dot(a_ref[rows, :], w_ref[...])
        zn = _res_ln(x_ref[rows, :], mods_ref[r // 2, g_row:g_row + 1, :], y, alpha)
        o_ref[rows, :] = zn * pg_ref[...] + pb_ref[...]


def _res_call(act, w, layer, xs, mods, mods_off, pg, pb, tm, tb, g_row, alpha, name):
    t, d = xs.shape
    ka = act.shape[1]
    nsub = tm // tb
    return pl.pallas_call(
        functools.partial(_res_kernel, nsub=nsub, tb=tb, g_row=g_row, alpha=alpha),
        out_shape=jax.ShapeDtypeStruct((t, d), F32),
        grid=(t // tm,),
        in_specs=[pl.BlockSpec((tm, ka), lambda i: (i, 0)),
                  pl.BlockSpec((None, ka, d), lambda i: (layer, 0, 0), pipeline_mode=pl.Buffered(1)),
                  pl.BlockSpec((tm, d), lambda i: (i, 0)),
                  pl.BlockSpec((nsub, 8, d), lambda i: (i + mods_off, 0, 0)),
                  pl.BlockSpec((1, d), lambda i: (0, 0)),
                  pl.BlockSpec((1, d), lambda i: (0, 0))],
        out_specs=pl.BlockSpec((tm, d), lambda i: (i, 0)),
        compiler_params=_cparams(("parallel",), 52),
        name=name,
    )(act, w, xs, mods, pg, pb)


def _head_norm(y, center):
    if center:
        y = y - jnp.mean(y, axis=-1, keepdims=True)
    return y * lax.rsqrt(jnp.mean(y * y, axis=-1, keepdims=True) + EPS)


def _mix_finalize(act_ref, rows, ys_ref, mo_ref, sz_ref, rg_ref, gg_ref, nrm_ref):
    gw = 4 * LANES
    for h in range(M_HEADS):
        hc = slice(h * LANES, (h + 1) * LANES)
        yn = _head_norm(ys_ref[rows, hc].astype(F32), True)
        act_ref[rows, hc] = (_sigmoid(mo_ref[rows, hc].astype(F32)) * (yn * nrm_ref[0:1, hc])).astype(BF16)
    halves = [slice(g * (gw // S_G), (g + 1) * (gw // S_G)) for g in range(S_G)]
    ys = [ys_ref[rows, gw + gc.start:gw + gc.stop].astype(F32) * _silu(sz_ref[rows, gc].astype(F32)) for gc in halves]
    inv = lax.rsqrt(sum(jnp.sum(y * y, axis=-1, keepdims=True) for y in ys) * (1.0 / gw) + EPS)
    for y, gc in zip(ys, halves):
        act_ref[rows, gw + gc.start:gw + gc.stop] = (y * inv * nrm_ref[1:2, gc]).astype(BF16)
    for h in range(R_HEADS):
        hc = slice(h * LANES, (h + 1) * LANES)
        yn = _head_norm(ys_ref[rows, 2 * gw + h * LANES:2 * gw + (h + 1) * LANES].astype(F32), True)
        act_ref[rows, 2 * gw + h * LANES:2 * gw + (h + 1) * LANES] = (
            yn * nrm_ref[2:3, hc] * _silu(rg_ref[rows, hc].astype(F32))).astype(BF16)
    for h in range(G_HEADS):
        hc = slice(h * LANES, (h + 1) * LANES)
        yn = _head_norm(ys_ref[rows, 3 * gw + h * LANES:3 * gw + (h + 1) * LANES].astype(F32), False)
        act_ref[rows, 3 * gw + h * LANES:3 * gw + (h + 1) * LANES] = (
            yn * nrm_ref[3:4, hc] * _silu(gg_ref[rows, hc].astype(F32))).astype(BF16)


def _out_kernel(*refs, n_src, lead_tiles, skip, nsub, tb, alpha):
    ys_ref, mo_ref, sz_ref, rg_ref, gg_ref, nrm_ref, w_ref = refs[:7]
    x_refs = refs[7:7 + n_src]
    mods_ref, pg_ref, pb_ref, o_ref, act_ref = refs[7 + n_src:]
    for r in range(nsub):
        rows = slice(r * tb, (r + 1) * tb)
        _mix_finalize(act_ref, rows, ys_ref, mo_ref, sz_ref, rg_ref, gg_ref, nrm_ref)
        y = _dot(act_ref[rows, :], w_ref[...])
        x = _stream_rows(x_refs, lead_tiles, pl.program_id(0) + skip, rows)
        o_ref[rows, :] = _res_ln(x, mods_ref[r, 2:3, :], y, alpha) * pg_ref[...] + pb_ref[...]


def _out_call(ysum, um, nrm, w, layer, srcs, mods, skip, pg, pb, tm, tb, alpha):
    t = sum(s.shape[0] for s in srcs)
    d = srcs[0].shape[1]
    gw = d // N_GROUPS
    nsub = tm // tb
    x_specs, lead_tiles = _stream_specs(srcs, tm, d, lambda i: i + skip)

    def gate(cb):
        return pl.BlockSpec((tm, gw), lambda i: (i + skip, cb))

    return pl.pallas_call(
        functools.partial(_out_kernel, n_src=len(srcs), lead_tiles=lead_tiles, skip=skip, nsub=nsub, tb=tb,
                          alpha=alpha),
        out_shape=jax.ShapeDtypeStruct((t - skip * tm, d), F32),
        grid=(t // tm - skip,),
        in_specs=[pl.BlockSpec((tm, d), lambda i: (i + skip, 0)), gate(3), gate(10), gate(9), gate(13),
                  pl.BlockSpec(nrm.shape, lambda i: (0, 0)),
                  pl.BlockSpec((None, d, d), lambda i: (layer, 0, 0), pipeline_mode=pl.Buffered(1))]
                 + x_specs
                 + [pl.BlockSpec((nsub, 8, d), lambda i: (i + skip, 0, 0)),
                    pl.BlockSpec((1, d), lambda i: (0, 0)),
                    pl.BlockSpec((1, d), lambda i: (0, 0))],
        out_specs=pl.BlockSpec((tm, d), lambda i: (i, 0)),
        scratch_shapes=[pltpu.VMEM((tm, d), BF16)],
        compiler_params=_cparams(("parallel",), 52),
        name="mix_out_proj_res_ln",
    )(ysum, um, um, um, um, nrm, w, *srcs, mods, pg, pb)


def _prep_kernel(cur_ref, prev_ref, next_ref, rqk_ref, cw_ref, cb_ref, rope_ref, o_ref, *, nctx, nblk, tb):
    t = pl.program_id(1)
    seg_start = jnp.logical_or(t == 0, t == nctx)
    seg_end = jnp.logical_or(t == nctx - 1, t == nblk - 1)
    nc = cur_ref.shape[1]
    ridx = lax.broadcasted_iota(jnp.int32, (tb, 1), 0)
    for blk in range(nc // LANES):
        cs = slice(blk * LANES, (blk + 1) * LANES)
        x = cur_ref[:, cs].astype(F32)
        prev_row = jnp.where(seg_start, 0.0, prev_ref[:, cs].astype(F32)[15:16, :])
        next_row = jnp.where(seg_end, 0.0, next_ref[:, cs].astype(F32)[0:1, :])
        xp = jnp.where(ridx == 0, prev_row, pltpu.roll(x, 1, 0))
        xn = jnp.where(ridx == tb - 1, next_row, pltpu.roll(x, tb - 1, 0))
        y = cb_ref[:, cs] + xp * cw_ref[0:1, cs] + x * cw_ref[1:2, cs] + xn * cw_ref[2:3, cs]
        o_ref[:, cs] = _silu(y).astype(BF16)

    cos = rope_ref[:, 0:LANES]
    sin = rope_ref[:, LANES:2 * LANES]
    lane = lax.broadcasted_iota(jnp.int32, (1, LANES), 1)
    first = (lane % 64) < 32
    for blk in range(rqk_ref.shape[1] // LANES):
        xh = rqk_ref[:, blk * LANES:(blk + 1) * LANES].astype(F32)
        partner = jnp.where(first, pltpu.roll(xh, 96, 1), pltpu.roll(xh, 32, 1))
        o_ref[:, nc + blk * LANES:nc + (blk + 1) * LANES] = (xh * cos + partner * sin).astype(BF16)


def _flat_block(b, nctx, nblk):
    nlat = nblk - nctx
    return lambda i, t: jnp.where(t < nctx, i * nctx + t, b * nctx + i * nlat + (t - nctx))


def _prep_call(um, conv_w, conv_b, rope, b, nctx, tb):
    t_rows = um.shape[0]
    nblk = t_rows // (b * tb)
    hb = tb // 16
    nc = conv_w.shape[1]
    fb = _flat_block(b, nctx, nblk)
    return pl.pallas_call(
        functools.partial(_prep_kernel, nctx=nctx, nblk=nblk, tb=tb),
        out_shape=jax.ShapeDtypeStruct((t_rows, 2 * nc), BF16),
        grid=(b, nblk),
        in_specs=[pl.BlockSpec((tb, nc), lambda i, t: (fb(i, t), 2)),
                  pl.BlockSpec((16, nc), lambda i, t: (jnp.maximum(fb(i, t) * hb - 1, 0), 2)),
                  pl.BlockSpec((16, nc), lambda i, t: (jnp.minimum((fb(i, t) + 1) * hb, t_rows // 16 - 1), 2)),
                  pl.BlockSpec((tb, nc), lambda i, t: (fb(i, t), 3)),
                  pl.BlockSpec((8, nc), lambda i, t: (0, 0)),
                  pl.BlockSpec((1, nc), lambda i, t: (0, 0)),
                  pl.BlockSpec((tb, 2 * LANES), lambda i, t: (t, 0))],
        out_specs=pl.BlockSpec((tb, 2 * nc), lambda i, t: (fb(i, t), 0)),
        compiler_params=_cparams(("parallel", "parallel"), 32),
        name="prep_conv_rope",
    )(um, um, um, um, conv_w, conv_b, rope)


def _mlstm_a(rev, L, gate_ref, bias_ref):
    lane = lax.broadcasted_iota(jnp.int32, (1, LANES), 1)
    is_f = jnp.logical_and(lane >= 8, lane < 16)
    g = gate_ref[...] + bias_ref[0:1, :]
    gp = jnp.where(is_f, _log_sigmoid(g), g)
    return gp, _tri_cumsum(_causal(L, rev).astype(BF16), gp)


def _mlstm_b(qkvo_ref, caug_ref):
    gw = 4 * LANES
    out = []
    for h in range(M_HEADS):
        q = qkvo_ref[:, h * LANES:(h + 1) * LANES]
        k = qkvo_ref[:, gw + h * LANES:gw + (h + 1) * LANES]
        out.append((_dot_nt(q, k), _dot(q, caug_ref[h].astype(BF16))))
    return out


def _mlstm_c(rev, final, L, pa, pb, qkvo_ref, caug_ref, m_ref, o_ref, oc, yb_ref):
    d = 1 if rev else 0
    gw = 4 * LANES
    gp, cum = pa
    mask = _causal(L, rev)
    e_idx = 0 if rev else L - 1
    ones = jnp.ones((L, LANES), BF16)
    r_all = pltpu.roll(gp, 2 * M_HEADS, 1) - cum
    r_all_t = r_all.T
    for h in range(M_HEADS):
        cf = 8 + 4 * d + h
        b_col = cum[:, cf:cf + 1]
        r_col = r_all[:, cf:cf + 1]
        r_row = r_all_t[cf:cf + 1, :]
        m_st = m_ref[h, 0:1, 0:1]
        rel = jnp.where(mask, r_row, NEG)
        a_col = jnp.maximum(m_st, jnp.max(rel, axis=1, keepdims=True))
        m_i = b_col + a_col
        w = jnp.exp(rel - a_col)
        w_inter = jnp.exp(m_st - a_col)
        b_end = cum[e_idx:e_idx + 1, cf:cf + 1]
        m_new = jnp.maximum(b_end + m_st, b_end + jnp.max(r_row, axis=1, keepdims=True))
        ws_col = jnp.exp(b_end + r_col - m_new)
        decay = jnp.exp(b_end + m_st - m_new)
        k = qkvo_ref[:, gw + h * LANES:gw + (h + 1) * LANES]
        v = qkvo_ref[:, 2 * gw + h * LANES:2 * gw + (h + 1) * LANES]
        vaug = jnp.concatenate([v, ones], axis=1)
        qk, qc = pb[h]
        s = (qk * w).astype(BF16)
        res = _dot(s, vaug) + w_inter * qc
        kws = (k.astype(F32) * ws_col).astype(BF16)
        caug_ref[h] = decay * caug_ref[h] + _dot_tn(kws, vaug)
        m_ref[h] = jnp.broadcast_to(m_new, (8, LANES))
        hh = res[:, 0:LANES] / jnp.maximum(jnp.abs(res[:, LANES:2 * LANES]), jnp.exp(-m_i))
        cols = slice(oc + h * LANES, oc + (h + 1) * LANES)
        if final:
            o_ref[:, cols] = (hh + yb_ref[:, cols]).astype(BF16)
        else:
            o_ref[:, cols] = hh


def _ssd_a(rev, L, gate_ref, prm_ref):
    dt_all = _softplus(gate_ref[...] + prm_ref[0:1, :])
    return dt_all, _tri_cumsum(_causal(L, rev).astype(BF16), dt_all * (-jnp.exp(prm_ref[1:2, :])))


def _ssd_b(xbc_ref, st_ref):
    gw = 4 * LANES
    out = []
    for g in range(S_G):
        bg = xbc_ref[:, gw + g * S_N:gw + (g + 1) * S_N]
        cg = xbc_ref[:, gw + S_G * S_N + g * S_N:gw + S_G * S_N + (g + 1) * S_N]
        out.append((_dot_nt(cg, bg), _dot(cg, st_ref[g].astype(BF16))))
    return out


def _ssd_c(rev, final, L, pa, pb, xbc_ref, st_ref, o_ref, oc, yb_ref, dsk_ref):
    d = 1 if rev else 0
    gw = 4 * LANES
    hpg = S_HEADS // S_G
    gcols = gw // S_G
    dt_all, cum = pa
    mask = _causal(L, rev)
    e_idx = 0 if rev else L - 1
    lane_head = lax.broadcasted_iota(jnp.int32, (1, gcols), 1) // (gcols // hpg)
    r_t = (cum - jnp.log(dt_all)).T
    for g in range(S_G):
        wts, e_cols, dec_cols, decays = [], [], [], []
        for hl in range(hpg):
            c = 16 + S_HEADS * d + g * hpg + hl
            b_col = cum[:, c:c + 1]
            b_end = cum[e_idx:e_idx + 1, c:c + 1]
            wts.append(jnp.exp(jnp.where(mask, b_col - r_t[c:c + 1, :], NEG)))
            e_cols.append(jnp.exp(b_col))
            dec_cols.append(dt_all[:, c:c + 1] * jnp.exp(b_end - b_col))
            decays.append(jnp.exp(b_end))
        gc = slice(g * gcols, (g + 1) * gcols)
        xs_g = xbc_ref[:, gc].astype(F32)
        bg = xbc_ref[:, gw + g * S_N:gw + (g + 1) * S_N]
        gm, inter = pb[g]
        acc = _expand4(e_cols, lane_head) * inter
        for hl in range(hpg):
            xh = jnp.where(lane_head == hl, xs_g, 0.0).astype(BF16)
            acc = acc + _dot((gm * wts[hl]).astype(BF16), xh)
        xdec = (xs_g * _expand4(dec_cols, lane_head)).astype(BF16)
        st_ref[g] = _expand4(decays, lane_head) * st_ref[g] + _dot_tn(bg, xdec)
        cols = slice(oc + g * gcols, oc + (g + 1) * gcols)
        if final:
            o_ref[:, cols] = (acc + yb_ref[:, cols] + dsk_ref[0:1, gc] * xs_g).astype(BF16)
        else:
            o_ref[:, cols] = acc


def _ret_b(qk_ref, qc, st_ref):
    gw = 4 * LANES
    out = []
    for h in range(R_HEADS):
        q = qk_ref[:, qc + h * LANES:qc + (h + 1) * LANES]
        k = qk_ref[:, qc + gw + h * LANES:qc + gw + (h + 1) * LANES]
        out.append((_dot_nt(q, k), _dot(q, st_ref[h].astype(BF16))))
    return out


def _ret_tables(rev, L, prm_ref, w_ref, e_ref, dec_ref):
    d = 1 if rev else 0
    mask = _causal(L, rev)
    icol = lax.broadcasted_iota(jnp.int32, (L, 1), 0).astype(F32)
    row = lax.broadcasted_iota(jnp.int32, (L, L), 0)
    col = lax.broadcasted_iota(jnp.int32, (L, L), 1)
    dist = ((col - row) if rev else (row - col)).astype(F32)
    lg_all = -jnp.exp(prm_ref[d:d + 1, :])
    for h in range(R_HEADS):
        lg = lg_all[:, h:h + 1]
        w_ref[h] = jnp.exp(jnp.where(mask, dist * lg, NEG))
        if rev:
            e_col = jnp.exp((float(L) - icol) * lg)
            dec_col = jnp.exp(icol * lg)
        else:
            e_col = jnp.exp((icol + 1.0) * lg)
            dec_col = jnp.exp((float(L - 1) - icol) * lg)
        e_ref[h] = jnp.broadcast_to(e_col, (L, LANES))
        dec_ref[h] = jnp.broadcast_to(dec_col, (L, LANES))


def _ret_c(rev, final, L, pb, qk_ref, qc, v_ref, prm_ref, tables, st_ref, o_ref, oc, yb_ref):
    d = 1 if rev else 0
    gw = 4 * LANES
    w_ref, e_ref, dec_ref = tables
    lg_all = -jnp.exp(prm_ref[d:d + 1, :])
    for h in range(R_HEADS):
        decay = jnp.exp(float(L) * lg_all[:, h:h + 1])
        hc = slice(h * LANES, (h + 1) * LANES)
        k = qk_ref[:, qc + gw + h * LANES:qc + gw + (h + 1) * LANES]
        v = v_ref[:, hc]
        qk, qst = pb[h]
        y = _dot((qk * w_ref[h]).astype(BF16), v) + e_ref[h] * qst
        kd = (k.astype(F32) * dec_ref[h]).astype(BF16)
        st_ref[h] = decay * st_ref[h] + _dot_tn(kd, v)
        cols = slice(oc + h * LANES, oc + (h + 1) * LANES)
        if final:
            o_ref[:, cols] = (y + yb_ref[:, cols]).astype(BF16)
        else:
            o_ref[:, cols] = y


def _gla_a(rev, L, tb, gate_ref, gw_ref, gb_ref):
    row = lax.broadcasted_iota(jnp.int32, (tb, tb), 0)
    col = lax.broadcasted_iota(jnp.int32, (tb, tb), 1)
    blockmask = jnp.logical_and((row // L) == (col // L), (col >= row) if rev else (col <= row))
    z = _dot_f32(gate_ref[...], gw_ref[...]) + gb_ref[...]
    return blockmask, _tri_cumsum(blockmask.astype(BF16), _log_sigmoid(z) * (1.0 / G_TAU))


def _gla_c(rev, final, L, nsub, pa, qk_ref, v_ref, st_ref, o_ref, oc, yb_ref):
    kw = 2 * LANES
    blockmask, cum = pa
    e_idx = 0 if rev else L - 1
    lane_head = lax.broadcasted_iota(jnp.int32, (1, kw), 1) // (kw // G_HEADS)
    sub = [slice(c * L, (c + 1) * L) for c in range(nsub)]
    qf = qk_ref[:, 0:kw].astype(F32)
    kf = qk_ref[:, kw:2 * kw].astype(F32)
    qg = qf * jnp.exp(cum)
    kg = (kf * jnp.exp(-cum)).astype(BF16)
    b_ends = [cum[c * L + e_idx:c * L + e_idx + 1, :] for c in range(nsub)]
    kd = jnp.concatenate([kf[sub[c], :] * jnp.exp(b_ends[c] - cum[sub[c], :]) for c in range(nsub)], axis=0)
    qgh = [jnp.where(lane_head == h, qg, 0.0).astype(BF16) for h in range(G_HEADS)]
    kdh = [jnp.where(lane_head == h, kd, 0.0).astype(BF16) for h in range(G_HEADS)]
    s_all = [_dot_nt(qgh[h], kg) for h in range(G_HEADS)]
    upd = [sum(_dot_tn(v_ref[sub[c], h * LANES:(h + 1) * LANES], kdh[h][sub[c], :]) for h in range(G_HEADS))
           for c in range(nsub)]
    st = st_ref[...]
    st_at = [None] * nsub
    for c in (reversed(range(nsub)) if rev else range(nsub)):
        st_at[c] = st.astype(BF16)
        st = jnp.exp(b_ends[c]) * st + upd[c]
    st_ref[...] = st
    for h in range(G_HEADS):
        hc = slice(h * LANES, (h + 1) * LANES)
        cols = slice(oc + h * LANES, oc + (h + 1) * LANES)
        inter = jnp.concatenate([_dot_nt(qgh[h][sub[c], :], st_at[c]) for c in range(nsub)], axis=0)
        y = _dot(jnp.where(blockmask, s_all[h], 0.0).astype(BF16), v_ref[:, hc]) + inter
        if final:
            o_ref[:, cols] = (y + yb_ref[:, cols]).astype(BF16)
        else:
            o_ref[:, cols] = y


def _scan_kernel(*refs, rev, final, tb):
    (um_m, ug, prep, r_v, g_qk, g_v, m_bias, s_prm, r_prm, gwp, gb) = refs[:11]
    if final:
        yb, dsk = refs[11:13]
        o_ref = refs[13]
        scratch = refs[14:]
    else:
        yb = dsk = None
        o_ref = refs[11]
        scratch = refs[12:]
    caug, m_st, s_st, r_st, g_st = scratch[:5]
    r_tables = scratch[5:]
    gw = 4 * LANES

    @pl.when(pl.program_id(1) == 0)
    def _():
        for ref in scratch[:5]:
            ref[...] = jnp.zeros_like(ref)
        _ret_tables(rev, tb, r_prm, *r_tables)

    r_b = _ret_b(prep, 2 * gw, r_st)
    m_b = _mlstm_b(um_m, caug)
    s_b = _ssd_b(prep, s_st)
    m_a = _mlstm_a(rev, tb, ug, m_bias)
    s_a = _ssd_a(rev, tb, ug, s_prm)
    g_a = _gla_a(rev, GLA_CHUNK, tb, ug, gwp, gb)
    _ret_c(rev, final, tb, r_b, prep, 2 * gw, r_v, r_prm, r_tables, r_st, o_ref, 2 * gw, yb)
    _gla_c(rev, final, GLA_CHUNK, tb // GLA_CHUNK, g_a, g_qk, g_v, g_st, o_ref, 3 * gw, yb)
    _mlstm_c(rev, final, tb, m_a, m_b, um_m, caug, m_st, o_ref, 0, yb)
    _ssd_c(rev, final, tb, s_a, s_b, prep, s_st, o_ref, gw, yb, dsk)


def _scan_call(rev, final, b, tb, nctx, um, ug, prep, params, yb, dsk):
    t_rows = um.shape[0]
    gw = 4 * LANES
    nblk = t_rows // (b * tb)
    fb = _flat_block(b, nctx, nblk)
    if rev:
        def order(c):
            return jnp.where(c < nctx, nctx - 1 - c, nblk + nctx - 1 - c)
    else:
        def order(c):
            return c

    def tok(width, cb):
        return pl.BlockSpec((tb, width), lambda i, c: (fb(i, order(c)), cb))

    def full(arr):
        return pl.BlockSpec(arr.shape, lambda i, c: (0,) * arr.ndim)

    ins = [um, ug, prep, um, um, um] + list(params)
    in_specs = [tok(4 * gw, 0), tok(GATE_W, 0), tok(4 * gw, 0), tok(gw, 8), tok(gw, 11), tok(gw, 12)]
    in_specs += [full(p) for p in params]
    if final:
        ins += [yb, dsk]
        in_specs += [tok(4 * gw, 0), full(dsk)]
    scratch = [pltpu.VMEM((M_HEADS, LANES, 2 * LANES), F32), pltpu.VMEM((M_HEADS, 8, LANES), F32),
               pltpu.VMEM((S_G, S_N, gw // S_G), F32), pltpu.VMEM((R_HEADS, LANES, LANES), F32),
               pltpu.VMEM((LANES, 2 * LANES), F32),
               pltpu.VMEM((R_HEADS, tb, tb), F32), pltpu.VMEM((R_HEADS, tb, LANES), F32),
               pltpu.VMEM((R_HEADS, tb, LANES), F32)]
    return pl.pallas_call(
        functools.partial(_scan_kernel, rev=rev, final=final, tb=tb),
        out_shape=jax.ShapeDtypeStruct((t_rows, 4 * gw), BF16 if final else F32),
        grid=(b, nblk),
        in_specs=in_specs,
        out_specs=tok(4 * gw, 0),
        scratch_shapes=scratch,
        compiler_params=_cparams(("parallel", "arbitrary"), 48),
        name="scan_fwd" if final else "scan_bwd",
    )(*ins)


def _in_col_layout(d_model):
    gw = d_model // N_GROUPS
    conv_ch = gw + 2 * S_G * S_N
    names = [('m_q', gw), ('m_k', gw), ('m_v', gw), ('m_o', gw), ('m_i', 2 * M_HEADS), ('m_f', 2 * M_HEADS),
             ('s_z', gw), ('s_xbc', conv_ch), ('s_dt', 2 * S_HEADS),
             ('r_q', gw), ('r_k', gw), ('r_v', gw), ('r_g', gw),
             ('g_q', gw // 2), ('g_k', gw // 2), ('g_v', gw), ('g_g', gw), ('g_a', 2 * G_RANK)]
    off, o = {}, 0
    for nm, n in names:
        off[nm] = (o, n)
        o += n
    return off


def kernel(x, c, ctx, c_ctx, ada_w, ada_b, w_in, m_ig_b, m_fg_b, m_norm, s_conv_w, s_conv_b, s_dt_bias, s_a_log, s_d, s_norm, r_decay, r_norm, g_gate_w, g_gate_b, g_norm, w_out, post_g, post_b, ffn_w_up, ffn_w_down):
    b, seq, d = x.shape
    n_ctx_tok = ctx.shape[1]
    depth = ada_w.shape[0]
    gw = d // N_GROUPS
    assert gw == 4 * LANES and b + 1 <= 8
    s_tot = n_ctx_tok + seq
    t = b * s_tot
    tb = math.gcd(MAX_TOKEN_BLOCK, math.gcd(n_ctx_tok, seq))
    assert tb % 16 == 0 and tb % GLA_CHUNK == 0
    nctx = n_ctx_tok // tb
    nblk = s_tot // tb
    nsb = t // tb
    t_ctx = b * n_ctx_tok
    t_lat = b * seq

    def row_tile(cap):
        return max(m for m in (1024, 512, 256, 128, 64, 32, 16)
                   if m <= cap and t_ctx % m == 0 and t_lat % m == 0 and m % tb == 0)

    tm = row_tile(1024)
    tm_out = row_tile(512)
    alpha = (2.0 * depth) ** 0.25

    quarter = LANES // 4
    n_rows = seq // GRID_W
    freqs = 1.0 / (ROPE_BASE ** (jnp.arange(quarter, dtype=F32) / quarter))
    ang_r = jnp.arange(n_rows, dtype=F32)[:, None] * freqs[None, :]
    ang_c = jnp.arange(GRID_W, dtype=F32)[:, None] * freqs[None, :]
    cos_r, sin_r = (jnp.repeat(f(ang_r), GRID_W, axis=0) for f in (jnp.cos, jnp.sin))
    cos_c, sin_c = (jnp.tile(f(ang_c), (n_rows, 1)) for f in (jnp.cos, jnp.sin))
    cos_t = jnp.concatenate([cos_r, cos_r, cos_c, cos_c], axis=1)
    sin_t = jnp.concatenate([-sin_r, sin_r, -sin_c, sin_c], axis=1)
    rope = jnp.concatenate([
        jnp.concatenate([jnp.ones((n_ctx_tok, LANES), F32), jnp.zeros((n_ctx_tok, LANES), F32)], axis=1),
        jnp.concatenate([cos_t, sin_t], axis=1)], axis=0)

    xs = (ctx.reshape(t_ctx, d), x.reshape(t_lat, d))
    cc =jnp.concatenate([c, c_ctx[None, :], jnp.zeros((8 - b - 1, d), F32)], axis=0)
    mod = _mod_call(cc, ada_w, ada_b)
    sb = np.arange(nsb)
    mod_row = np.where(sb < b * nctx, b, (sb - b * nctx) // (nblk - nctx))

    def lane_rows(vecs, start, n_rows):
        rows = jnp.stack(vecs)
        return jnp.pad(rows, ((0, n_rows - rows.shape[0]), (start, LANES - start - rows.shape[1])))

    wm, wg = _win_call(w_in)
    w_out_b = w_out.astype(BF16)
    w_up_b = ffn_w_up.astype(BF16)
    w_down_b = ffn_w_down.astype(BF16)
    for l in range(depth):
        row_scale = np.array([1.0, 1.0, 1.0 / alpha, 1.0, 1.0, 1.0 / alpha], np.float32)[None, :, None]
        mods = jnp.pad(mod[l][mod_row].reshape(nsb, 6, d) * row_scale, ((0, 0), (0, 2), (0, 0)))
        um, ug = _in_call(xs, mods, wm, wg, l, tm, tb)

        conv_w = jnp.pad(s_conv_w[l], ((0, 8 - CONV_W), (0, 0)))
        prep = _prep_call(um, conv_w, s_conv_b[l][None, :], rope, b, nctx, tb)

        m_bias = lane_rows([jnp.concatenate([m_ig_b[l].reshape(-1), m_fg_b[l].reshape(-1)])], 0, 1)
        s_prm = lane_rows([s_dt_bias[l].reshape(-1), s_a_log[l].reshape(-1)], 16, 8)
        r_prm = lane_rows([r_decay[l][0], r_decay[l][1]], 0, 8)
        nrm = jnp.concatenate([jnp.stack([m_norm[l], s_norm[l], r_norm[l], g_norm[l]]), jnp.zeros((4, gw), F32)])
        dsk = jnp.repeat(s_d[l], gw // S_HEADS)[None, :]

        def dir_params(dd):
            lo = 32 + G_RANK * dd
            gwp = jnp.pad(g_gate_w[l][dd], ((lo, GATE_W - lo - G_RANK), (0, 0)))
            return [m_bias, s_prm, r_prm, gwp, g_gate_b[l][dd][None, :]]

        yb = _scan_call(True, False, b, tb, nctx, um, ug, prep, dir_params(1), None, None)
        ysum = _scan_call(False, True, b, tb, nctx, um, ug, prep, dir_params(0), yb, dsk)

        skip = t_ctx if l == depth - 1 else 0
        x1 = _out_call(ysum, um, nrm, w_out_b, l, xs, mods, skip // tm_out,
                       post_g[l, 0][None, :], post_b[l, 0][None, :], tm_out, tb, alpha)
        act = _up_call(x1, mods, skip // tm, w_up_b, l, tm, tb)
        xs = (_res_call(act, w_down_b, l, x1, mods, skip // tb,
                        post_g[l, 1][None, :], post_b[l, 1][None, :], tb, tb, 5, alpha, "ffn_down_res_ln"),)

    return xs[0].reshape(b, seq, d)
```

```python
import functools
import math

import numpy as np
import jax
import jax.numpy as jnp
from jax import lax
from jax.experimental import pallas as pl
from jax.experimental.pallas import tpu as pltpu

F32 = jnp.float32
BF16 = jnp.bfloat16

EPS = 1e-5
ROPE_BASE = 10000.0
GRID_W = 64
N_GROUPS = 4
M_HEADS = 4
S_HEADS = 8
S_G = 2
S_N = 128
CONV_W = 3
R_HEADS = 4
G_HEADS = 4
G_RANK = 16
G_TAU = 16.0
NEG = -1e30

LANES = 128
GATE_W = LANES
MAX_TOKEN_BLOCK = 256
GLA_CHUNK = 64


def _cparams(sem, vmem_mb):
    return pltpu.CompilerParams(dimension_semantics=sem, vmem_limit_bytes=vmem_mb * 1024 * 1024)


def _dot(a, b):
    return jnp.dot(a, b, preferred_element_type=F32)


def _dot_nt(a, b):
    return lax.dot_general(a, b, (((1,), (1,)), ((), ())), preferred_element_type=F32)


def _dot_tn(a, b):
    return lax.dot_general(a, b, (((0,), (0,)), ((), ())), preferred_element_type=F32)


def _split3(x):
    hi = x.astype(BF16)
    r1 = x - hi.astype(F32)
    mid = r1.astype(BF16)
    lo = (r1 - mid.astype(F32)).astype(BF16)
    return hi, mid, lo


def _tri_cumsum(tri, x):
    hi, mid, lo = _split3(x)
    return _dot(tri, hi) + _dot(tri, mid) + _dot(tri, lo)


def _dot_f32(a, b):
    ah = a.astype(BF16)
    al = (a - ah.astype(F32)).astype(BF16)
    bh = b.astype(BF16)
    bl = (b - bh.astype(F32)).astype(BF16)
    return _dot(ah, bh) + _dot(ah, bl) + _dot(al, bh)


def _sigmoid(x):
    return 1.0 / (1.0 + jnp.exp(-x))


def _silu(x):
    return x * _sigmoid(x)


def _softplus(x):
    return jnp.maximum(x, 0.0) + jnp.log1p(jnp.exp(-jnp.abs(x)))


def _log_sigmoid(x):
    return jnp.minimum(x, 0.0) - jnp.log1p(jnp.exp(-jnp.abs(x)))


def _ln(x, eps=EPS):
    mu = jnp.mean(x, axis=-1, keepdims=True)
    xc = x - mu
    var = jnp.mean(xc * xc, axis=-1, keepdims=True)
    return xc * lax.rsqrt(var + eps)


def _res_ln(x, gate_over_alpha, y, alpha):
    return _ln(x + gate_over_alpha * y, EPS / (alpha * alpha))


def _causal(n, rev):
    row = lax.broadcasted_iota(jnp.int32, (n, n), 0)
    col = lax.broadcasted_iota(jnp.int32, (n, n), 1)
    return (col >= row) if rev else (col <= row)


def _expand4(cols, lane_head):
    return jnp.where(lane_head == 0, cols[0],
                     jnp.where(lane_head == 1, cols[1],
                               jnp.where(lane_head == 2, cols[2], cols[3])))


MOD_SPLIT = 4


def _mod_kernel(c_ref, *refs):
    w_refs = refs[:MOD_SPLIT]
    b_ref, o_ref = refs[MOD_SPLIT:]
    s = _silu(c_ref[...]).astype(BF16)
    wn = w_refs[0].shape[2]
    for q, w_ref in enumerate(w_refs):
        cols = slice(q * wn, (q + 1) * wn)
        part = _dot(s, w_ref[0].astype(BF16))

        @pl.when(pl.program_id(1) == 0)
        def _():
            o_ref[0, :, cols] = part + b_ref[0, :, cols]

        @pl.when(pl.program_id(1) > 0)
        def _():
            o_ref[0, :, cols] += part


def _mod_call(cc, ada_w, ada_b):
    depth, d, n = ada_w.shape
    tk = 256
    wn = n // MOD_SPLIT
    w_specs = [pl.BlockSpec((1, tk, wn), functools.partial(lambda l, k, q: (l, k, q), q=q))
               for q in range(MOD_SPLIT)]
    return pl.pallas_call(
        _mod_kernel,
        out_shape=jax.ShapeDtypeStruct((depth, 8, n), F32),
        grid=(depth, d // tk),
        in_specs=[pl.BlockSpec((8, tk), lambda l, k: (0, k))] + w_specs
                 + [pl.BlockSpec((1, 1, n), lambda l, k: (l, 0, 0))],
        out_specs=pl.BlockSpec((1, 8, n), lambda l, k: (l, 0, 0)),
        compiler_params=_cparams(("parallel", "arbitrary"), 40),
        name="adaln_mod",
    )(cc, *([ada_w] * MOD_SPLIT), ada_b.reshape(depth, 1, n))


def _stream_specs(srcs, tm, d, row_tile):
    if len(srcs) == 1:
        return [pl.BlockSpec((tm, d), lambda *g: (row_tile(*g), 0))], 0
    lead_tiles = srcs[0].shape[0] // tm
    return [pl.BlockSpec((tm, d), lambda *g: (jnp.minimum(row_tile(*g), lead_tiles - 1), 0),
                         pipeline_mode=pl.Buffered(1)),
            pl.BlockSpec((tm, d), lambda *g: (jnp.maximum(row_tile(*g) - lead_tiles, 0), 0))], lead_tiles


def _stream_rows(x_refs, lead_tiles, tile, rows):
    if len(x_refs) == 1:
        return x_refs[0][rows, :]
    return jnp.where(tile < lead_tiles, x_refs[0][rows, :], x_refs[1][rows, :])


def _ln_mod_to(h_ref, x_ref, mods_ref, nsub, tb, sh_row, sc_row):
    for r in range(nsub):
        x = x_ref[r * tb:(r + 1) * tb, :]
        h = _ln(x) * (1.0 + mods_ref[r, sc_row:sc_row + 1, :]) + mods_ref[r, sh_row:sh_row + 1, :]
        h_ref[r * tb:(r + 1) * tb, :] = h.astype(BF16)


def _in_kernel(*refs, n_src, lead_tiles, nsub, tb):
    x_refs = refs[:n_src]
    mods_ref, wm_ref, wg_ref, um_ref, ug_ref, h_ref = refs[n_src:]

    @pl.when(pl.program_id(1) == 0)
    def _():
        for r in range(nsub):
            rows = slice(r * tb, (r + 1) * tb)
            x = _stream_rows(x_refs, lead_tiles, pl.program_id(0), rows)
            h = _ln(x) * (1.0 + mods_ref[r, 1:2, :]) + mods_ref[r, 0:1, :]
            h_ref[rows, :] = h.astype(BF16)
        ug_ref[...] = _dot(h_ref[...], wg_ref[...])

    um_ref[...] = _dot(h_ref[...], wm_ref[...]).astype(BF16)


def _in_call(srcs, mods, wm, wg, layer, tm, tb):
    t = sum(s.shape[0] for s in srcs)
    d = srcs[0].shape[1]
    n = wm.shape[2]
    tn = 1024
    nsub = tm // tb
    x_specs, lead_tiles = _stream_specs(srcs, tm, d, lambda i, j: i)
    return pl.pallas_call(
        functools.partial(_in_kernel, n_src=len(srcs), lead_tiles=lead_tiles, nsub=nsub, tb=tb),
        out_shape=(jax.ShapeDtypeStruct((t, n), BF16), jax.ShapeDtypeStruct((t, GATE_W), F32)),
        grid=(t // tm, n // tn),
        in_specs=x_specs + [pl.BlockSpec((nsub, 8, d), lambda i, j: (i, 0, 0)),
                            pl.BlockSpec((None, d, tn), lambda i, j: (layer, 0, j)),
                            pl.BlockSpec((None, d, GATE_W), lambda i, j: (layer, 0, 0))],
        out_specs=(pl.BlockSpec((tm, tn), lambda i, j: (i, j)),
                   pl.BlockSpec((tm, GATE_W), lambda i, j: (i, 0))),
        scratch_shapes=[pltpu.VMEM((tm, d), BF16)],
        compiler_params=_cparams(("parallel", "arbitrary"), 48 + 8 * (len(srcs) - 1)),
        name="ln_in_proj",
    )(*srcs, mods, wm, wg)


MAIN_ORDER = ('m_q', 'm_k', 'm_v', 'm_o', 's_xbc', 'r_q', 'r_k', 'r_v', 'r_g', 's_z', 'g_q', 'g_k', 'g_v', 'g_g')
GATE_ORDER = ('m_i', 'm_f', 's_dt', 'g_a')


def _win_plan(d_model):
    off = _in_col_layout(d_model)
    gw = d_model // N_GROUPS
    scale = {'m_q': float(gw // M_HEADS) ** -0.5, 'r_k': float(gw // R_HEADS) ** -0.5,
             'g_q': float(gw // 2 // G_HEADS) ** -0.5}
    slabs = [(off[name][0] + k * LANES, scale.get(name, 1.0))
             for name in MAIN_ORDER for k in range(off[name][1] // LANES)]
    return slabs, [off[name] for name in GATE_ORDER]


def _win_kernel(w_ref, tail_ref, wm_ref, wg_ref, *, slabs, gates, n_full):
    lane = lax.broadcasted_iota(jnp.int32, (1, LANES), 1)
    rolled = {}

    def col(v):
        return tail_ref[...] if v == n_full else w_ref[:, v * LANES:(v + 1) * LANES]

    def rolled_col(v, s):
        if (v, s) not in rolled:
            rolled[(v, s)] = pltpu.roll(col(v), LANES - s, 1)
        return rolled[(v, s)]

    for j, (src, sc) in enumerate(slabs):
        v, s = divmod(src, LANES)
        val = col(v) if s == 0 else jnp.where(lane < LANES - s, rolled_col(v, s), rolled_col(v + 1, s))
        wm_ref[:, j * LANES:(j + 1) * LANES] = (val if sc == 1.0 else val * sc).astype(BF16)
    g = jnp.zeros(wg_ref.shape, F32)
    dst = 0
    for src, n in gates:
        v, s = divmod(src, LANES)
        assert s == dst, "gate group must already sit at its destination lane"
        g = jnp.where(jnp.logical_and(lane >= dst, lane < dst + n), col(v), g)
        dst += n
    wg_ref[...] = g.astype(BF16)


def _win_call(w_in):
    depth, d, n_in = w_in.shape
    slabs, gates = _win_plan(d)
    n_full = n_in // LANES
    assert n_in % LANES != 0 and sum(n for _, n in gates) <= GATE_W
    tr = 256
    nm = len(slabs) * LANES
    return pl.pallas_call(
        functools.partial(_win_kernel, slabs=slabs, gates=gates, n_full=n_full),
        out_shape=(jax.ShapeDtypeStruct((depth, d, nm), BF16), jax.ShapeDtypeStruct((depth, d, GATE_W), BF16)),
        grid=(depth, d // tr),
        in_specs=[pl.BlockSpec((None, tr, n_in), lambda l, r: (l, r, 0)),
                  pl.BlockSpec((None, tr, LANES), lambda l, r: (l, r, n_full))],
        out_specs=(pl.BlockSpec((None, tr, nm), lambda l, r: (l, r, 0)),
                   pl.BlockSpec((None, tr, GATE_W), lambda l, r: (l, r, 0))),
        compiler_params=_cparams(("parallel", "parallel"), 40),
        name="w_in_relayout",
    )(w_in, w_in)


def _up_kernel(x_ref, mods_ref, wa_ref, wg_ref, o_ref, h_ref, *, nsub, tb):
    @pl.when(pl.program_id(1) == 0)
    def _():
        _ln_mod_to(h_ref, x_ref, mods_ref, nsub, tb, 3, 4)

    a = _dot(h_ref[...], wa_ref[...])
    g = _dot(h_ref[...], wg_ref[...])
    o_ref[...] = (_silu(a) * g).astype(BF16)


def _up_call(xs, mods, mods_off, w_up, layer, tm, tb):
    t, d = xs.shape
    dff = w_up.shape[2] // 2
    tn = 512
    nj = dff // tn
    nsub = tm // tb
    return pl.pallas_call(
        functools.partial(_up_kernel, nsub=nsub, tb=tb),
        out_shape=jax.ShapeDtypeStruct((t, dff), BF16),
        grid=(t // tm, nj),
        in_specs=[pl.BlockSpec((tm, d), lambda i, j: (i, 0)),
                  pl.BlockSpec((nsub, 8, d), lambda i, j: (i + mods_off, 0, 0)),
                  pl.BlockSpec((None, d, tn), lambda i, j: (layer, 0, j)),
                  pl.BlockSpec((None, d, tn), lambda i, j: (layer, 0, j + nj))],
        out_specs=pl.BlockSpec((tm, tn), lambda i, j: (i, j)),
        scratch_shapes=[pltpu.VMEM((tm, d), BF16)],
        compiler_params=_cparams(("parallel", "arbitrary"), 48),
        name="ln_ffn_up",
    )(xs, mods, w_up, w_up)


def _res_kernel(a_ref, w_ref, x_ref, mods_ref, pg_ref, pb_ref, o_ref, *, nsub, tb, g_row, alpha):
    chunk = tb // 2
    for r in range(nsub * 2):
        rows = slice(r * chunk, (r + 1) * chunk)
        y = _dot(a_ref[rows, :], w_ref[...])
        zn = _res_ln(x_ref[rows, :], mods_ref[r // 2, g_row:g_row + 1, :], y, alpha)
        o_ref[rows, :] = zn * pg_ref[...] + pb_ref[...]


def _res_call(act, w, layer, xs, mods, mods_off, pg, pb, tm, tb, g_row, alpha, name):
    t, d = xs.shape
    ka = act.shape[1]
    nsub = tm // tb
    return pl.pallas_call(
        functools.partial(_res_kernel, nsub=nsub, tb=tb, g_row=g_row, alpha=alpha),
        out_shape=jax.ShapeDtypeStruct((t, d), F32),
        grid=(t // tm,),
        in_specs=[pl.BlockSpec((tm, ka), lambda i: (i, 0)),
                  pl.BlockSpec((None, ka, d), lambda i: (layer, 0, 0), pipeline_mode=pl.Buffered(1)),
                  pl.BlockSpec((tm, d), lambda i: (i, 0)),
                  pl.BlockSpec((nsub, 8, d), lambda i: (i + mods_off, 0, 0)),
                  pl.BlockSpec((1, d), lambda i: (0, 0)),
                  pl.BlockSpec((1, d), lambda i: (0, 0))],
        out_specs=pl.BlockSpec((tm, d), lambda i: (i, 0)),
        compiler_params=_cparams(("parallel",), 52),
        name=name,
    )(act, w, xs, mods, pg, pb)


def _head_norm(y, center):
    if center:
        y = y - jnp.mean(y, axis=-1, keepdims=True)
    return y * lax.rsqrt(jnp.mean(y * y, axis=-1, keepdims=True) + EPS)


def _mix_finalize(act_ref, rows, ys_ref, mo_ref, sz_ref, rg_ref, gg_ref, nrm_ref):
    gw = 4 * LANES
    for h in range(M_HEADS):
        hc = slice(h * LANES, (h + 1) * LANES)
        yn = _head_norm(ys_ref[rows, hc].astype(F32), True)
        act_ref[rows, hc] = (_sigmoid(mo_ref[rows, hc].astype(F32)) * (yn * nrm_ref[0:1, hc])).astype(BF16)
    halves = [slice(g * (gw // S_G), (g + 1) * (gw // S_G)) for g in range(S_G)]
    ys = [ys_ref[rows, gw + gc.start:gw + gc.stop].astype(F32) * _silu(sz_ref[rows, gc].astype(F32)) for gc in halves]
    inv = lax.rsqrt(sum(jnp.sum(y * y, axis=-1, keepdims=True) for y in ys) * (1.0 / gw) + EPS)
    for y, gc in zip(ys, halves):
        act_ref[rows, gw + gc.start:gw + gc.stop] = (y * inv * nrm_ref[1:2, gc]).astype(BF16)
    for h in range(R_HEADS):
        hc = slice(h * LANES, (h + 1) * LANES)
        yn = _head_norm(ys_ref[rows, 2 * gw + h * LANES:2 * gw + (h + 1) * LANES].astype(F32), True)
        act_ref[rows, 2 * gw + h * LANES:2 * gw + (h + 1) * LANES] = (
            yn * nrm_ref[2:3, hc] * _silu(rg_ref[rows, hc].astype(F32))).astype(BF16)
    for h in range(G_HEADS):
        hc = slice(h * LANES, (h + 1) * LANES)
        yn = _head_norm(ys_ref[rows, 3 * gw + h * LANES:3 * gw + (h + 1) * LANES].astype(F32), False)
        act_ref[rows, 3 * gw + h * LANES:3 * gw + (h + 1) * LANES] = (
            yn * nrm_ref[3:4, hc] * _silu(gg_ref[rows, hc].astype(F32))).astype(BF16)


def _out_kernel(*refs, n_src, lead_tiles, skip, nsub, tb, alpha):
    ys_ref, mo_ref, sz_ref, rg_ref, gg_ref, nrm_ref, w_ref = refs[:7]
    x_refs = refs[7:7 + n_src]
    mods_ref, pg_ref, pb_ref, o_ref, act_ref = refs[7 + n_src:]
    for r in range(nsub):
        rows = slice(r * tb, (r + 1) * tb)
        _mix_finalize(act_ref, rows, ys_ref, mo_ref, sz_ref, rg_ref, gg_ref, nrm_ref)
        y = _dot(act_ref[rows, :], w_ref[...])
        x = _stream_rows(x_refs, lead_tiles, pl.program_id(0) + skip, rows)
        o_ref[rows, :] = _res_ln(x, mods_ref[r, 2:3, :], y, alpha) * pg_ref[...] + pb_ref[...]


def _out_call(ysum, um, nrm, w, layer, srcs, mods, skip, pg, pb, tm, tb, alpha):
    t = sum(s.shape[0] for s in srcs)
    d = srcs[0].shape[1]
    gw = d // N_GROUPS
    nsub = tm // tb
    x_specs, lead_tiles = _stream_specs(srcs, tm, d, lambda i: i + skip)

    def gate(cb):
        return pl.BlockSpec((tm, gw), lambda i: (i + skip, cb))

    return pl.pallas_call(
        functools.partial(_out_kernel, n_src=len(srcs), lead_tiles=lead_tiles, skip=skip, nsub=nsub, tb=tb,
                          alpha=alpha),
        out_shape=jax.ShapeDtypeStruct((t - skip * tm, d), F32),
        grid=(t // tm - skip,),
        in_specs=[pl.BlockSpec((tm, d), lambda i: (i + skip, 0)), gate(3), gate(10), gate(9), gate(13),
                  pl.BlockSpec(nrm.shape, lambda i: (0, 0)),
                  pl.BlockSpec((None, d, d), lambda i: (layer, 0, 0), pipeline_mode=pl.Buffered(1))]
                 + x_specs
                 + [pl.BlockSpec((nsub, 8, d), lambda i: (i + skip, 0, 0)),
                    pl.BlockSpec((1, d), lambda i: (0, 0)),
                    pl.BlockSpec((1, d), lambda i: (0, 0))],
        out_specs=pl.BlockSpec((tm, d), lambda i: (i, 0)),
        scratch_shapes=[pltpu.VMEM((tm, d), BF16)],
        compiler_params=_cparams(("parallel",), 52),
        name="mix_out_proj_res_ln",
    )(ysum, um, um, um, um, nrm, w, *srcs, mods, pg, pb)


def _prep_kernel(cur_ref, prev_ref, next_ref, rqk_ref, cw_ref, cb_ref, rope_ref, o_ref, *, nctx, nblk, tb):
    t = pl.program_id(1)
    seg_start = jnp.logical_or(t == 0, t == nctx)
    seg_end = jnp.logical_or(t == nctx - 1, t == nblk - 1)
    nc = cur_ref.shape[1]
    ridx = lax.broadcasted_iota(jnp.int32, (tb, 1), 0)
    for blk in range(nc // LANES):
        cs = slice(blk * LANES, (blk + 1) * LANES)
        x = cur_ref[:, cs].astype(F32)
        prev_row = jnp.where(seg_start, 0.0, prev_ref[:, cs].astype(F32)[15:16, :])
        next_row = jnp.where(seg_end, 0.0, next_ref[:, cs].astype(F32)[0:1, :])
        xp = jnp.where(ridx == 0, prev_row, pltpu.roll(x, 1, 0))
        xn = jnp.where(ridx == tb - 1, next_row, pltpu.roll(x, tb - 1, 0))
        y = cb_ref[:, cs] + xp * cw_ref[0:1, cs] + x * cw_ref[1:2, cs] + xn * cw_ref[2:3, cs]
        o_ref[:, cs] = _silu(y).astype(BF16)

    cos = rope_ref[:, 0:LANES]
    sin = rope_ref[:, LANES:2 * LANES]
    lane = lax.broadcasted_iota(jnp.int32, (1, LANES), 1)
    first = (lane % 64) < 32
    for blk in range(rqk_ref.shape[1] // LANES):
        xh = rqk_ref[:, blk * LANES:(blk + 1) * LANES].astype(F32)
        partner = jnp.where(first, pltpu.roll(xh, 96, 1), pltpu.roll(xh, 32, 1))
        o_ref[:, nc + blk * LANES:nc + (blk + 1) * LANES] = (xh * cos + partner * sin).astype(BF16)


def _flat_block(b, nctx, nblk):
    nlat = nblk - nctx
    return lambda i, t: jnp.where(t < nctx, i * nctx + t, b * nctx + i * nlat + (t - nctx))


def _prep_call(um, conv_w, conv_b, rope, b, nctx, tb):
    t_rows = um.shape[0]
    nblk = t_rows // (b * tb)
    hb = tb // 16
    nc = conv_w.shape[1]
    fb = _flat_block(b, nctx, nblk)
    return pl.pallas_call(
        functools.partial(_prep_kernel, nctx=nctx, nblk=nblk, tb=tb),
        out_shape=jax.ShapeDtypeStruct((t_rows, 2 * nc), BF16),
        grid=(b, nblk),
        in_specs=[pl.BlockSpec((tb, nc), lambda i, t: (fb(i, t), 2)),
                  pl.BlockSpec((16, nc), lambda i, t: (jnp.maximum(fb(i, t) * hb - 1, 0), 2)),
                  pl.BlockSpec((16, nc), lambda i, t: (jnp.minimum((fb(i, t) + 1) * hb, t_rows // 16 - 1), 2)),
                  pl.BlockSpec((tb, nc), lambda i, t: (fb(i, t), 3)),
                  pl.BlockSpec((8, nc), lambda i, t: (0, 0)),
                  pl.BlockSpec((1, nc), lambda i, t: (0, 0)),
                  pl.BlockSpec((tb, 2 * LANES), lambda i, t: (t, 0))],
        out_specs=pl.BlockSpec((tb, 2 * nc), lambda i, t: (fb(i, t), 0)),
        compiler_params=_cparams(("parallel", "parallel"), 32),
        name="prep_conv_rope",
    )(um, um, um, um, conv_w, conv_b, rope)


def _mlstm_a(rev, L, gate_ref, bias_ref):
    lane = lax.broadcasted_iota(jnp.int32, (1, LANES), 1)
    is_f = jnp.logical_and(lane >= 8, lane < 16)
    g = gate_ref[...] + bias_ref[0:1, :]
    gp = jnp.where(is_f, _log_sigmoid(g), g)
    return gp, _tri_cumsum(_causal(L, rev).astype(BF16), gp)


def _mlstm_b(qkvo_ref, caug_ref):
    gw = 4 * LANES
    out = []
    for h in range(M_HEADS):
        q = qkvo_ref[:, h * LANES:(h + 1) * LANES]
        k = qkvo_ref[:, gw + h * LANES:gw + (h + 1) * LANES]
        out.append((_dot_nt(q, k), _dot(q, caug_ref[h].astype(BF16))))
    return out


def _mlstm_c(rev, final, L, pa, pb, qkvo_ref, caug_ref, m_ref, o_ref, oc, yb_ref):
    d = 1 if rev else 0
    gw = 4 * LANES
    gp, cum = pa
    mask = _causal(L, rev)
    e_idx = 0 if rev else L - 1
    ones = jnp.ones((L, LANES), BF16)
    r_all = pltpu.roll(gp, 2 * M_HEADS, 1) - cum
    r_all_t = r_all.T
    for h in range(M_HEADS):
        cf = 8 + 4 * d + h
        b_col = cum[:, cf:cf + 1]
        r_col = r_all[:, cf:cf + 1]
        r_row = r_all_t[cf:cf + 1, :]
        m_st = m_ref[h, 0:1, 0:1]
        rel = jnp.where(mask, r_row, NEG)
        a_col = jnp.maximum(m_st, jnp.max(rel, axis=1, keepdims=True))
        m_i = b_col + a_col
        w = jnp.exp(rel - a_col)
        w_inter = jnp.exp(m_st - a_col)
        b_end = cum[e_idx:e_idx + 1, cf:cf + 1]
        m_new = jnp.maximum(b_end + m_st, b_end + jnp.max(r_row, axis=1, keepdims=True))
        ws_col = jnp.exp(b_end + r_col - m_new)
        decay = jnp.exp(b_end + m_st - m_new)
        k = qkvo_ref[:, gw + h * LANES:gw + (h + 1) * LANES]
        v = qkvo_ref[:, 2 * gw + h * LANES:2 * gw + (h + 1) * LANES]
        vaug = jnp.concatenate([v, ones], axis=1)
        qk, qc = pb[h]
        s = (qk * w).astype(BF16)
        res = _dot(s, vaug) + w_inter * qc
        kws = (k.astype(F32) * ws_col).astype(BF16)
        caug_ref[h] = decay * caug_ref[h] + _dot_tn(kws, vaug)
        m_ref[h] = jnp.broadcast_to(m_new, (8, LANES))
        hh = res[:, 0:LANES] / jnp.maximum(jnp.abs(res[:, LANES:2 * LANES]), jnp.exp(-m_i))
        cols = slice(oc + h * LANES, oc + (h + 1) * LANES)
        if final:
            o_ref[:, cols] = (hh + yb_ref[:, cols]).astype(BF16)
        else:
            o_ref[:, cols] = hh.astype(BF16)


def _ssd_a(rev, L, gate_ref, prm_ref):
    dt_all = _softplus(gate_ref[...] + prm_ref[0:1, :])
    return dt_all, _tri_cumsum(_causal(L, rev).astype(BF16), dt_all * (-jnp.exp(prm_ref[1:2, :])))


def _ssd_b(xbc_ref, st_ref):
    gw = 4 * LANES
    out = []
    for g in range(S_G):
        bg = xbc_ref[:, gw + g * S_N:gw + (g + 1) * S_N]
        cg = xbc_ref[:, gw + S_G * S_N + g * S_N:gw + S_G * S_N + (g + 1) * S_N]
        out.append((_dot_nt(cg, bg), _dot(cg, st_ref[g].astype(BF16))))
    return out


def _ssd_c(rev, final, L, pa, pb, xbc_ref, st_ref, o_ref, oc, yb_ref, dsk_ref):
    d = 1 if rev else 0
    gw = 4 * LANES
    hpg = S_HEADS // S_G
    gcols = gw // S_G
    dt_all, cum = pa
    mask = _causal(L, rev)
    e_idx = 0 if rev else L - 1
    lane_head = lax.broadcasted_iota(jnp.int32, (1, gcols), 1) // (gcols // hpg)
    r_t = (cum - jnp.log(dt_all)).T
    for g in range(S_G):
        wts, e_cols, dec_cols, decays = [], [], [], []
        for hl in range(hpg):
            c = 16 + S_HEADS * d + g * hpg + hl
            b_col = cum[:, c:c + 1]
            b_end = cum[e_idx:e_idx + 1, c:c + 1]
            wts.append(jnp.exp(jnp.where(mask, b_col - r_t[c:c + 1, :], NEG)))
            e_cols.append(jnp.exp(b_col))
            dec_cols.append(dt_all[:, c:c + 1] * jnp.exp(b_end - b_col))
            decays.append(jnp.exp(b_end))
        gc = slice(g * gcols, (g + 1) * gcols)
        xs_g = xbc_ref[:, gc].astype(F32)
        bg = xbc_ref[:, gw + g * S_N:gw + (g + 1) * S_N]
        gm, inter = pb[g]
        acc = _expand4(e_cols, lane_head) * inter
        for hl in range(hpg):
            xh = jnp.where(lane_head == hl, xs_g, 0.0).astype(BF16)
            acc = acc + _dot((gm * wts[hl]).astype(BF16), xh)
        xdec = (xs_g * _expand4(dec_cols, lane_head)).astype(BF16)
        st_ref[g] = _expand4(decays, lane_head) * st_ref[g] + _dot_tn(bg, xdec)
        cols = slice(oc + g * gcols, oc + (g + 1) * gcols)
        if final:
            o_ref[:, cols] = (acc + yb_ref[:, cols] + dsk_ref[0:1, gc] * xs_g).astype(BF16)
        else:
            o_ref[:, cols] = acc.astype(BF16)


def _ret_b(qk_ref, qc, st_ref):
    gw = 4 * LANES
    out = []
    for h in range(R_HEADS):
        q = qk_ref[:, qc + h * LANES:qc + (h + 1) * LANES]
        k = qk_ref[:, qc + gw + h * LANES:qc + gw + (h + 1) * LANES]
        out.append((_dot_nt(q, k), _dot(q, st_ref[h].astype(BF16))))
    return out


def _ret_tables(rev, L, prm_ref, w_ref, e_ref, dec_ref):
    d = 1 if rev else 0
    mask = _causal(L, rev)
    icol = lax.broadcasted_iota(jnp.int32, (L, 1), 0).astype(F32)
    row = lax.broadcasted_iota(jnp.int32, (L, L), 0)
    col = lax.broadcasted_iota(jnp.int32, (L, L), 1)
    dist = ((col - row) if rev else (row - col)).astype(F32)
    lg_all = -jnp.exp(prm_ref[d:d + 1, :])
    for h in range(R_HEADS):
        lg = lg_all[:, h:h + 1]
        w_ref[h] = jnp.exp(jnp.where(mask, dist * lg, NEG))
        if rev:
            e_col = jnp.exp((float(L) - icol) * lg)
            dec_col = jnp.exp(icol * lg)
        else:
            e_col = jnp.exp((icol + 1.0) * lg)
            dec_col = jnp.exp((float(L - 1) - icol) * lg)
        e_ref[h] = jnp.broadcast_to(e_col, (L, LANES))
        dec_ref[h] = jnp.broadcast_to(dec_col, (L, LANES))


def _ret_c(rev, final, L, pb, qk_ref, qc, v_ref, prm_ref, tables, st_ref, o_ref, oc, yb_ref):
    d = 1 if rev else 0
    gw = 4 * LANES
    w_ref, e_ref, dec_ref = tables
    lg_all = -jnp.exp(prm_ref[d:d + 1, :])
    for h in range(R_HEADS):
        decay = jnp.exp(float(L) * lg_all[:, h:h + 1])
        hc = slice(h * LANES, (h + 1) * LANES)
        k = qk_ref[:, qc + gw + h * LANES:qc + gw + (h + 1) * LANES]
        v = v_ref[:, hc]
        qk, qst = pb[h]
        y = _dot((qk * w_ref[h]).astype(BF16), v) + e_ref[h] * qst
        kd = (k.astype(F32) * dec_ref[h]).astype(BF16)
        st_ref[h] = decay * st_ref[h] + _dot_tn(kd, v)
        cols = slice(oc + h * LANES, oc + (h + 1) * LANES)
        if final:
            o_ref[:, cols] = (y + yb_ref[:, cols]).astype(BF16)
        else:
            o_ref[:, cols] = y.astype(BF16)


def _gla_a(rev, L, tb, gate_ref, gw_ref, gb_ref):
    row = lax.broadcasted_iota(jnp.int32, (tb, tb), 0)
    col = lax.broadcasted_iota(jnp.int32, (tb, tb), 1)
    blockmask = jnp.logical_and((row // L) == (col // L), (col >= row) if rev else (col <= row))
    z = _dot_f32(gate_ref[...], gw_ref[...]) + gb_ref[...]
    return blockmask, _tri_cumsum(blockmask.astype(BF16), _log_sigmoid(z) * (1.0 / G_TAU))


def _gla_c(rev, final, L, nsub, pa, qk_ref, v_ref, st_ref, o_ref, oc, yb_ref):
    kw = 2 * LANES
    blockmask, cum = pa
    e_idx = 0 if rev else L - 1
    lane_head = lax.broadcasted_iota(jnp.int32, (1, kw), 1) // (kw // G_HEADS)
    sub = [slice(c * L, (c + 1) * L) for c in range(nsub)]
    qf = qk_ref[:, 0:kw].astype(F32)
    kf = qk_ref[:, kw:2 * kw].astype(F32)
    qg = qf * jnp.exp(cum)
    kg = (kf * jnp.exp(-cum)).astype(BF16)
    b_ends = [cum[c * L + e_idx:c * L + e_idx + 1, :] for c in range(nsub)]
    kd = jnp.concatenate([kf[sub[c], :] * jnp.exp(b_ends[c] - cum[sub[c], :]) for c in range(nsub)], axis=0)
    qgh = [jnp.where(lane_head == h, qg, 0.0).astype(BF16) for h in range(G_HEADS)]
    kdh = [jnp.where(lane_head == h, kd, 0.0).astype(BF16) for h in range(G_HEADS)]
    s_all = [_dot_nt(qgh[h], kg) for h in range(G_HEADS)]
    upd = [sum(_dot_tn(v_ref[sub[c], h * LANES:(h + 1) * LANES], kdh[h][sub[c], :]) for h in range(G_HEADS))
           for c in range(nsub)]
    st = st_ref[...]
    st_at = [None] * nsub
    for c in (reversed(range(nsub)) if rev else range(nsub)):
        st_at[c] = st.astype(BF16)
        st = jnp.exp(b_ends[c]) * st + upd[c]
    st_ref[...] = st
    for h in range(G_HEADS):
        hc = slice(h * LANES, (h + 1) * LANES)
        cols = slice(oc + h * LANES, oc + (h + 1) * LANES)
        inter = jnp.concatenate([_dot_nt(qgh[h][sub[c], :], st_at[c]) for c in range(nsub)], axis=0)
        y = _dot(jnp.where(blockmask, s_all[h], 0.0).astype(BF16), v_ref[:, hc]) + inter
        if final:
            o_ref[:, cols] = (y + yb_ref[:, cols]).astype(BF16)
        else:
            o_ref[:, cols] = y.astype(BF16)


def _scan_kernel(*refs, rev, final, tb):
    (um_m, ug, prep, r_v, g_qk, g_v, m_bias, s_prm, r_prm, gwp, gb) = refs[:11]
    if final:
        yb, dsk = refs[11:13]
        o_ref = refs[13]
        scratch = refs[14:]
    else:
        yb = dsk = None
        o_ref = refs[11]
        scratch = refs[12:]
    caug, m_st, s_st, r_st, g_st = scratch[:5]
    r_tables = scratch[5:]
    gw = 4 * LANES

    @pl.when(pl.program_id(1) == 0)
    def _():
        for ref in scratch[:5]:
            ref[...] = jnp.zeros_like(ref)
        _ret_tables(rev, tb, r_prm, *r_tables)

    r_b = _ret_b(prep, 2 * gw, r_st)
    m_b = _mlstm_b(um_m, caug)
    s_b = _ssd_b(prep, s_st)
    m_a = _mlstm_a(rev, tb, ug, m_bias)
    s_a = _ssd_a(rev, tb, ug, s_prm)
    g_a = _gla_a(rev, GLA_CHUNK, tb, ug, gwp, gb)
    _ret_c(rev, final, tb, r_b, prep, 2 * gw, r_v, r_prm, r_tables, r_st, o_ref, 2 * gw, yb)
    _gla_c(rev, final, GLA_CHUNK, tb // GLA_CHUNK, g_a, g_qk, g_v, g_st, o_ref, 3 * gw, yb)
    _mlstm_c(rev, final, tb, m_a, m_b, um_m, caug, m_st, o_ref, 0, yb)
    _ssd_c(rev, final, tb, s_a, s_b, prep, s_st, o_ref, gw, yb, dsk)


def _scan_call(rev, final, b, tb, nctx, um, ug, prep, params, yb, dsk):
    t_rows = um.shape[0]
    gw = 4 * LANES
    nblk = t_rows // (b * tb)
    fb = _flat_block(b, nctx, nblk)
    if rev:
        def order(c):
            return jnp.where(c < nctx, nctx - 1 - c, nblk + nctx - 1 - c)
    else:
        def order(c):
            return c

    def tok(width, cb):
        return pl.BlockSpec((tb, width), lambda i, c: (fb(i, order(c)), cb))

    def full(arr):
        return pl.BlockSpec(arr.shape, lambda i, c: (0,) * arr.ndim)

    ins = [um, ug, prep, um, um, um] + list(params)
    in_specs = [tok(4 * gw, 0), tok(GATE_W, 0), tok(4 * gw, 0), tok(gw, 8), tok(gw, 11), tok(gw, 12)]
    in_specs += [full(p) for p in params]
    if final:
        ins += [yb, dsk]
        in_specs += [tok(4 * gw, 0), full(dsk)]
    scratch = [pltpu.VMEM((M_HEADS, LANES, 2 * LANES), F32), pltpu.VMEM((M_HEADS, 8, LANES), F32),
               pltpu.VMEM((S_G, S_N, gw // S_G), F32), pltpu.VMEM((R_HEADS, LANES, LANES), F32),
               pltpu.VMEM((LANES, 2 * LANES), F32),
               pltpu.VMEM((R_HEADS, tb, tb), F32), pltpu.VMEM((R_HEADS, tb, LANES), F32),
               pltpu.VMEM((R_HEADS, tb, LANES), F32)]
    return pl.pallas_call(
        functools.partial(_scan_kernel, rev=rev, final=final, tb=tb),
        out_shape=jax.ShapeDtypeStruct((t_rows, 4 * gw), BF16),
        grid=(b, nblk),
        in_specs=in_specs,
        out_specs=tok(4 * gw, 0),
        scratch_shapes=scratch,
        compiler_params=_cparams(("parallel", "arbitrary"), 48),
        name="scan_fwd" if final else "scan_bwd",
    )(*ins)


def _in_col_layout(d_model):
    gw = d_model // N_GROUPS
    conv_ch = gw + 2 * S_G * S_N
    names = [('m_q', gw), ('m_k', gw), ('m_v', gw), ('m_o', gw), ('m_i', 2 * M_HEADS), ('m_f', 2 * M_HEADS),
             ('s_z', gw), ('s_xbc', conv_ch), ('s_dt', 2 * S_HEADS),
             ('r_q', gw), ('r_k', gw), ('r_v', gw), ('r_g', gw),
             ('g_q', gw // 2), ('g_k', gw // 2), ('g_v', gw), ('g_g', gw), ('g_a', 2 * G_RANK)]
    off, o = {}, 0
    for nm, n in names:
        off[nm] = (o, n)
        o += n
    return off


def kernel(x, c, ctx, c_ctx, ada_w, ada_b, w_in, m_ig_b, m_fg_b, m_norm, s_conv_w, s_conv_b, s_dt_bias, s_a_log, s_d, s_norm, r_decay, r_norm, g_gate_w, g_gate_b, g_norm, w_out, post_g, post_b, ffn_w_up, ffn_w_down):
    b, seq, d = x.shape
    n_ctx_tok = ctx.shape[1]
    depth = ada_w.shape[0]
    gw = d // N_GROUPS
    assert gw == 4 * LANES and b + 1 <= 8
    s_tot = n_ctx_tok + seq
    t = b * s_tot
    tb = math.gcd(MAX_TOKEN_BLOCK, math.gcd(n_ctx_tok, seq))
    assert tb % 16 == 0 and tb % GLA_CHUNK == 0
    nctx = n_ctx_tok // tb
    nblk = s_tot // tb
    nsb = t // tb
    t_ctx = b * n_ctx_tok
    t_lat = b * seq

    def row_tile(cap):
        return max(m for m in (1024, 512, 256, 128, 64, 32, 16)
                   if m <= cap and t_ctx % m == 0 and t_lat % m == 0 and m % tb == 0)

    tm = row_tile(1024)
    tm_out = row_tile(512)
    alpha = (2.0 * depth) ** 0.25

    quarter = LANES // 4
    n_rows = seq // GRID_W
    freqs = 1.0 / (ROPE_BASE ** (jnp.arange(quarter, dtype=F32) / quarter))
    ang_r = jnp.arange(n_rows, dtype=F32)[:, None] * freqs[None, :]
    ang_c = jnp.arange(GRID_W, dtype=F32)[:, None] * freqs[None, :]
    cos_r, sin_r = (jnp.repeat(f(ang_r), GRID_W, axis=0) for f in (jnp.cos, jnp.sin))
    cos_c, sin_c = (jnp.tile(f(ang_c), (n_rows, 1)) for f in (jnp.cos, jnp.sin))
    cos_t = jnp.concatenate([cos_r, cos_r, cos_c, cos_c], axis=1)
    sin_t = jnp.concatenate([-sin_r, sin_r, -sin_c, sin_c], axis=1)
    rope = jnp.concatenate([
        jnp.concatenate([jnp.ones((n_ctx_tok, LANES), F32), jnp.zeros((n_ctx_tok, LANES), F32)], axis=1),
        jnp.concatenate([cos_t, sin_t], axis=1)], axis=0)

    xs = (ctx.reshape(t_ctx, d), x.reshape(t_lat, d))
    cc =jnp.concatenate([c, c_ctx[None, :], jnp.zeros((8 - b - 1, d), F32)], axis=0)
    mod = _mod_call(cc, ada_w, ada_b)
    sb = np.arange(nsb)
    mod_row = np.where(sb < b * nctx, b, (sb - b * nctx) // (nblk - nctx))

    def lane_rows(vecs, start, n_rows):
        rows = jnp.stack(vecs)
        return jnp.pad(rows, ((0, n_rows - rows.shape[0]), (start, LANES - start - rows.shape[1])))

    wm, wg = _win_call(w_in)
    w_out_b = w_out.astype(BF16)
    w_up_b = ffn_w_up.astype(BF16)
    w_down_b = ffn_w_down.astype(BF16)
    for l in range(depth):
        row_scale = np.array([1.0, 1.0, 1.0 / alpha, 1.0, 1.0, 1.0 / alpha], np.float32)[None, :, None]
        mods = jnp.pad(mod[l][mod_row].reshape(nsb, 6, d) * row_scale, ((0, 0), (0, 2), (0, 0)))
        um, ug = _in_call(xs, mods, wm, wg, l, tm, tb)

        conv_w = jnp.pad(s_conv_w[l], ((0, 8 - CONV_W), (0, 0)))
        prep = _prep_call(um, conv_w, s_conv_b[l][None, :], rope, b, nctx, tb)

        m_bias = lane_rows([jnp.concatenate([m_ig_b[l].reshape(-1), m_fg_b[l].reshape(-1)])], 0, 1)
        s_prm = lane_rows([s_dt_bias[l].reshape(-1), s_a_log[l].reshape(-1)], 16, 8)
        r_prm = lane_rows([r_decay[l][0], r_decay[l][1]], 0, 8)
        nrm = jnp.concatenate([jnp.stack([m_norm[l], s_norm[l], r_norm[l], g_norm[l]]), jnp.zeros((4, gw), F32)])
        dsk = jnp.repeat(s_d[l], gw // S_HEADS)[None, :]

        def dir_params(dd):
            lo = 32 + G_RANK * dd
            gwp = jnp.pad(g_gate_w[l][dd], ((lo, GATE_W - lo - G_RANK), (0, 0)))
            return [m_bias, s_prm, r_prm, gwp, g_gate_b[l][dd][None, :]]

        yb = _scan_call(True, False, b, tb, nctx, um, ug, prep, dir_params(1), None, None)
        ysum = _scan_call(False, True, b, tb, nctx, um, ug, prep, dir_params(0), yb, dsk)

        skip = t_ctx if l == depth - 1 else 0
        x1 = _out_call(ysum, um, nrm, w_out_b, l, xs, mods, skip // tm_out,
                       post_g[l, 0][None, :], post_b[l, 0][None, :], tm_out, tb, alpha)
        act = _up_call(x1, mods, skip // tm, w_up_b, l, tm, tb)
        xs = (_res_call(act, w_down_b, l, x1, mods, skip // tb,
                        post_g[l, 1][None, :], post_b[l, 1][None, :], tb, tb, 5, alpha, "ffn_down_res_ln"),)

    return xs[0].reshape(b, seq, d)
```
